```python
import jax
import jax.numpy as jnp
from jax import lax
import numpy as np

D_MODEL = 2048
BATCH = 1
SEQ = 16384
DEPTH = 1

EPS = 1e-6

ATTN_Q_HEADS = 8
ATTN_KV_HEADS = 2
ATTN_HEAD_DIM = 128
ATTN_GROUP = ATTN_Q_HEADS // ATTN_KV_HEADS
WINDOW = 128
ATTN_BLOCK = 128
ROPE_THETA = 500000.0
ROPE_DIM = ATTN_HEAD_DIM // 4
ATTN_WIDTH = ATTN_Q_HEADS * ATTN_HEAD_DIM

GLA_HEADS = 4
GLA_DK = 128
GLA_DV = 256
GLA_GATE_RANK = 16
GLA_GATE_TAU = 16.0
GLA_CHUNK = 64
GLA_WIDTH = GLA_HEADS * GLA_DV

MIX_WIDTH = ATTN_WIDTH + GLA_WIDTH

IN_SPLIT_SIZES = (ATTN_WIDTH, ATTN_KV_HEADS * ATTN_HEAD_DIM, ATTN_KV_HEADS * ATTN_HEAD_DIM,
                  GLA_HEADS * GLA_DK, GLA_HEADS * GLA_DK, GLA_WIDTH, GLA_WIDTH,
                  GLA_GATE_RANK, GLA_GATE_RANK)
IN_WIDTH = sum(IN_SPLIT_SIZES)
IN_SPLIT_POINTS = tuple(int(v) for v in np.cumsum(IN_SPLIT_SIZES)[:-1])

N_GROUPS = 4
EXPERTS_PER_GROUP = 8
N_EXPERTS = N_GROUPS * EXPERTS_PER_GROUP
TOP_K = 2
D_FF_EXPERT = 1024
MOE_BLOCK = 128

kernel_name = 'hymba_swa_gla_hier_moe'


def rmsnorm(x, w):
    xf = x.astype(jnp.float32)
    y = xf * lax.rsqrt(jnp.mean(xf * xf, axis=-1, keepdims=True) + EPS)
    return (y * w.astype(jnp.float32)).astype(x.dtype)


def partial_rope(x, pos):
    half = ROPE_DIM // 2
    inv_freq = jnp.power(jnp.float32(ROPE_THETA), -jnp.arange(half, dtype=jnp.float32) * (2.0 / ROPE_DIM))
    ang = pos.astype(jnp.float32)[:, None] * inv_freq[None, :]
    cos = jnp.cos(ang)[None, :, None, :]
    sin = jnp.sin(ang)[None, :, None, :]
    xf = x.astype(jnp.float32)
    x1 = xf[..., :half]
    x2 = xf[..., half:ROPE_DIM]
    out = jnp.concatenate([x1 * cos - x2 * sin, x2 * cos + x1 * sin, xf[..., ROPE_DIM:]], axis=-1)
    return out.astype(x.dtype)


def windowed_gqa(q, k, v, sink):
    B, S = q.shape[0], q.shape[1]
    nb = S // ATTN_BLOCK
    qb = q.reshape(B, nb, ATTN_BLOCK, ATTN_KV_HEADS, ATTN_GROUP, ATTN_HEAD_DIM)

    def band(t):
        tp = jnp.pad(t, ((0, 0), (ATTN_BLOCK, ATTN_BLOCK), (0, 0), (0, 0)))
        tp = tp.reshape(B, nb + 2, ATTN_BLOCK, ATTN_KV_HEADS, ATTN_HEAD_DIM)
        return jnp.concatenate([tp[:, :-2], tp[:, 1:-1], tp[:, 2:]], axis=2)

    kb = band(k)
    vb = band(v)
    s = jnp.einsum('bnqhgd,bnkhd->bnhgqk', qb, kb,
                   preferred_element_type=jnp.float32) * (ATTN_HEAD_DIM ** -0.5)
    blk = jnp.arange(nb)
    qpos = blk[:, None] * ATTN_BLOCK + jnp.arange(ATTN_BLOCK)[None, :]
    kpos = (blk[:, None] - 1) * ATTN_BLOCK + jnp.arange(3 * ATTN_BLOCK)[None, :]
    rel = kpos[:, None, :] - qpos[:, :, None]
    valid = (jnp.abs(rel) <= WINDOW) & (kpos[:, None, :] >= 0) & (kpos[:, None, :] < S)
    s = jnp.where(valid[None, :, None, None], s, -jnp.inf)
    sink_col = jnp.broadcast_to(
        sink.astype(jnp.float32).reshape(1, 1, ATTN_KV_HEADS, ATTN_GROUP, 1, 1), s.shape[:-1] + (1,))
    p = jax.nn.softmax(jnp.concatenate([s, sink_col], axis=-1), axis=-1)[..., :-1]
    o = jnp.einsum('bnhgqk,bnkhd->bnqhgd', p.astype(v.dtype), vb)
    return o.reshape(B, S, ATTN_WIDTH)


def gla_chunked(q, k, v, log_a):
    B, S, H, dk = q.shape
    dv = v.shape[-1]
    n = S // GLA_CHUNK

    def chunks(t):
        return t.reshape(B, n, GLA_CHUNK, H, t.shape[-1]).transpose(0, 3, 1, 2, 4).astype(jnp.float32)

    q, k, v, log_a = chunks(q), chunks(k), chunks(v), chunks(log_a)
    b = jnp.cumsum(log_a, axis=-2)
    q_dec = q * jnp.exp(b)
    k_inv = k * jnp.exp(-b)
    k_end = k * jnp.exp(b[..., -1:, :] - b)
    tri = jnp.tril(jnp.ones((GLA_CHUNK, GLA_CHUNK), dtype=bool))
    att = jnp.where(tri, jnp.einsum('bhncd,bhnjd->bhncj', q_dec, k_inv), 0.0)
    o_intra = jnp.einsum('bhncj,bhnjv->bhncv', att, v)
    d_state = jnp.einsum('bhncd,bhncv->bhndv', k_end, v)
    chunk_decay = jnp.exp(b[..., -1, :])

    def step(state, inp):
        decay_n, ds_n = inp
        return decay_n[..., None] * state + ds_n, state

    state0 = jnp.zeros((B, H, dk, dv), jnp.float32)
    _, s_in = lax.scan(step, state0, (jnp.moveaxis(chunk_decay, 2, 0), jnp.moveaxis(d_state, 2, 0)))
    s_in = jnp.moveaxis(s_in, 0, 2)
    o = o_intra + jnp.einsum('bhncd,bhndv->bhncv', q_dec, s_in)
    return o.transpose(0, 2, 3, 1, 4).reshape(B, S, H, dv)


def gla_bidirectional(q, k, v, r, lr_f, lr_b, up_f, bias_f, up_b, bias_b, out_norm_w):
    B, S = q.shape[0], q.shape[1]
    q = q.reshape(B, S, GLA_HEADS, GLA_DK) * (GLA_DK ** -0.5)
    k = k.reshape(B, S, GLA_HEADS, GLA_DK)
    v = v.reshape(B, S, GLA_HEADS, GLA_DV)

    def log_gate(lr, up, bias):
        g = jnp.einsum('bsr,rc->bsc', lr, up) + bias
        return (jax.nn.log_sigmoid(g.astype(jnp.float32)) / GLA_GATE_TAU).reshape(B, S, GLA_HEADS, GLA_DK)

    flip = lambda t: jnp.flip(t, axis=1)
    o_f = gla_chunked(q, k, v, log_gate(lr_f, up_f, bias_f))
    o_b = flip(gla_chunked(flip(q), flip(k), flip(v), flip(log_gate(lr_b, up_b, bias_b))))
    o = rmsnorm(o_f + o_b, out_norm_w).reshape(B, S, GLA_WIDTH)
    return (o * jax.nn.silu(r.astype(jnp.float32))).astype(r.dtype)


def hybrid_mixer(x, ln_w, w_in, q_norm_w, k_norm_w, sink, attn_norm_w,
                 gate_up_f, gate_bias_f, gate_up_b, gate_bias_b, gla_norm_w, w_out):
    B, S, _ = x.shape
    xn = rmsnorm(x, ln_w)
    proj = xn @ w_in
    aq, ak, av, gq, gk, gv, gr, lrf, lrb = jnp.split(proj, IN_SPLIT_POINTS, axis=-1)
    pos = jnp.arange(S)
    aq = partial_rope(rmsnorm(aq.reshape(B, S, ATTN_Q_HEADS, ATTN_HEAD_DIM), q_norm_w), pos)
    ak = partial_rope(rmsnorm(ak.reshape(B, S, ATTN_KV_HEADS, ATTN_HEAD_DIM), k_norm_w), pos)
    av = av.reshape(B, S, ATTN_KV_HEADS, ATTN_HEAD_DIM)
    attn = rmsnorm(windowed_gqa(aq, ak, av, sink), attn_norm_w)
    gla = gla_bidirectional(gq, gk, gv, gr, lrf, lrb, gate_up_f, gate_bias_f,
                            gate_up_b, gate_bias_b, gla_norm_w)
    mixed = jnp.concatenate([attn, gla.astype(attn.dtype)], axis=-1)
    return x + mixed @ w_out


def routed_experts(xn, expert_id, weights, w_gate, w_up, w_down):
    T, D = xn.shape
    A = T * TOP_K
    flat_e = expert_id.reshape(A)
    flat_tok = jnp.repeat(jnp.arange(T, dtype=jnp.int32), TOP_K)
    flat_w = weights.reshape(A).astype(jnp.float32)
    counts = jnp.bincount(flat_e, length=N_EXPERTS)
    padded = (counts + MOE_BLOCK - 1) // MOE_BLOCK * MOE_BLOCK
    pad_end = jnp.cumsum(padded)
    pad_start = pad_end - padded
    start = jnp.cumsum(counts) - counts
    order = jnp.argsort(flat_e)
    sorted_e = flat_e[order]
    slot = pad_start[sorted_e] + jnp.arange(A) - start[sorted_e]
    n_blocks = -(-A // MOE_BLOCK) + N_EXPERTS
    n_slots = n_blocks * MOE_BLOCK
    slot_tok = jnp.zeros((n_slots,), jnp.int32).at[slot].set(flat_tok[order])
    slot_w = jnp.zeros((n_slots,), jnp.float32).at[slot].set(flat_w[order])
    block_e = jnp.minimum(jnp.searchsorted(pad_end, jnp.arange(n_blocks) * MOE_BLOCK, side='right'),
                          N_EXPERTS - 1)
    xs = xn[slot_tok].reshape(n_blocks, MOE_BLOCK, D)

    def expert_block(args):
        xb, e = args
        hid = jax.nn.silu(xb @ w_gate[e]) * (xb @ w_up[e])
        return hid @ w_down[e]

    ys = lax.map(expert_block, (xs, block_e)).reshape(n_slots, D)
    ys = ys * slot_w[:, None].astype(ys.dtype)
    return jnp.zeros((T, D), ys.dtype).at[slot_tok].add(ys)


def hier_moe(h, ln_w, w_group, b_group, w_router, b_router, w_gate, w_up, w_down):
    B, S, D = h.shape
    T = B * S
    xn = rmsnorm(h, ln_w).reshape(T, D)
    group_logits = (xn @ w_group).astype(jnp.float32) + b_group.astype(jnp.float32)
    group_p = jax.nn.softmax(group_logits, axis=-1)
    g_sel = jnp.argmax(group_logits, axis=-1)
    g_gate = jnp.take_along_axis(group_p, g_sel[:, None], axis=-1)
    e_logits = ((xn @ w_router).astype(jnp.float32) + b_router.astype(jnp.float32)).reshape(
        T, N_GROUPS, EXPERTS_PER_GROUP)
    in_group = jnp.take_along_axis(e_logits, g_sel[:, None, None], axis=1)[:, 0]
    top_vals, top_idx = lax.top_k(in_group, TOP_K)
    weights = g_gate * jax.nn.softmax(top_vals, axis=-1)
    expert_id = g_sel[:, None].astype(jnp.int32) * EXPERTS_PER_GROUP + top_idx.astype(jnp.int32)
    y = routed_experts(xn, expert_id, weights, w_gate, w_up, w_down)
    return h + y.reshape(B, S, D).astype(h.dtype)


def setup_inputs(seed: int = 0) -> dict:
    key = jax.random.key(seed)
    ks = jax.random.split(key, 21)
    f32 = jnp.float32
    nrm = lambda k, shape, scale: jax.random.normal(k, shape, f32) * scale
    gain = lambda k, shape: 1.0 + 0.02 * jax.random.normal(k, shape, f32)
    return {
        'x': nrm(ks[0], (BATCH, SEQ, D_MODEL), 1.0),
        'ln1_w': gain(ks[1], (DEPTH, D_MODEL)),
        'w_in': nrm(ks[2], (DEPTH, D_MODEL, IN_WIDTH), D_MODEL ** -0.5),
        'q_norm_w': gain(ks[3], (DEPTH, ATTN_HEAD_DIM)),
        'k_norm_w': gain(ks[4], (DEPTH, ATTN_HEAD_DIM)),
        'attn_sink': nrm(ks[5], (DEPTH, ATTN_Q_HEADS), 0.5),
        'attn_out_norm_w': gain(ks[6], (DEPTH, ATTN_WIDTH)),
        'gla_gate_up_f': nrm(ks[7], (DEPTH, GLA_GATE_RANK, GLA_HEADS * GLA_DK), GLA_GATE_RANK ** -0.5),
        'gla_gate_bias_f': nrm(ks[8], (DEPTH, GLA_HEADS * GLA_DK), 0.02),
        'gla_gate_up_b': nrm(ks[9], (DEPTH, GLA_GATE_RANK, GLA_HEADS * GLA_DK), GLA_GATE_RANK ** -0.5),
        'gla_gate_bias_b': nrm(ks[10], (DEPTH, GLA_HEADS * GLA_DK), 0.02),
        'gla_out_norm_w': gain(ks[11], (DEPTH, GLA_DV)),
        'w_out': nrm(ks[12], (DEPTH, MIX_WIDTH, D_MODEL), MIX_WIDTH ** -0.5),
        'ln2_w': gain(ks[13], (DEPTH, D_MODEL)),
        'w_group': nrm(ks[14], (DEPTH, D_MODEL, N_GROUPS), D_MODEL ** -0.5),
        'b_group': nrm(ks[15], (DEPTH, N_GROUPS), 0.01),
        'w_router': nrm(ks[16], (DEPTH, D_MODEL, N_EXPERTS), D_MODEL ** -0.5),
        'b_router': nrm(ks[17], (DEPTH, N_EXPERTS), 0.01),
        'w_gate_e': nrm(ks[18], (DEPTH, N_EXPERTS, D_MODEL, D_FF_EXPERT), D_MODEL ** -0.5),
        'w_up_e': nrm(ks[19], (DEPTH, N_EXPERTS, D_MODEL, D_FF_EXPERT), D_MODEL ** -0.5),
        'w_down_e': nrm(ks[20], (DEPTH, N_EXPERTS, D_FF_EXPERT, D_MODEL), D_FF_EXPERT ** -0.5),
    }


def reference(x, ln1_w, w_in, q_norm_w, k_norm_w, attn_sink, attn_out_norm_w,
              gla_gate_up_f, gla_gate_bias_f, gla_gate_up_b, gla_gate_bias_b, gla_out_norm_w,
              w_out, ln2_w, w_group, b_group, w_router, b_router, w_gate_e, w_up_e, w_down_e):
    h = x
    for l in range(DEPTH):
        h = hybrid_mixer(h, ln1_w[l], w_in[l], q_norm_w[l], k_norm_w[l], attn_sink[l],
                         attn_out_norm_w[l], gla_gate_up_f[l], gla_gate_bias_f[l],
                         gla_gate_up_b[l], gla_gate_bias_b[l], gla_out_norm_w[l], w_out[l])
        h = hier_moe(h, ln2_w[l], w_group[l], b_group[l], w_router[l], b_router[l],
                     w_gate_e[l], w_up_e[l], w_down_e[l])
    return h
```

```python
import functools

import jax
import jax.numpy as jnp
from jax import lax
from jax.experimental import pallas as pl
from jax.experimental.pallas import tpu as pltpu

F32 = jnp.float32
BF16 = jnp.bfloat16

EPS = 1e-6
D_MODEL = 2048

ATTN_Q_HEADS = 8
ATTN_KV_HEADS = 2
ATTN_GROUP = ATTN_Q_HEADS // ATTN_KV_HEADS
HEAD_DIM = 128
WINDOW = 128
ATTN_BLOCK = 128
ROPE_THETA = 500000.0
ROPE_DIM = HEAD_DIM // 4
ROPE_HALF = ROPE_DIM // 2
ATTN_WIDTH = ATTN_Q_HEADS * HEAD_DIM
KV_WIDTH = ATTN_KV_HEADS * HEAD_DIM
A_WIDTH = ATTN_WIDTH + 2 * KV_WIDTH

GLA_HEADS = 4
GLA_DK = 128
GLA_DV = 256
GLA_RANK = 16
GLA_TAU = 16.0
GLA_QK_WIDTH = GLA_HEADS * GLA_DK
GLA_WIDTH = GLA_HEADS * GLA_DV
GLA_CHUNK = 128

N_GROUPS = 4
EXPERTS_PER_GROUP = 8
N_EXPERTS = N_GROUPS * EXPERTS_PER_GROUP
TOP_K = 2
D_FF = 1024
ROUTER_ROWS = 128
EXPERT_ROW0 = 8

PROJ_ROWS = 512
OUT_ROWS = 256
MOE_ROWS = 256
DISPATCH_ROWS = 512
COMBINE_ROWS = 256

VMEM_LIMIT = 56 * 1024 * 1024


def _dot(a, b):
    return jnp.dot(a, b, preferred_element_type=F32)


def _dot_nt(a, b):
    return lax.dot_general(a, b, (((1,), (1,)), ((), ())), preferred_element_type=F32)


def _dot_tn(a, b):
    return lax.dot_general(a, b, (((0,), (0,)), ((), ())), preferred_element_type=F32)


def _resident(shape):
    nd = len(shape)
    return pl.BlockSpec(shape, lambda *_: (0,) * nd, pipeline_mode=pl.Buffered(1))


def _inproj_body(x_ref, ln_ref, wa_ref, wqk_ref, wv_ref, wr_ref, wl_ref, qn_ref, kn_ref,
                 cos_ref, sa_ref, sb_ref, oa_ref, oqk_ref, ov_ref, or_ref, ol_ref):
    x = x_ref[...]
    ms = jnp.mean(x * x, axis=-1, keepdims=True)
    xn = (x * lax.rsqrt(ms + EPS) * ln_ref[...]).astype(BF16)
    oqk_ref[...] = _dot(xn, wqk_ref[...]).astype(BF16)
    ov_ref[...] = _dot(xn, wv_ref[...]).astype(BF16)
    or_ref[...] = _dot(xn, wr_ref[...]).astype(BF16)
    ol_ref[...] = _dot(xn, wl_ref[...])
    acc = _dot(xn, wa_ref[...])
    cos, sa, sb = cos_ref[...], sa_ref[...], sb_ref[...]
    for c in range(ATTN_Q_HEADS + ATTN_KV_HEADS):
        xh = acc[:, c * HEAD_DIM:(c + 1) * HEAD_DIM]
        w = qn_ref[...] if c < ATTN_Q_HEADS else kn_ref[...]
        y = xh * lax.rsqrt(jnp.mean(xh * xh, axis=-1, keepdims=True) + EPS) * w
        y = (y * cos + pltpu.roll(y, ROPE_HALF, 1) * sa
             + pltpu.roll(y, HEAD_DIM - ROPE_HALF, 1) * sb)
        if c < ATTN_Q_HEADS:
            y = y * (HEAD_DIM ** -0.5)
        oa_ref[:, c * HEAD_DIM:(c + 1) * HEAD_DIM] = y.astype(BF16)
    oa_ref[:, ATTN_WIDTH + KV_WIDTH:] = acc[:, ATTN_WIDTH + KV_WIDTH:].astype(BF16)


def _inproj(x2, ln_w, wa, wqk, wv, wr, wl, qn, kn, cos_t, sa_t, sb_t):
    T = x2.shape[0]
    tm = PROJ_ROWS
    row = lambda w: pl.BlockSpec((tm, w), lambda i: (i, 0))
    return pl.pallas_call(
        _inproj_body,
        grid=(T // tm,),
        in_specs=[row(D_MODEL), _resident((1, D_MODEL)),
                  _resident(wa.shape), _resident(wqk.shape), _resident(wv.shape),
                  _resident(wr.shape), _resident(wl.shape),
                  _resident((1, HEAD_DIM)), _resident((1, HEAD_DIM)),
                  row(HEAD_DIM), row(HEAD_DIM), row(HEAD_DIM)],
        out_specs=[row(A_WIDTH), row(2 * GLA_QK_WIDTH), row(GLA_WIDTH), row(GLA_WIDTH),
                   row(2 * GLA_RANK)],
        out_shape=[jax.ShapeDtypeStruct((T, A_WIDTH), BF16),
                   jax.ShapeDtypeStruct((T, 2 * GLA_QK_WIDTH), BF16),
                   jax.ShapeDtypeStruct((T, GLA_WIDTH), BF16),
                   jax.ShapeDtypeStruct((T, GLA_WIDTH), BF16),
                   jax.ShapeDtypeStruct((T, 2 * GLA_RANK), F32)],
        compiler_params=pltpu.CompilerParams(dimension_semantics=("parallel",),
                                             vmem_limit_bytes=VMEM_LIMIT),
        name="inproj",
    )(x2, ln_w, wa, wqk, wv, wr, wl, qn, kn, cos_t, sa_t, sb_t)


def _attn_body(sink_ref, q_ref, kp_ref, kc_ref, kn_ref, vp_ref, vc_ref, vn_ref, nw_ref, o_ref):
    n = pl.program_id(0)
    nb = pl.num_programs(0)
    rows = ATTN_GROUP * ATTN_BLOCK
    keys = 3 * ATTN_BLOCK
    r = lax.broadcasted_iota(jnp.int32, (rows, keys), 0) & (ATTN_BLOCK - 1)
    c = lax.broadcasted_iota(jnp.int32, (rows, keys), 1)
    valid = (c >= r + (ATTN_BLOCK - WINDOW)) & (c <= r + (ATTN_BLOCK + WINDOW))
    valid = valid & ((c >= ATTN_BLOCK) | (n > 0)) & ((c < 2 * ATTN_BLOCK) | (n < nb - 1))
    outs = []
    for g in range(ATTN_KV_HEADS):
        ks = slice(g * HEAD_DIM, (g + 1) * HEAD_DIM)
        k3 = jnp.concatenate([kp_ref[:, ks], kc_ref[:, ks], kn_ref[:, ks]], axis=0)
        v3 = jnp.concatenate([vp_ref[:, ks], vc_ref[:, ks], vn_ref[:, ks]], axis=0)
        heads = range(g * ATTN_GROUP, (g + 1) * ATTN_GROUP)
        q4 = jnp.concatenate([q_ref[:, h * HEAD_DIM:(h + 1) * HEAD_DIM] for h in heads], axis=0)
        s = jnp.where(valid, _dot_nt(q4, k3), -jnp.inf)
        sink = jnp.concatenate([jnp.full((ATTN_BLOCK, 1), sink_ref[h], F32) for h in heads], axis=0)
        m = jnp.maximum(jnp.max(s, axis=-1, keepdims=True), sink)
        e = jnp.exp(s - m)
        denom = jnp.sum(e, axis=-1, keepdims=True) + jnp.exp(sink - m)
        o = _dot(e.astype(BF16), v3) / denom
        outs.extend(o[i * ATTN_BLOCK:(i + 1) * ATTN_BLOCK] for i in range(ATTN_GROUP))
    a = jnp.concatenate(outs, axis=1)
    a = a * lax.rsqrt(jnp.mean(a * a, axis=-1, keepdims=True) + EPS) * nw_ref[...]
    o_ref[...] = a.astype(BF16)


def _attention(pa, sink, norm_w):
    T = pa.shape[0]
    nb = T // ATTN_BLOCK
    kcol = ATTN_WIDTH // KV_WIDTH
    vcol = kcol + 1
    prev = lambda n, _: jnp.maximum(n - 1, 0)
    cur = lambda n, _: n
    nxt = lambda n, _: jnp.minimum(n + 1, nb - 1)
    kv = lambda f, col: pl.BlockSpec((ATTN_BLOCK, KV_WIDTH), lambda n, s: (f(n, s), col))
    grid_spec = pltpu.PrefetchScalarGridSpec(
        num_scalar_prefetch=1,
        grid=(nb,),
        in_specs=[pl.BlockSpec((ATTN_BLOCK, ATTN_WIDTH), lambda n, s: (n, 0)),
                  kv(prev, kcol), kv(cur, kcol), kv(nxt, kcol),
                  kv(prev, vcol), kv(cur, vcol), kv(nxt, vcol),
                  pl.BlockSpec((1, ATTN_WIDTH), lambda n, s: (0, 0))],
        out_specs=pl.BlockSpec((ATTN_BLOCK, ATTN_WIDTH), lambda n, s: (n, 0)),
    )
    return pl.pallas_call(
        _attn_body,
        grid_spec=grid_spec,
        out_shape=jax.ShapeDtypeStruct((T, ATTN_WIDTH), BF16),
        compiler_params=pltpu.CompilerParams(dimension_semantics=("parallel",),
                                             vmem_limit_bytes=VMEM_LIMIT),
        name="attention",
    )(sink, pa, pa, pa, pa, pa, pa, pa, norm_w)


def _split3(x):
    hi = x.astype(BF16)
    r1 = x - hi.astype(F32)
    mid = r1.astype(BF16)
    lo = (r1 - mid.astype(F32)).astype(BF16)
    return hi, mid, lo


def _gla_body(reverse, final, *refs):
    if final:
        (q_ref, k_ref, v_ref, lr_ref, up_ref, bias_ref, tri_ref, of_ref, gr_ref, nw_ref,
         o_ref, st_ref) = refs
    else:
        q_ref, k_ref, v_ref, lr_ref, up_ref, bias_ref, tri_ref, o_ref, st_ref = refs
    C = GLA_CHUNK

    @pl.when(pl.program_id(0) == 0)
    def _():
        st_ref[...] = jnp.zeros_like(st_ref)

    g = jnp.dot(lr_ref[...], up_ref[...], preferred_element_type=F32,
                precision=lax.Precision.HIGHEST) + bias_ref[...]
    la = jax.nn.log_sigmoid(g) * (1.0 / GLA_TAU)
    tri = tri_ref[...]
    hi, mid, lo = _split3(la)
    b = _dot(tri, hi) + _dot(tri, mid) + _dot(tri, lo)
    end = 0 if reverse else C - 1
    b_end = b[end:end + 1]
    b_mid = b[C // 2:C // 2 + 1]
    scale = GLA_DK ** -0.5
    q = q_ref[...].astype(F32) * scale
    k = k_ref[...].astype(F32)
    q_in = (q * jnp.exp(b - b_mid)).astype(BF16)
    k_in = (k * jnp.exp(b_mid - b)).astype(BF16)
    q_dec = (q * jnp.exp(b)).astype(BF16)
    k_end = (k * jnp.exp(b_end - b)).astype(BF16)
    decay = jnp.exp(b_end)
    ri = lax.broadcasted_iota(jnp.int32, (C, C), 0)
    ci = lax.broadcasted_iota(jnp.int32, (C, C), 1)
    causal = (ri <= ci) if reverse else (ri >= ci)
    for h in range(GLA_HEADS):
        ks = slice(h * GLA_DK, (h + 1) * GLA_DK)
        vs = slice(h * GLA_DV, (h + 1) * GLA_DV)
        vh = v_ref[:, vs]
        att = jnp.where(causal, _dot_nt(q_in[:, ks], k_in[:, ks]), 0.0).astype(BF16)
        st = st_ref[h]
        o = _dot(att, vh) + _dot_nt(q_dec[:, ks], st.astype(BF16))
        st_ref[h] = st * decay[:, ks] + _dot_tn(vh, k_end[:, ks])
        if final:
            tot = o + of_ref[:, vs]
            y = tot * lax.rsqrt(jnp.mean(tot * tot, axis=-1, keepdims=True) + EPS) * nw_ref[...]
            o_ref[:, vs] = (y * jax.nn.silu(gr_ref[:, vs].astype(F32))).astype(BF16)
        else:
            o_ref[:, vs] = o


def _gla_pass(reverse, pqk, pv, plr, up, bias, tri, extra=None):
    T = pqk.shape[0]
    C = GLA_CHUNK
    n = T // C
    final = extra is not None
    blk = (lambda i: n - 1 - i) if reverse else (lambda i: i)
    row = lambda w, col=0: pl.BlockSpec((C, w), lambda i: (blk(i), col))
    in_specs = [row(GLA_QK_WIDTH, 0), row(GLA_QK_WIDTH, 1), row(GLA_WIDTH), row(2 * GLA_RANK),
                _resident(up.shape), _resident(bias.shape), _resident(tri.shape)]
    args = [pqk, pqk, pv, plr, up, bias, tri]
    if final:
        o_f, pr, norm_w = extra
        in_specs += [row(GLA_WIDTH), row(GLA_WIDTH), _resident(norm_w.shape)]
        args += [o_f, pr, norm_w]
    return pl.pallas_call(
        functools.partial(_gla_body, reverse, final),
        grid=(n,),
        in_specs=in_specs,
        out_specs=row(GLA_WIDTH),
        out_shape=jax.ShapeDtypeStruct((T, GLA_WIDTH), BF16 if final else F32),
        scratch_shapes=[pltpu.VMEM((GLA_HEADS, GLA_DV, GLA_DK), F32)],
        compiler_params=pltpu.CompilerParams(dimension_semantics=("arbitrary",),
                                             vmem_limit_bytes=VMEM_LIMIT),
        name="gla_bwd" if reverse else "gla_fwd",
    )(*args)


def _outproj_body(a_ref, g_ref, x_ref, wa_ref, wg_ref, ln_ref, wr_hl_ref, wr_hi_ref, rb_ref, tri_ref,
                  h_ref, xn_ref, ri_ref, rw_ref, cnt_ref):
    tm = x_ref.shape[0]

    @pl.when(pl.program_id(0) == 0)
    def _():
        cnt_ref[...] = jnp.zeros_like(cnt_ref)

    h = x_ref[...] + _dot(a_ref[...], wa_ref[...]) + _dot(g_ref[...], wg_ref[...])
    h_ref[...] = h
    xn = h * lax.rsqrt(jnp.mean(h * h, axis=-1, keepdims=True) + EPS) * ln_ref[...]
    xn_ref[...] = xn

    x_hi = xn.astype(BF16)
    x_lo = (xn - x_hi.astype(F32)).astype(BF16)
    l2 = _dot(x_hi, wr_hl_ref[...])
    logits = l2[:, :ROUTER_ROWS] + l2[:, ROUTER_ROWS:] + _dot(x_lo, wr_hi_ref[...])
    lt = jnp.transpose(logits) + rb_ref[...]

    gl = lt[0:N_GROUPS]
    gmax = jnp.max(gl, axis=0, keepdims=True)
    gi = lax.broadcasted_iota(jnp.int32, gl.shape, 0).astype(F32)
    g_sel = jnp.min(jnp.where(gl == gmax, gi, float(N_GROUPS)), axis=0, keepdims=True)
    g_gate = 1.0 / jnp.sum(jnp.exp(gl - gmax), axis=0, keepdims=True)

    el = lt[EXPERT_ROW0:EXPERT_ROW0 + N_EXPERTS]
    ei_int = lax.broadcasted_iota(jnp.int32, el.shape, 0)
    ei = ei_int.astype(F32)
    grp = (ei_int >> 3).astype(F32)
    cand = jnp.where(grp == g_sel, el, -jnp.inf)
    v1 = jnp.max(cand, axis=0, keepdims=True)
    e1 = jnp.min(jnp.where(cand == v1, ei, float(N_EXPERTS)), axis=0, keepdims=True)
    cand2 = jnp.where(ei == e1, -jnp.inf, cand)
    v2 = jnp.max(cand2, axis=0, keepdims=True)
    e2 = jnp.min(jnp.where(cand2 == v2, ei, float(N_EXPERTS)), axis=0, keepdims=True)
    d = jnp.exp(v2 - v1)
    w1 = g_gate / (1.0 + d)
    w2 = g_gate * d / (1.0 + d)

    oh1 = (ei == e1).astype(F32)
    oh2 = (ei == e2).astype(F32)
    cnt = oh1 + oh2
    before = _dot(cnt.astype(BF16), tri_ref[...]) + cnt_ref[:, 0:1]
    r1 = jnp.sum(oh1 * before, axis=0, keepdims=True)
    r2 = jnp.sum(oh2 * before, axis=0, keepdims=True)
    cnt_ref[...] = cnt_ref[...] + jnp.sum(cnt, axis=1, keepdims=True)

    ri_ref[...] = jnp.concatenate([e1, e2, r1, r2, jnp.zeros((4, tm), F32)], axis=0).astype(jnp.int32)
    rw_ref[...] = jnp.concatenate([w1, w2, jnp.zeros((6, tm), F32)], axis=0)


def _outproj_router(attn, gla, x2, w_attn, w_gla, ln_w, wr_hl, wr_hi, rbias, tri):
    T = x2.shape[0]
    tm = OUT_ROWS
    row = lambda w: pl.BlockSpec((tm, w), lambda i: (i, 0))
    col = lambda r: pl.BlockSpec((r, tm), lambda i: (0, i))
    return pl.pallas_call(
        _outproj_body,
        grid=(T // tm,),
        in_specs=[row(ATTN_WIDTH), row(GLA_WIDTH), row(D_MODEL),
                  _resident(w_attn.shape), _resident(w_gla.shape), _resident(ln_w.shape),
                  _resident(wr_hl.shape), _resident(wr_hi.shape), _resident(rbias.shape),
                  _resident(tri.shape)],
        out_specs=[row(D_MODEL), row(D_MODEL), col(8), col(8),
                   pl.BlockSpec((N_EXPERTS, 128), lambda i: (0, 0))],
        out_shape=[jax.ShapeDtypeStruct((T, D_MODEL), F32),
                   jax.ShapeDtypeStruct((T, D_MODEL), F32),
                   jax.ShapeDtypeStruct((8, T), jnp.int32),
                   jax.ShapeDtypeStruct((8, T), F32),
                   jax.ShapeDtypeStruct((N_EXPERTS, 128), F32)],
        compiler_params=pltpu.CompilerParams(dimension_semantics=("arbitrary",),
                                             vmem_limit_bytes=VMEM_LIMIT),
        name="outproj_router",
    )(attn, gla, x2, w_attn, w_gla, ln_w, wr_hl, wr_hi, rbias, tri)


def _row_copy(src, s, dst, d, sem):
    return pltpu.make_async_copy(src.at[pl.ds(s, 1)], dst.at[pl.ds(d, 1)], sem)


def _dispatch_body(slot_ref, xn_ref, xs_in_ref, xs_ref, sem):
    del xs_in_ref
    base = pl.program_id(0) * DISPATCH_ROWS

    def issue(t, carry):
        tok = base + t
        for k in range(TOP_K):
            _row_copy(xn_ref, tok, xs_ref, slot_ref[TOP_K * tok + k], sem).start()
        return carry

    def drain(t, carry):
        _row_copy(xn_ref, 0, xs_ref, 0, sem).wait()
        return carry

    lax.fori_loop(0, DISPATCH_ROWS, issue, 0)
    lax.fori_loop(0, TOP_K * DISPATCH_ROWS, drain, 0)


def _dispatch(slots, xn, n_slots):
    T = xn.shape[0]
    xs0 = jnp.zeros((n_slots, D_MODEL), F32)
    any_spec = pl.BlockSpec(memory_space=pl.ANY)
    grid_spec = pltpu.PrefetchScalarGridSpec(
        num_scalar_prefetch=1,
        grid=(T // DISPATCH_ROWS,),
        in_specs=[any_spec, any_spec],
        out_specs=any_spec,
        scratch_shapes=[pltpu.SemaphoreType.DMA(())],
    )
    return pl.pallas_call(
        _dispatch_body,
        grid_spec=grid_spec,
        out_shape=jax.ShapeDtypeStruct((n_slots, D_MODEL), F32),
        input_output_aliases={2: 0},
        compiler_params=pltpu.CompilerParams(dimension_semantics=("arbitrary",)),
        name="dispatch",
    )(slots, xn, xs0)


def _moe_body(be_ref, nu_ref, x_ref, wg_ref, wu_ref, wd_ref, y_ref):
    @pl.when(pl.program_id(0) < nu_ref[0])
    def _():
        x = x_ref[...].astype(BF16)
        hid = jax.nn.silu(_dot(x, wg_ref[0])) * _dot(x, wu_ref[0])
        y_ref[...] = _dot(hid.astype(BF16), wd_ref[0])

    @pl.when(pl.program_id(0) >= nu_ref[0])
    def _():
        y_ref[...] = jnp.zeros_like(y_ref)


def _moe(block_e, n_used, xs, wg, wu, wd):
    n_slots = xs.shape[0]
    bm = MOE_ROWS
    grid_spec = pltpu.PrefetchScalarGridSpec(
        num_scalar_prefetch=2,
        grid=(n_slots // bm,),
        in_specs=[pl.BlockSpec((bm, D_MODEL), lambda b, be, nu: (b, 0)),
                  pl.BlockSpec((1, D_MODEL, D_FF), lambda b, be, nu: (be[b], 0, 0)),
                  pl.BlockSpec((1, D_MODEL, D_FF), lambda b, be, nu: (be[b], 0, 0)),
                  pl.BlockSpec((1, D_FF, D_MODEL), lambda b, be, nu: (be[b], 0, 0))],
        out_specs=pl.BlockSpec((bm, D_MODEL), lambda b, be, nu: (b, 0)),
    )
    return pl.pallas_call(
        _moe_body,
        grid_spec=grid_spec,
        out_shape=jax.ShapeDtypeStruct((n_slots, D_MODEL), F32),
        compiler_params=pltpu.CompilerParams(dimension_semantics=("arbitrary",),
                                             vmem_limit_bytes=VMEM_LIMIT),
        name="moe",
    )(block_e, n_used, xs, wg, wu, wd)


def _combine_body(slot_ref, h_ref, w_ref, ys_ref, o_ref, g_ref, sem):
    base = pl.program_id(0) * COMBINE_ROWS

    def issue(t, carry):
        tok = base + t
        for k in range(TOP_K):
            _row_copy(ys_ref, slot_ref[TOP_K * tok + k], g_ref.at[k], t, sem).start()
        return carry

    def drain(t, carry):
        _row_copy(ys_ref, 0, g_ref.at[0], 0, sem).wait()
        return carry

    lax.fori_loop(0, COMBINE_ROWS, issue, 0)
    lax.fori_loop(0, TOP_K * COMBINE_ROWS, drain, 0)
    o_ref[...] = h_ref[...] + w_ref[:, 0:1] * g_ref[0] + w_ref[:, 1:2] * g_ref[1]


def _combine(slots, h, wcol, ys):
    T = h.shape[0]
    tb = COMBINE_ROWS
    grid_spec = pltpu.PrefetchScalarGridSpec(
        num_scalar_prefetch=1,
        grid=(T // tb,),
        in_specs=[pl.BlockSpec((tb, D_MODEL), lambda i, s: (i, 0)),
                  pl.BlockSpec((tb, TOP_K), lambda i, s: (i, 0)),
                  pl.BlockSpec(memory_space=pl.ANY)],
        out_specs=pl.BlockSpec((tb, D_MODEL), lambda i, s: (i, 0)),
        scratch_shapes=[pltpu.VMEM((TOP_K, tb, D_MODEL), F32), pltpu.SemaphoreType.DMA(())],
    )
    return pl.pallas_call(
        _combine_body,
        grid_spec=grid_spec,
        out_shape=jax.ShapeDtypeStruct((T, D_MODEL), F32),
        compiler_params=pltpu.CompilerParams(dimension_semantics=("arbitrary",),
                                             vmem_limit_bytes=VMEM_LIMIT),
        name="combine",
    )(slots, h, wcol, ys)


def _rope_tables(T):
    inv_freq = jnp.power(jnp.float32(ROPE_THETA),
                         -jnp.arange(ROPE_HALF, dtype=F32) * (2.0 / ROPE_DIM))
    ang = jnp.arange(T).astype(F32)[:, None] * inv_freq[None, :]
    cos, sin = jnp.cos(ang), jnp.sin(ang)
    zeros = jnp.zeros((T, HEAD_DIM - ROPE_DIM), F32)
    zh = jnp.zeros((T, ROPE_HALF), F32)
    cos_t = jnp.concatenate([cos, cos, jnp.ones((T, HEAD_DIM - ROPE_DIM), F32)], axis=1)
    sa_t = jnp.concatenate([zh, sin, zeros], axis=1)
    sb_t = jnp.concatenate([-sin, zh, zeros], axis=1)
    return cos_t, sa_t, sb_t


def _split_hi_lo(w):
    hi = w.astype(BF16)
    lo = (w - hi.astype(F32)).astype(BF16)
    return hi, lo


def _layer(x2, ln1_w, w_in, q_norm_w, k_norm_w, attn_sink, attn_out_norm_w, gate_up_f, gate_bias_f,
           gate_up_b, gate_bias_b, gla_out_norm_w, w_out, ln2_w, w_group, b_group, w_router, b_router,
           w_gate_e, w_up_e, w_down_e):
    T = x2.shape[0]
    row = lambda v: v.reshape(1, -1).astype(F32)

    w_in_b = w_in.astype(BF16)
    c0, c1, c2, c3 = A_WIDTH, A_WIDTH + 2 * GLA_QK_WIDTH, A_WIDTH + 2 * GLA_QK_WIDTH + GLA_WIDTH, \
        A_WIDTH + 2 * GLA_QK_WIDTH + 2 * GLA_WIDTH
    wa, wqk, wv, wr, wl = (w_in_b[:, :c0], w_in_b[:, c0:c1], w_in_b[:, c1:c2], w_in_b[:, c2:c3],
                           w_in_b[:, c3:])
    cos_t, sa_t, sb_t = _rope_tables(T)
    pa, pqk, pv, pr, plr = _inproj(x2, row(ln1_w), wa, wqk, wv, wr, wl, row(q_norm_w), row(k_norm_w),
                                   cos_t, sa_t, sb_t)

    attn = _attention(pa, attn_sink.astype(F32), row(attn_out_norm_w))

    C = GLA_CHUNK
    ones = jnp.ones((C, C), F32)
    zr = jnp.zeros((GLA_RANK, GLA_QK_WIDTH), F32)
    up_f = jnp.concatenate([gate_up_f.astype(F32), zr], axis=0)
    up_b = jnp.concatenate([zr, gate_up_b.astype(F32)], axis=0)
    o_f = _gla_pass(False, pqk, pv, plr, up_f, row(gate_bias_f), jnp.tril(ones).astype(BF16))
    gla = _gla_pass(True, pqk, pv, plr, up_b, row(gate_bias_b), jnp.triu(ones).astype(BF16),
                    extra=(o_f, pr, row(gla_out_norm_w)))

    w_out_b = w_out.astype(BF16)
    wr_full = jnp.zeros((D_MODEL, ROUTER_ROWS), F32)
    wr_full = wr_full.at[:, :N_GROUPS].set(w_group.astype(F32))
    wr_full = wr_full.at[:, EXPERT_ROW0:EXPERT_ROW0 + N_EXPERTS].set(w_router.astype(F32))
    wr_hi, wr_lo = _split_hi_lo(wr_full)
    rbias = jnp.zeros((ROUTER_ROWS, 1), F32)
    rbias = rbias.at[:N_GROUPS, 0].set(b_group.astype(F32))
    rbias = rbias.at[EXPERT_ROW0:EXPERT_ROW0 + N_EXPERTS, 0].set(b_router.astype(F32))
    tm = OUT_ROWS
    earlier = jnp.triu(jnp.ones((tm, tm), F32), k=1).astype(BF16)
    h, xn, r_int, r_w, counts = _outproj_router(
        attn, gla, x2, w_out_b[:ATTN_WIDTH], w_out_b[ATTN_WIDTH:], row(ln2_w),
        jnp.concatenate([wr_hi, wr_lo], axis=1), wr_hi, rbias, earlier)

    bm = MOE_ROWS
    n_blocks = (T * TOP_K) // bm + N_EXPERTS
    cnt = counts[:, 0].astype(jnp.int32)
    padded = (cnt + bm - 1) // bm * bm
    pad_end = jnp.cumsum(padded)
    pad_start = pad_end - padded
    n_used = (pad_end[-1] // bm).astype(jnp.int32).reshape(1)
    block_e = jnp.minimum(jnp.searchsorted(pad_end, jnp.arange(n_blocks) * bm, side='right'),
                          N_EXPERTS - 1).astype(jnp.int32)
    slots = (pad_start[r_int[0:TOP_K]] + r_int[TOP_K:2 * TOP_K]).T.reshape(-1).astype(jnp.int32)

    xs = _dispatch(slots, xn, n_blocks * bm)
    ys = _moe(block_e, n_used, xs, w_gate_e.astype(BF16), w_up_e.astype(BF16), w_down_e.astype(BF16))
    return _combine(slots, h, r_w[0:TOP_K].T, ys)


def kernel(x, ln1_w, w_in, q_norm_w, k_norm_w, attn_sink, attn_out_norm_w, gla_gate_up_f, gla_gate_bias_f,
           gla_gate_up_b, gla_gate_bias_b, gla_out_norm_w, w_out, ln2_w, w_group, b_group, w_router,
           b_router, w_gate_e, w_up_e, w_down_e):
    B, S, D = x.shape
    h = x.reshape(B * S, D)
    assert B == 1
    for l in range(ln1_w.shape[0]):
        h = _layer(h, ln1_w[l], w_in[l], q_norm_w[l], k_norm_w[l], attn_sink[l], attn_out_norm_w[l],
                   gla_gate_up_f[l], gla_gate_bias_f[l], gla_gate_up_b[l], gla_gate_bias_b[l],
                   gla_out_norm_w[l], w_out[l], ln2_w[l], w_group[l], b_group[l], w_router[l],
                   b_router[l], w_gate_e[l], w_up_e[l], w_down_e[l])
    return h.reshape(B, S, D)
```

```python
import functools

import jax
import jax.numpy as jnp
from jax import lax
from jax.experimental import pallas as pl
from jax.experimental.pallas import tpu as pltpu

F32 = jnp.float32
BF16 = jnp.bfloat16

EPS = 1e-6
D_MODEL = 2048

ATTN_Q_HEADS = 8
ATTN_KV_HEADS = 2
ATTN_GROUP = ATTN_Q_HEADS // ATTN_KV_HEADS
HEAD_DIM = 128
WINDOW = 128
ATTN_BLOCK = 128
ROPE_THETA = 500000.0
ROPE_DIM = HEAD_DIM // 4
ROPE_HALF = ROPE_DIM // 2
ATTN_WIDTH = ATTN_Q_HEADS * HEAD_DIM
KV_WIDTH = ATTN_KV_HEADS * HEAD_DIM
A_WIDTH = ATTN_WIDTH + 2 * KV_WIDTH

GLA_HEADS = 4
GLA_DK = 128
GLA_DV = 256
GLA_RANK = 16
GLA_TAU = 16.0
GLA_QK_WIDTH = GLA_HEADS * GLA_DK
GLA_WIDTH = GLA_HEADS * GLA_DV
GLA_CHUNK = 128

N_GROUPS = 4
EXPERTS_PER_GROUP = 8
N_EXPERTS = N_GROUPS * EXPERTS_PER_GROUP
TOP_K = 2
D_FF = 1024
ROUTER_ROWS = 128
EXPERT_ROW0 = 8

PROJ_ROWS = 512
OUT_ROWS = 256
MOE_ROWS = 256
DISPATCH_ROWS = 512
COMBINE_ROWS = 256

VMEM_LIMIT = 56 * 1024 * 1024


def _dot(a, b):
    return jnp.dot(a, b, preferred_element_type=F32)


def _dot_nt(a, b):
    return lax.dot_general(a, b, (((1,), (1,)), ((), ())), preferred_element_type=F32)


def _dot_tn(a, b):
    return lax.dot_general(a, b, (((0,), (0,)), ((), ())), preferred_element_type=F32)


def _resident(shape):
    nd = len(shape)
    return pl.BlockSpec(shape, lambda *_: (0,) * nd, pipeline_mode=pl.Buffered(1))


def _inproj_body(x_ref, ln_ref, wa_ref, wqk_ref, wv_ref, wr_ref, wl_ref, qn_ref, kn_ref,
                 cos_ref, sa_ref, sb_ref, oa_ref, oqk_ref, ov_ref, or_ref, ol_ref):
    x = x_ref[...]
    ms = jnp.mean(x * x, axis=-1, keepdims=True)
    xn = (x * lax.rsqrt(ms + EPS) * ln_ref[...]).astype(BF16)
    oqk_ref[...] = _dot(xn, wqk_ref[...]).astype(BF16)
    ov_ref[...] = _dot(xn, wv_ref[...]).astype(BF16)
    or_ref[...] = _dot(xn, wr_ref[...]).astype(BF16)
    ol_ref[...] = _dot(xn, wl_ref[...])
    acc = _dot(xn, wa_ref[...])
    cos, sa, sb = cos_ref[...], sa_ref[...], sb_ref[...]
    for c in range(ATTN_Q_HEADS + ATTN_KV_HEADS):
        xh = acc[:, c * HEAD_DIM:(c + 1) * HEAD_DIM]
        w = qn_ref[...] if c < ATTN_Q_HEADS else kn_ref[...]
        y = xh * lax.rsqrt(jnp.mean(xh * xh, axis=-1, keepdims=True) + EPS) * w
        y = (y * cos + pltpu.roll(y, ROPE_HALF, 1) * sa
             + pltpu.roll(y, HEAD_DIM - ROPE_HALF, 1) * sb)
        if c < ATTN_Q_HEADS:
            y = y * (HEAD_DIM ** -0.5)
        oa_ref[:, c * HEAD_DIM:(c + 1) * HEAD_DIM] = y.astype(BF16)
    oa_ref[:, ATTN_WIDTH + KV_WIDTH:] = acc[:, ATTN_WIDTH + KV_WIDTH:].astype(BF16)


def _inproj(x2, ln_w, wa, wqk, wv, wr, wl, qn, kn, cos_t, sa_t, sb_t):
    T = x2.shape[0]
    tm = PROJ_ROWS
    row = lambda w: pl.BlockSpec((tm, w), lambda i: (i, 0))
    return pl.pallas_call(
        _inproj_body,
        grid=(T // tm,),
        in_specs=[row(D_MODEL), _resident((1, D_MODEL)),
                  _resident(wa.shape), _resident(wqk.shape), _resident(wv.shape),
                  _resident(wr.shape), _resident(wl.shape),
                  _resident((1, HEAD_DIM)), _resident((1, HEAD_DIM)),
                  row(HEAD_DIM), row(HEAD_DIM), row(HEAD_DIM)],
        out_specs=[row(A_WIDTH), row(2 * GLA_QK_WIDTH), row(GLA_WIDTH), row(GLA_WIDTH),
                   row(2 * GLA_RANK)],
        out_shape=[jax.ShapeDtypeStruct((T, A_WIDTH), BF16),
                   jax.ShapeDtypeStruct((T, 2 * GLA_QK_WIDTH), BF16),
                   jax.ShapeDtypeStruct((T, GLA_WIDTH), BF16),
                   jax.ShapeDtypeStruct((T, GLA_WIDTH), BF16),
                   jax.ShapeDtypeStruct((T, 2 * GLA_RANK), F32)],
        compiler_params=pltpu.CompilerParams(dimension_semantics=("parallel",),
                                             vmem_limit_bytes=VMEM_LIMIT),
        name="inproj",
    )(x2, ln_w, wa, wqk, wv, wr, wl, qn, kn, cos_t, sa_t, sb_t)


def _attn_body(sink_ref, q_ref, kp_ref, kc_ref, kn_ref, vp_ref, vc_ref, vn_ref, nw_ref, o_ref):
    n = pl.program_id(0)
    nb = pl.num_programs(0)
    rows = ATTN_GROUP * ATTN_BLOCK
    keys = 3 * ATTN_BLOCK
    r = lax.broadcasted_iota(jnp.int32, (rows, keys), 0) & (ATTN_BLOCK - 1)
    c = lax.broadcasted_iota(jnp.int32, (rows, keys), 1)
    valid = (c >= r + (ATTN_BLOCK - WINDOW)) & (c <= r + (ATTN_BLOCK + WINDOW))
    valid = valid & ((c >= ATTN_BLOCK) | (n > 0)) & ((c < 2 * ATTN_BLOCK) | (n < nb - 1))
    outs = []
    for g in range(ATTN_KV_HEADS):
        ks = slice(g * HEAD_DIM, (g + 1) * HEAD_DIM)
        k3 = jnp.concatenate([kp_ref[:, ks], kc_ref[:, ks], kn_ref[:, ks]], axis=0)
        v3 = jnp.concatenate([vp_ref[:, ks], vc_ref[:, ks], vn_ref[:, ks]], axis=0)
        heads = range(g * ATTN_GROUP, (g + 1) * ATTN_GROUP)
        q4 = jnp.concatenate([q_ref[:, h * HEAD_DIM:(h + 1) * HEAD_DIM] for h in heads], axis=0)
        s = jnp.where(valid, _dot_nt(q4, k3), -jnp.inf)
        sink = jnp.concatenate([jnp.full((ATTN_BLOCK, 1), sink_ref[h], F32) for h in heads], axis=0)
        m = jnp.maximum(jnp.max(s, axis=-1, keepdims=True), sink)
        e = jnp.exp(s - m)
        denom = jnp.sum(e, axis=-1, keepdims=True) + jnp.exp(sink - m)
        o = _dot(e.astype(BF16), v3) / denom
        outs.extend(o[i * ATTN_BLOCK:(i + 1) * ATTN_BLOCK] for i in range(ATTN_GROUP))
    a = jnp.concatenate(outs, axis=1)
    a = a * lax.rsqrt(jnp.mean(a * a, axis=-1, keepdims=True) + EPS) * nw_ref[...]
    o_ref[...] = a.astype(BF16)


def _attention(pa, sink, norm_w):
    T = pa.shape[0]
    nb = T // ATTN_BLOCK
    kcol = ATTN_WIDTH // KV_WIDTH
    vcol = kcol + 1
    prev = lambda n, _: jnp.maximum(n - 1, 0)
    cur = lambda n, _: n
    nxt = lambda n, _: jnp.minimum(n + 1, nb - 1)
    kv = lambda f, col: pl.BlockSpec((ATTN_BLOCK, KV_WIDTH), lambda n, s: (f(n, s), col))
    grid_spec = pltpu.PrefetchScalarGridSpec(
        num_scalar_prefetch=1,
        grid=(nb,),
        in_specs=[pl.BlockSpec((ATTN_BLOCK, ATTN_WIDTH), lambda n, s: (n, 0)),
                  kv(prev, kcol), kv(cur, kcol), kv(nxt, kcol),
                  kv(prev, vcol), kv(cur, vcol), kv(nxt, vcol),
                  pl.BlockSpec((1, ATTN_WIDTH), lambda n, s: (0, 0))],
        out_specs=pl.BlockSpec((ATTN_BLOCK, ATTN_WIDTH), lambda n, s: (n, 0)),
    )
    return pl.pallas_call(
        _attn_body,
        grid_spec=grid_spec,
        out_shape=jax.ShapeDtypeStruct((T, ATTN_WIDTH), BF16),
        compiler_params=pltpu.CompilerParams(dimension_semantics=("parallel",),
                                             vmem_limit_bytes=VMEM_LIMIT),
        name="attention",
    )(sink, pa, pa, pa, pa, pa, pa, pa, norm_w)


def _split3(x):
    hi = x.astype(BF16)
    r1 = x - hi.astype(F32)
    mid = r1.astype(BF16)
    lo = (r1 - mid.astype(F32)).astype(BF16)
    return hi, mid, lo


def _gla_body(reverse, final, *refs):
    if final:
        (q_ref, k_ref, v_ref, lr_ref, up_ref, bias_ref, tri_ref, of_ref, gr_ref, nw_ref,
         o_ref, st_ref) = refs
    else:
        q_ref, k_ref, v_ref, lr_ref, up_ref, bias_ref, tri_ref, o_ref, st_ref = refs
    C = GLA_CHUNK

    @pl.when(pl.program_id(0) == 0)
    def _():
        st_ref[...] = jnp.zeros_like(st_ref)

    g = jnp.dot(lr_ref[...], up_ref[...], preferred_element_type=F32,
                precision=lax.Precision.HIGHEST) + bias_ref[...]
    la = jax.nn.log_sigmoid(g) * (1.0 / GLA_TAU)
    tri = tri_ref[...]
    hi, mid, lo = _split3(la)
    b = _dot(tri, hi) + _dot(tri, mid) + _dot(tri, lo)
    end = 0 if reverse else C - 1
    b_end = b[end:end + 1]
    b_mid = b[C // 2:C // 2 + 1]
    scale = GLA_DK ** -0.5
    q = q_ref[...].astype(F32) * scale
    k = k_ref[...].astype(F32)
    q_in = (q * jnp.exp(b - b_mid)).astype(BF16)
    k_in = (k * jnp.exp(b_mid - b)).astype(BF16)
    q_dec = (q * jnp.exp(b)).astype(BF16)
    k_end = (k * jnp.exp(b_end - b)).astype(BF16)
    decay = jnp.exp(b_end)
    ri = lax.broadcasted_iota(jnp.int32, (C, C), 0)
    ci = lax.broadcasted_iota(jnp.int32, (C, C), 1)
    causal = (ri <= ci) if reverse else (ri >= ci)
    for h in range(GLA_HEADS):
        ks = slice(h * GLA_DK, (h + 1) * GLA_DK)
        vs = slice(h * GLA_DV, (h + 1) * GLA_DV)
        vh = v_ref[:, vs]
        att = jnp.where(causal, _dot_nt(q_in[:, ks], k_in[:, ks]), 0.0).astype(BF16)
        st = st_ref[h]
        o = _dot(att, vh) + _dot_nt(q_dec[:, ks], st.astype(BF16))
        st_ref[h] = st * decay[:, ks] + _dot_tn(vh, k_end[:, ks])
        if final:
            tot = o + of_ref[:, vs]
            y = tot * lax.rsqrt(jnp.mean(tot * tot, axis=-1, keepdims=True) + EPS) * nw_ref[...]
            o_ref[:, vs] = (y * jax.nn.silu(gr_ref[:, vs].astype(F32))).astype(BF16)
        else:
            o_ref[:, vs] = o


def _gla_pass(reverse, pqk, pv, plr, up, bias, tri, extra=None):
    T = pqk.shape[0]
    C = GLA_CHUNK
    n = T // C
    final = extra is not None
    blk = (lambda i: n - 1 - i) if reverse else (lambda i: i)
    row = lambda w, col=0: pl.BlockSpec((C, w), lambda i: (blk(i), col))
    in_specs = [row(GLA_QK_WIDTH, 0), row(GLA_QK_WIDTH, 1), row(GLA_WIDTH), row(2 * GLA_RANK),
                _resident(up.shape), _resident(bias.shape), _resident(tri.shape)]
    args = [pqk, pqk, pv, plr, up, bias, tri]
    if final:
        o_f, pr, norm_w = extra
        in_specs += [row(GLA_WIDTH), row(GLA_WIDTH), _resident(norm_w.shape)]
        args += [o_f, pr, norm_w]
    return pl.pallas_call(
        functools.partial(_gla_body, reverse, final),
        grid=(n,),
        in_specs=in_specs,
        out_specs=row(GLA_WIDTH),
        out_shape=jax.ShapeDtypeStruct((T, GLA_WIDTH), BF16 if final else F32),
        scratch_shapes=[pltpu.VMEM((GLA_HEADS, GLA_DV, GLA_DK), F32)],
        compiler_params=pltpu.CompilerParams(dimension_semantics=("arbitrary",),
                                             vmem_limit_bytes=VMEM_LIMIT),
        name="gla_bwd" if reverse else "gla_fwd",
    )(*args)


def _outproj_body(a_ref, g_ref, x_ref, wa_ref, wg_ref, ln_ref, wr_hl_ref, wr_hi_ref, rb_ref, tri_ref,
                  h_ref, xn_ref, ri_ref, rw_ref, cnt_ref):
    tm = x_ref.shape[0]

    @pl.when(pl.program_id(0) == 0)
    def _():
        cnt_ref[...] = jnp.zeros_like(cnt_ref)

    h = x_ref[...] + _dot(a_ref[...], wa_ref[...]) + _dot(g_ref[...], wg_ref[...])
    h_ref[...] = h
    xn = h * lax.rsqrt(jnp.mean(h * h, axis=-1, keepdims=True) + EPS) * ln_ref[...]
    xn_ref[...] = xn

    x_hi = xn.astype(BF16)
    x_lo = (xn - x_hi.astype(F32)).astype(BF16)
    l2 = _dot(x_hi, wr_hl_ref[...])
    logits = l2[:, :ROUTER_ROWS] + l2[:, ROUTER_ROWS:] + _dot(x_lo, wr_hi_ref[...])
    lt = jnp.transpose(logits) + rb_ref[...]

    gl = lt[0:N_GROUPS]
    gmax = jnp.max(gl, axis=0, keepdims=True)
    gi = lax.broadcasted_iota(jnp.int32, gl.shape, 0).astype(F32)
    g_sel = jnp.min(jnp.where(gl == gmax, gi, float(N_GROUPS)), axis=0, keepdims=True)
    g_gate = 1.0 / jnp.sum(jnp.exp(gl - gmax), axis=0, keepdims=True)

    el = lt[EXPERT_ROW0:EXPERT_ROW0 + N_EXPERTS]
    ei_int = lax.broadcasted_iota(jnp.int32, el.shape, 0)
    ei = ei_int.astype(F32)
    grp = (ei_int >> 3).astype(F32)
    cand = jnp.where(grp == g_sel, el, -jnp.inf)
    v1 = jnp.max(cand, axis=0, keepdims=True)
    e1 = jnp.min(jnp.where(cand == v1, ei, float(N_EXPERTS)), axis=0, keepdims=True)
    cand2 = jnp.where(ei == e1, -jnp.inf, cand)
    v2 = jnp.max(cand2, axis=0, keepdims=True)
    e2 = jnp.min(jnp.where(cand2 == v2, ei, float(N_EXPERTS)), axis=0, keepdims=True)
    d = jnp.exp(v2 - v1)
    w1 = g_gate / (1.0 + d)
    w2 = g_gate * d / (1.0 + d)

    oh1 = (ei == e1).astype(F32)
    oh2 = (ei == e2).astype(F32)
    cnt = oh1 + oh2
    before = _dot(cnt.astype(BF16), tri_ref[...]) + cnt_ref[:, 0:1]
    r1 = jnp.sum(oh1 * before, axis=0, keepdims=True)
    r2 = jnp.sum(oh2 * before, axis=0, keepdims=True)
    cnt_ref[...] = cnt_ref[...] + jnp.sum(cnt, axis=1, keepdims=True)

    ri_ref[...] = jnp.concatenate([e1, e2, r1, r2, jnp.zeros((4, tm), F32)], axis=0).astype(jnp.int32)
    rw_ref[...] = jnp.concatenate([w1, w2, jnp.zeros((6, tm), F32)], axis=0)


def _outproj_router(attn, gla, x2, w_attn, w_gla, ln_w, wr_hl, wr_hi, rbias, tri):
    T = x2.shape[0]
    tm = OUT_ROWS
    row = lambda w: pl.BlockSpec((tm, w), lambda i: (i, 0))
    col = lambda r: pl.BlockSpec((r, tm), lambda i: (0, i))
    return pl.pallas_call(
        _outproj_body,
        grid=(T // tm,),
        in_specs=[row(ATTN_WIDTH), row(GLA_WIDTH), row(D_MODEL),
                  _resident(w_attn.shape), _resident(w_gla.shape), _resident(ln_w.shape),
                  _resident(wr_hl.shape), _resident(wr_hi.shape), _resident(rbias.shape),
                  _resident(tri.shape)],
        out_specs=[row(D_MODEL), row(D_MODEL), col(8), col(8),
                   pl.BlockSpec((N_EXPERTS, 128), lambda i: (0, 0))],
        out_shape=[jax.ShapeDtypeStruct((T, D_MODEL), F32),
                   jax.ShapeDtypeStruct((T, D_MODEL), F32),
                   jax.ShapeDtypeStruct((8, T), jnp.int32),
                   jax.ShapeDtypeStruct((8, T), F32),
                   jax.ShapeDtypeStruct((N_EXPERTS, 128), F32)],
        compiler_params=pltpu.CompilerParams(dimension_semantics=("arbitrary",),
                                             vmem_limit_bytes=VMEM_LIMIT),
        name="outproj_router",
    )(attn, gla, x2, w_attn, w_gla, ln_w, wr_hl, wr_hi, rbias, tri)


def _row_copy(src, s, dst, d, sem):
    return pltpu.make_async_copy(src.at[pl.ds(s, 1)], dst.at[pl.ds(d, 1)], sem)


def _dispatch_body(slot_ref, xn_ref, xs_in_ref, xs_ref, sem):
    del xs_in_ref
    base = pl.program_id(0) * DISPATCH_ROWS

    def issue(t, carry):
        for k in range(TOP_K):
            _row_copy(xn_ref, t, xs_ref, slot_ref[TOP_K * (base + t) + k], sem).start()
        return carry

    def drain(t, carry):
        _row_copy(xn_ref, 0, xs_ref, 0, sem).wait()
        return carry

    lax.fori_loop(0, DISPATCH_ROWS, issue, 0)
    lax.fori_loop(0, TOP_K * DISPATCH_ROWS, drain, 0)


def _dispatch(slots, xn, n_slots):
    T = xn.shape[0]
    xs0 = jnp.zeros((n_slots, D_MODEL), F32)
    any_spec = pl.BlockSpec(memory_space=pl.ANY)
    grid_spec = pltpu.PrefetchScalarGridSpec(
        num_scalar_prefetch=1,
        grid=(T // DISPATCH_ROWS,),
        in_specs=[pl.BlockSpec((DISPATCH_ROWS, D_MODEL), lambda i, s: (i, 0)), any_spec],
        out_specs=any_spec,
        scratch_shapes=[pltpu.SemaphoreType.DMA(())],
    )
    return pl.pallas_call(
        _dispatch_body,
        grid_spec=grid_spec,
        out_shape=jax.ShapeDtypeStruct((n_slots, D_MODEL), F32),
        input_output_aliases={2: 0},
        compiler_params=pltpu.CompilerParams(dimension_semantics=("arbitrary",)),
        name="dispatch",
    )(slots, xn, xs0)


def _moe_body(be_ref, nu_ref, x_ref, wg_ref, wu_ref, wd_ref, y_ref):
    @pl.when(pl.program_id(0) < nu_ref[0])
    def _():
        x = x_ref[...].astype(BF16)
        hid = jax.nn.silu(_dot(x, wg_ref[0])) * _dot(x, wu_ref[0])
        y_ref[...] = _dot(hid.astype(BF16), wd_ref[0])

    @pl.when(pl.program_id(0) >= nu_ref[0])
    def _():
        y_ref[...] = jnp.zeros_like(y_ref)


def _moe(block_e, n_used, xs, wg, wu, wd):
    n_slots = xs.shape[0]
    bm = MOE_ROWS
    grid_spec = pltpu.PrefetchScalarGridSpec(
        num_scalar_prefetch=2,
        grid=(n_slots // bm,),
        in_specs=[pl.BlockSpec((bm, D_MODEL), lambda b, be, nu: (b, 0)),
                  pl.BlockSpec((1, D_MODEL, D_FF), lambda b, be, nu: (be[b], 0, 0)),
                  pl.BlockSpec((1, D_MODEL, D_FF), lambda b, be, nu: (be[b], 0, 0)),
                  pl.BlockSpec((1, D_FF, D_MODEL), lambda b, be, nu: (be[b], 0, 0))],
        out_specs=pl.BlockSpec((bm, D_MODEL), lambda b, be, nu: (b, 0)),
    )
    return pl.pallas_call(
        _moe_body,
        grid_spec=grid_spec,
        out_shape=jax.ShapeDtypeStruct((n_slots, D_MODEL), F32),
        compiler_params=pltpu.CompilerParams(dimension_semantics=("arbitrary",),
                                             vmem_limit_bytes=VMEM_LIMIT),
        name="moe",
    )(block_e, n_used, xs, wg, wu, wd)


def _combine_body(slot_ref, h_ref, w_ref, ys_ref, o_ref, g_ref, sem):
    base = pl.program_id(0) * COMBINE_ROWS

    def issue(t, carry):
        tok = base + t
        for k in range(TOP_K):
            _row_copy(ys_ref, slot_ref[TOP_K * tok + k], g_ref.at[k], t, sem).start()
        return carry

    def drain(t, carry):
        _row_copy(ys_ref, 0, g_ref.at[0], 0, sem).wait()
        return carry

    lax.fori_loop(0, COMBINE_ROWS, issue, 0)
    lax.fori_loop(0, TOP_K * COMBINE_ROWS, drain, 0)
    o_ref[...] = h_ref[...] + w_ref[:, 0:1] * g_ref[0] + w_ref[:, 1:2] * g_ref[1]


def _combine(slots, h, wcol, ys):
    T = h.shape[0]
    tb = COMBINE_ROWS
    grid_spec = pltpu.PrefetchScalarGridSpec(
        num_scalar_prefetch=1,
        grid=(T // tb,),
        in_specs=[pl.BlockSpec((tb, D_MODEL), lambda i, s: (i, 0)),
                  pl.BlockSpec((tb, TOP_K), lambda i, s: (i, 0)),
                  pl.BlockSpec(memory_space=pl.ANY)],
        out_specs=pl.BlockSpec((tb, D_MODEL), lambda i, s: (i, 0)),
        scratch_shapes=[pltpu.VMEM((TOP_K, tb, D_MODEL), F32), pltpu.SemaphoreType.DMA(())],
    )
    return pl.pallas_call(
        _combine_body,
        grid_spec=grid_spec,
        out_shape=jax.ShapeDtypeStruct((T, D_MODEL), F32),
        compiler_params=pltpu.CompilerParams(dimension_semantics=("arbitrary",),
                                             vmem_limit_bytes=VMEM_LIMIT),
        name="combine",
    )(slots, h, wcol, ys)


def _rope_tables(T):
    inv_freq = jnp.power(jnp.float32(ROPE_THETA),
                         -jnp.arange(ROPE_HALF, dtype=F32) * (2.0 / ROPE_DIM))
    ang = jnp.arange(T).astype(F32)[:, None] * inv_freq[None, :]
    cos, sin = jnp.cos(ang), jnp.sin(ang)
    zeros = jnp.zeros((T, HEAD_DIM - ROPE_DIM), F32)
    zh = jnp.zeros((T, ROPE_HALF), F32)
    cos_t = jnp.concatenate([cos, cos, jnp.ones((T, HEAD_DIM - ROPE_DIM), F32)], axis=1)
    sa_t = jnp.concatenate([zh, sin, zeros], axis=1)
    sb_t = jnp.concatenate([-sin, zh, zeros], axis=1)
    return cos_t, sa_t, sb_t


def _split_hi_lo(w):
    hi = w.astype(BF16)
    lo = (w - hi.astype(F32)).astype(BF16)
    return hi, lo


def _layer(x2, ln1_w, w_in, q_norm_w, k_norm_w, attn_sink, attn_out_norm_w, gate_up_f, gate_bias_f,
           gate_up_b, gate_bias_b, gla_out_norm_w, w_out, ln2_w, w_group, b_group, w_router, b_router,
           w_gate_e, w_up_e, w_down_e):
    T = x2.shape[0]
    row = lambda v: v.reshape(1, -1).astype(F32)

    w_in_b = w_in.astype(BF16)
    c0, c1, c2, c3 = A_WIDTH, A_WIDTH + 2 * GLA_QK_WIDTH, A_WIDTH + 2 * GLA_QK_WIDTH + GLA_WIDTH, \
        A_WIDTH + 2 * GLA_QK_WIDTH + 2 * GLA_WIDTH
    wa, wqk, wv, wr, wl = (w_in_b[:, :c0], w_in_b[:, c0:c1], w_in_b[:, c1:c2], w_in_b[:, c2:c3],
                           w_in_b[:, c3:])
    cos_t, sa_t, sb_t = _rope_tables(T)
    pa, pqk, pv, pr, plr = _inproj(x2, row(ln1_w), wa, wqk, wv, wr, wl, row(q_norm_w), row(k_norm_w),
                                   cos_t, sa_t, sb_t)

    attn = _attention(pa, attn_sink.astype(F32), row(attn_out_norm_w))

    C = GLA_CHUNK
    ones = jnp.ones((C, C), F32)
    zr = jnp.zeros((GLA_RANK, GLA_QK_WIDTH), F32)
    up_f = jnp.concatenate([gate_up_f.astype(F32), zr], axis=0)
    up_b = jnp.concatenate([zr, gate_up_b.astype(F32)], axis=0)
    o_f = _gla_pass(False, pqk, pv, plr, up_f, row(gate_bias_f), jnp.tril(ones).astype(BF16))
    gla = _gla_pass(True, pqk, pv, plr, up_b, row(gate_bias_b), jnp.triu(ones).astype(BF16),
                    extra=(o_f, pr, row(gla_out_norm_w)))

    w_out_b = w_out.astype(BF16)
    wr_full = jnp.zeros((D_MODEL, ROUTER_ROWS), F32)
    wr_full = wr_full.at[:, :N_GROUPS].set(w_group.astype(F32))
    wr_full = wr_full.at[:, EXPERT_ROW0:EXPERT_ROW0 + N_EXPERTS].set(w_router.astype(F32))
    wr_hi, wr_lo = _split_hi_lo(wr_full)
    rbias = jnp.zeros((ROUTER_ROWS, 1), F32)
    rbias = rbias.at[:N_GROUPS, 0].set(b_group.astype(F32))
    rbias = rbias.at[EXPERT_ROW0:EXPERT_ROW0 + N_EXPERTS, 0].set(b_router.astype(F32))
    tm = OUT_ROWS
    earlier = jnp.triu(jnp.ones((tm, tm), F32), k=1).astype(BF16)
    h, xn, r_int, r_w, counts = _outproj_router(
        attn, gla, x2, w_out_b[:ATTN_WIDTH], w_out_b[ATTN_WIDTH:], row(ln2_w),
        jnp.concatenate([wr_hi, wr_lo], axis=1), wr_hi, rbias, earlier)

    bm = MOE_ROWS
    n_blocks = (T * TOP_K) // bm + N_EXPERTS
    cnt = counts[:, 0].astype(jnp.int32)
    padded = (cnt + bm - 1) // bm * bm
    pad_end = jnp.cumsum(padded)
    pad_start = pad_end - padded
    n_used = (pad_end[-1] // bm).astype(jnp.int32).reshape(1)
    block_e = jnp.minimum(jnp.sum(pad_end[None, :] <= (jnp.arange(n_blocks) * bm)[:, None], axis=1),
                          N_EXPERTS - 1).astype(jnp.int32)
    e_sel = r_int[0:TOP_K].T
    start_sel = jnp.sum(jnp.where(e_sel[..., None] == jnp.arange(N_EXPERTS), pad_start, 0), axis=-1)
    slots = (start_sel + r_int[TOP_K:2 * TOP_K].T).reshape(-1).astype(jnp.int32)

    xs = _dispatch(slots, xn, n_blocks * bm)
    ys = _moe(block_e, n_used, xs, w_gate_e.astype(BF16), w_up_e.astype(BF16), w_down_e.astype(BF16))
    return _combine(slots, h, r_w[0:TOP_K].T, ys)


def kernel(x, ln1_w, w_in, q_norm_w, k_norm_w, attn_sink, attn_out_norm_w, gla_gate_up_f, gla_gate_bias_f,
           gla_gate_up_b, gla_gate_bias_b, gla_out_norm_w, w_out, ln2_w, w_group, b_group, w_router,
           b_router, w_gate_e, w_up_e, w_down_e):
    B, S, D = x.shape
    h = x.reshape(B * S, D)
    assert B == 1
    for l in range(ln1_w.shape[0]):
        h = _layer(h, ln1_w[l], w_in[l], q_norm_w[l], k_norm_w[l], attn_sink[l], attn_out_norm_w[l],
                   gla_gate_up_f[l], gla_gate_bias_f[l], gla_gate_up_b[l], gla_gate_bias_b[l],
                   gla_out_norm_w[l], w_out[l], ln2_w[l], w_group[l], b_group[l], w_router[l],
                   b_router[l], w_gate_e[l], w_up_e[l], w_down_e[l])
    return h.reshape(B, S, D)
```

```python
import functools

import jax
import jax.numpy as jnp
from jax import lax
from jax.experimental import pallas as pl
from jax.experimental.pallas import tpu as pltpu

F32 = jnp.float32
BF16 = jnp.bfloat16

EPS = 1e-6
D_MODEL = 2048

ATTN_Q_HEADS = 8
ATTN_KV_HEADS = 2
ATTN_GROUP = ATTN_Q_HEADS // ATTN_KV_HEADS
HEAD_DIM = 128
WINDOW = 128
ATTN_BLOCK = 128
ROPE_THETA = 500000.0
ROPE_DIM = HEAD_DIM // 4
ROPE_HALF = ROPE_DIM // 2
ATTN_WIDTH = ATTN_Q_HEADS * HEAD_DIM
KV_WIDTH = ATTN_KV_HEADS * HEAD_DIM
A_WIDTH = ATTN_WIDTH + 2 * KV_WIDTH

GLA_HEADS = 4
GLA_DK = 128
GLA_DV = 256
GLA_RANK = 16
GLA_TAU = 16.0
GLA_QK_WIDTH = GLA_HEADS * GLA_DK
GLA_WIDTH = GLA_HEADS * GLA_DV
GLA_CHUNK = 128

N_GROUPS = 4
EXPERTS_PER_GROUP = 8
N_EXPERTS = N_GROUPS * EXPERTS_PER_GROUP
TOP_K = 2
D_FF = 1024
ROUTER_ROWS = 128
EXPERT_ROW0 = 8

PROJ_ROWS = 512
OUT_ROWS = 256
MOE_ROWS = 256
COMBINE_ROWS = 256

VMEM_LIMIT = 56 * 1024 * 1024


def _dot(a, b):
    return jnp.dot(a, b, preferred_element_type=F32)


def _dot_nt(a, b):
    return lax.dot_general(a, b, (((1,), (1,)), ((), ())), preferred_element_type=F32)


def _dot_tn(a, b):
    return lax.dot_general(a, b, (((0,), (0,)), ((), ())), preferred_element_type=F32)


def _resident(shape):
    nd = len(shape)
    return pl.BlockSpec(shape, lambda *_: (0,) * nd, pipeline_mode=pl.Buffered(1))


def _inproj_body(x_ref, ln_ref, wa_ref, wqk_ref, wv_ref, wr_ref, wl_ref, qn_ref, kn_ref,
                 cos_ref, sa_ref, sb_ref, oa_ref, oqk_ref, ov_ref, or_ref, ol_ref):
    x = x_ref[...]
    ms = jnp.mean(x * x, axis=-1, keepdims=True)
    xn = (x * lax.rsqrt(ms + EPS) * ln_ref[...]).astype(BF16)
    oqk_ref[...] = _dot(xn, wqk_ref[...]).astype(BF16)
    ov_ref[...] = _dot(xn, wv_ref[...]).astype(BF16)
    or_ref[...] = _dot(xn, wr_ref[...]).astype(BF16)
    ol_ref[...] = _dot(xn, wl_ref[...])
    acc = _dot(xn, wa_ref[...])
    cos, sa, sb = cos_ref[...], sa_ref[...], sb_ref[...]
    for c in range(ATTN_Q_HEADS + ATTN_KV_HEADS):
        xh = acc[:, c * HEAD_DIM:(c + 1) * HEAD_DIM]
        w = qn_ref[...] if c < ATTN_Q_HEADS else kn_ref[...]
        y = xh * lax.rsqrt(jnp.mean(xh * xh, axis=-1, keepdims=True) + EPS) * w
        y = (y * cos + pltpu.roll(y, ROPE_HALF, 1) * sa
             + pltpu.roll(y, HEAD_DIM - ROPE_HALF, 1) * sb)
        if c < ATTN_Q_HEADS:
            y = y * (HEAD_DIM ** -0.5)
        oa_ref[:, c * HEAD_DIM:(c + 1) * HEAD_DIM] = y.astype(BF16)
    oa_ref[:, ATTN_WIDTH + KV_WIDTH:] = acc[:, ATTN_WIDTH + KV_WIDTH:].astype(BF16)


def _inproj(x2, ln_w, wa, wqk, wv, wr, wl, qn, kn, cos_t, sa_t, sb_t):
    T = x2.shape[0]
    tm = PROJ_ROWS
    row = lambda w: pl.BlockSpec((tm, w), lambda i: (i, 0))
    return pl.pallas_call(
        _inproj_body,
        grid=(T // tm,),
        in_specs=[row(D_MODEL), _resident((1, D_MODEL)),
                  _resident(wa.shape), _resident(wqk.shape), _resident(wv.shape),
                  _resident(wr.shape), _resident(wl.shape),
                  _resident((1, HEAD_DIM)), _resident((1, HEAD_DIM)),
                  row(HEAD_DIM), row(HEAD_DIM), row(HEAD_DIM)],
        out_specs=[row(A_WIDTH), row(2 * GLA_QK_WIDTH), row(GLA_WIDTH), row(GLA_WIDTH),
                   row(2 * GLA_RANK)],
        out_shape=[jax.ShapeDtypeStruct((T, A_WIDTH), BF16),
                   jax.ShapeDtypeStruct((T, 2 * GLA_QK_WIDTH), BF16),
                   jax.ShapeDtypeStruct((T, GLA_WIDTH), BF16),
                   jax.ShapeDtypeStruct((T, GLA_WIDTH), BF16),
                   jax.ShapeDtypeStruct((T, 2 * GLA_RANK), F32)],
        compiler_params=pltpu.CompilerParams(dimension_semantics=("parallel",),
                                             vmem_limit_bytes=VMEM_LIMIT),
        name="inproj",
    )(x2, ln_w, wa, wqk, wv, wr, wl, qn, kn, cos_t, sa_t, sb_t)


def _attn_body(sink_ref, q_ref, kp_ref, kc_ref, kn_ref, vp_ref, vc_ref, vn_ref, nw_ref, o_ref):
    n = pl.program_id(0)
    nb = pl.num_programs(0)
    rows = ATTN_GROUP * ATTN_BLOCK
    keys = 3 * ATTN_BLOCK
    r = lax.broadcasted_iota(jnp.int32, (rows, keys), 0) & (ATTN_BLOCK - 1)
    c = lax.broadcasted_iota(jnp.int32, (rows, keys), 1)
    valid = (c >= r + (ATTN_BLOCK - WINDOW)) & (c <= r + (ATTN_BLOCK + WINDOW))
    valid = valid & ((c >= ATTN_BLOCK) | (n > 0)) & ((c < 2 * ATTN_BLOCK) | (n < nb - 1))
    outs = []
    for g in range(ATTN_KV_HEADS):
        ks = slice(g * HEAD_DIM, (g + 1) * HEAD_DIM)
        k3 = jnp.concatenate([kp_ref[:, ks], kc_ref[:, ks], kn_ref[:, ks]], axis=0)
        v3 = jnp.concatenate([vp_ref[:, ks], vc_ref[:, ks], vn_ref[:, ks]], axis=0)
        heads = range(g * ATTN_GROUP, (g + 1) * ATTN_GROUP)
        q4 = jnp.concatenate([q_ref[:, h * HEAD_DIM:(h + 1) * HEAD_DIM] for h in heads], axis=0)
        s = jnp.where(valid, _dot_nt(q4, k3), -jnp.inf)
        sink = jnp.concatenate([jnp.full((ATTN_BLOCK, 1), sink_ref[h], F32) for h in heads], axis=0)
        m = jnp.maximum(jnp.max(s, axis=-1, keepdims=True), sink)
        e = jnp.exp(s - m)
        denom = jnp.sum(e, axis=-1, keepdims=True) + jnp.exp(sink - m)
        o = _dot(e.astype(BF16), v3) / denom
        outs.extend(o[i * ATTN_BLOCK:(i + 1) * ATTN_BLOCK] for i in range(ATTN_GROUP))
    a = jnp.concatenate(outs, axis=1)
    a = a * lax.rsqrt(jnp.mean(a * a, axis=-1, keepdims=True) + EPS) * nw_ref[...]
    o_ref[...] = a.astype(BF16)


def _attention(pa, sink, norm_w):
    T = pa.shape[0]
    nb = T // ATTN_BLOCK
    kcol = ATTN_WIDTH // KV_WIDTH
    vcol = kcol + 1
    prev = lambda n, _: jnp.maximum(n - 1, 0)
    cur = lambda n, _: n
    nxt = lambda n, _: jnp.minimum(n + 1, nb - 1)
    kv = lambda f, col: pl.BlockSpec((ATTN_BLOCK, KV_WIDTH), lambda n, s: (f(n, s), col))
    grid_spec = pltpu.PrefetchScalarGridSpec(
        num_scalar_prefetch=1,
        grid=(nb,),
        in_specs=[pl.BlockSpec((ATTN_BLOCK, ATTN_WIDTH), lambda n, s: (n, 0)),
                  kv(prev, kcol), kv(cur, kcol), kv(nxt, kcol),
                  kv(prev, vcol), kv(cur, vcol), kv(nxt, vcol),
                  pl.BlockSpec((1, ATTN_WIDTH), lambda n, s: (0, 0))],
        out_specs=pl.BlockSpec((ATTN_BLOCK, ATTN_WIDTH), lambda n, s: (n, 0)),
    )
    return pl.pallas_call(
        _attn_body,
        grid_spec=grid_spec,
        out_shape=jax.ShapeDtypeStruct((T, ATTN_WIDTH), BF16),
        compiler_params=pltpu.CompilerParams(dimension_semantics=("parallel",),
                                             vmem_limit_bytes=VMEM_LIMIT),
        name="attention",
    )(sink, pa, pa, pa, pa, pa, pa, pa, norm_w)


def _split3(x):
    hi = x.astype(BF16)
    r1 = x - hi.astype(F32)
    mid = r1.astype(BF16)
    lo = (r1 - mid.astype(F32)).astype(BF16)
    return hi, mid, lo


def _gla_body(reverse, final, *refs):
    if final:
        (q_ref, k_ref, v_ref, lr_ref, up_ref, bias_ref, tri_ref, of_ref, gr_ref, nw_ref,
         o_ref, st_ref) = refs
    else:
        q_ref, k_ref, v_ref, lr_ref, up_ref, bias_ref, tri_ref, o_ref, st_ref = refs
    C = GLA_CHUNK

    @pl.when(pl.program_id(0) == 0)
    def _():
        st_ref[...] = jnp.zeros_like(st_ref)

    g = jnp.dot(lr_ref[...], up_ref[...], preferred_element_type=F32,
                precision=lax.Precision.HIGHEST) + bias_ref[...]
    la = jax.nn.log_sigmoid(g) * (1.0 / GLA_TAU)
    tri = tri_ref[...]
    hi, mid, lo = _split3(la)
    b = _dot(tri, hi) + _dot(tri, mid) + _dot(tri, lo)
    end = 0 if reverse else C - 1
    b_end = b[end:end + 1]
    b_mid = b[C // 2:C // 2 + 1]
    scale = GLA_DK ** -0.5
    q = q_ref[...].astype(F32) * scale
    k = k_ref[...].astype(F32)
    q_in = (q * jnp.exp(b - b_mid)).astype(BF16)
    k_in = (k * jnp.exp(b_mid - b)).astype(BF16)
    q_dec = (q * jnp.exp(b)).astype(BF16)
    k_end = (k * jnp.exp(b_end - b)).astype(BF16)
    decay = jnp.exp(b_end)
    ri = lax.broadcasted_iota(jnp.int32, (C, C), 0)
    ci = lax.broadcasted_iota(jnp.int32, (C, C), 1)
    causal = (ri <= ci) if reverse else (ri >= ci)
    for h in range(GLA_HEADS):
        ks = slice(h * GLA_DK, (h + 1) * GLA_DK)
        vs = slice(h * GLA_DV, (h + 1) * GLA_DV)
        vh = v_ref[:, vs]
        att = jnp.where(causal, _dot_nt(q_in[:, ks], k_in[:, ks]), 0.0).astype(BF16)
        st = st_ref[h]
        o = _dot(att, vh) + _dot_nt(q_dec[:, ks], st.astype(BF16))
        st_ref[h] = st * decay[:, ks] + _dot_tn(vh, k_end[:, ks])
        if final:
            tot = o + of_ref[:, vs]
            y = tot * lax.rsqrt(jnp.mean(tot * tot, axis=-1, keepdims=True) + EPS) * nw_ref[...]
            o_ref[:, vs] = (y * jax.nn.silu(gr_ref[:, vs].astype(F32))).astype(BF16)
        else:
            o_ref[:, vs] = o


def _gla_pass(reverse, pqk, pv, plr, up, bias, tri, extra=None):
    T = pqk.shape[0]
    C = GLA_CHUNK
    n = T // C
    final = extra is not None
    blk = (lambda i: n - 1 - i) if reverse else (lambda i: i)
    row = lambda w, col=0: pl.BlockSpec((C, w), lambda i: (blk(i), col))
    in_specs = [row(GLA_QK_WIDTH, 0), row(GLA_QK_WIDTH, 1), row(GLA_WIDTH), row(2 * GLA_RANK),
                _resident(up.shape), _resident(bias.shape), _resident(tri.shape)]
    args = [pqk, pqk, pv, plr, up, bias, tri]
    if final:
        o_f, pr, norm_w = extra
        in_specs += [row(GLA_WIDTH), row(GLA_WIDTH), _resident(norm_w.shape)]
        args += [o_f, pr, norm_w]
    return pl.pallas_call(
        functools.partial(_gla_body, reverse, final),
        grid=(n,),
        in_specs=in_specs,
        out_specs=row(GLA_WIDTH),
        out_shape=jax.ShapeDtypeStruct((T, GLA_WIDTH), BF16 if final else F32),
        scratch_shapes=[pltpu.VMEM((GLA_HEADS, GLA_DV, GLA_DK), F32)],
        compiler_params=pltpu.CompilerParams(dimension_semantics=("arbitrary",),
                                             vmem_limit_bytes=VMEM_LIMIT),
        name="gla_bwd" if reverse else "gla_fwd",
    )(*args)


def _outproj_body(a_ref, g_ref, x_ref, wa_ref, wg_ref, ln_ref, wr_hl_ref, wr_hi_ref, rb_ref, tri_ref,
                  h_ref, xn_ref, ri_ref, rw_ref, cnt_ref):
    tm = x_ref.shape[0]

    @pl.when(pl.program_id(0) == 0)
    def _():
        cnt_ref[...] = jnp.zeros_like(cnt_ref)

    h = x_ref[...] + _dot(a_ref[...], wa_ref[...]) + _dot(g_ref[...], wg_ref[...])
    h_ref[...] = h
    xn = h * lax.rsqrt(jnp.mean(h * h, axis=-1, keepdims=True) + EPS) * ln_ref[...]
    xn_ref[...] = xn

    x_hi = xn.astype(BF16)
    x_lo = (xn - x_hi.astype(F32)).astype(BF16)
    l2 = _dot(x_hi, wr_hl_ref[...])
    logits = l2[:, :ROUTER_ROWS] + l2[:, ROUTER_ROWS:] + _dot(x_lo, wr_hi_ref[...])
    lt = jnp.transpose(logits) + rb_ref[...]

    gl = lt[0:N_GROUPS]
    gmax = jnp.max(gl, axis=0, keepdims=True)
    gi = lax.broadcasted_iota(jnp.int32, gl.shape, 0).astype(F32)
    g_sel = jnp.min(jnp.where(gl == gmax, gi, float(N_GROUPS)), axis=0, keepdims=True)
    g_gate = 1.0 / jnp.sum(jnp.exp(gl - gmax), axis=0, keepdims=True)

    el = lt[EXPERT_ROW0:EXPERT_ROW0 + N_EXPERTS]
    ei_int = lax.broadcasted_iota(jnp.int32, el.shape, 0)
    ei = ei_int.astype(F32)
    grp = (ei_int >> 3).astype(F32)
    cand = jnp.where(grp == g_sel, el, -jnp.inf)
    v1 = jnp.max(cand, axis=0, keepdims=True)
    e1 = jnp.min(jnp.where(cand == v1, ei, float(N_EXPERTS)), axis=0, keepdims=True)
    cand2 = jnp.where(ei == e1, -jnp.inf, cand)
    v2 = jnp.max(cand2, axis=0, keepdims=True)
    e2 = jnp.min(jnp.where(cand2 == v2, ei, float(N_EXPERTS)), axis=0, keepdims=True)
    d = jnp.exp(v2 - v1)
    w1 = g_gate / (1.0 + d)
    w2 = g_gate * d / (1.0 + d)

    oh1 = (ei == e1).astype(F32)
    oh2 = (ei == e2).astype(F32)
    cnt = oh1 + oh2
    before = _dot(cnt.astype(BF16), tri_ref[...]) + cnt_ref[:, 0:1]
    r1 = jnp.sum(oh1 * before, axis=0, keepdims=True)
    r2 = jnp.sum(oh2 * before, axis=0, keepdims=True)
    cnt_ref[...] = cnt_ref[...] + jnp.sum(cnt, axis=1, keepdims=True)

    ri_ref[...] = jnp.concatenate([e1, e2, r1, r2, jnp.zeros((4, tm), F32)], axis=0).astype(jnp.int32)
    rw_ref[...] = jnp.concatenate([w1, w2, jnp.zeros((6, tm), F32)], axis=0)


def _outproj_router(attn, gla, x2, w_attn, w_gla, ln_w, wr_hl, wr_hi, rbias, tri):
    T = x2.shape[0]
    tm = OUT_ROWS
    row = lambda w: pl.BlockSpec((tm, w), lambda i: (i, 0))
    col = lambda r: pl.BlockSpec((r, tm), lambda i: (0, i))
    return pl.pallas_call(
        _outproj_body,
        grid=(T // tm,),
        in_specs=[row(ATTN_WIDTH), row(GLA_WIDTH), row(D_MODEL),
                  _resident(w_attn.shape), _resident(w_gla.shape), _resident(ln_w.shape),
                  _resident(wr_hl.shape), _resident(wr_hi.shape), _resident(rbias.shape),
                  _resident(tri.shape)],
        out_specs=[row(D_MODEL), row(D_MODEL), col(8), col(8),
                   pl.BlockSpec((N_EXPERTS, 128), lambda i: (0, 0))],
        out_shape=[jax.ShapeDtypeStruct((T, D_MODEL), F32),
                   jax.ShapeDtypeStruct((T, D_MODEL), F32),
                   jax.ShapeDtypeStruct((8, T), jnp.int32),
                   jax.ShapeDtypeStruct((8, T), F32),
                   jax.ShapeDtypeStruct((N_EXPERTS, 128), F32)],
        compiler_params=pltpu.CompilerParams(dimension_semantics=("arbitrary",),
                                             vmem_limit_bytes=VMEM_LIMIT),
        name="outproj_router",
    )(attn, gla, x2, w_attn, w_gla, ln_w, wr_hl, wr_hi, rbias, tri)


def _row_copy(src, s, dst, d, sem):
    return pltpu.make_async_copy(src.at[pl.ds(s, 1)], dst.at[pl.ds(d, 1)], sem)


def _moe_body(be_ref, nu_ref, slot_ref, lo_ref, hi_ref, xn_ref, wg_ref, wu_ref, wd_ref, y_ref,
              xbuf, xb_ref, tok_ref, sem):
    bm = MOE_ROWS
    b = pl.program_id(0)
    n_used = nu_ref[0]
    cur = b % 2

    def gather_start(block, buf, unrolled):
        def one(i):
            _row_copy(xn_ref, tok_ref[block * bm + i], xbuf.at[buf], i, sem.at[buf]).start()
        if unrolled:
            for i in range(bm):
                one(i)
        else:
            lax.fori_loop(0, bm, lambda i, c: (one(i), c)[1], 0)

    def gather_wait(buf):
        pltpu.make_async_copy(xn_ref.at[pl.ds(0, bm)], xbuf.at[buf], sem.at[buf]).wait()

    @pl.when(b == 0)
    def _():
        def per_expert(e, c):
            def zero(s, c2):
                tok_ref[s] = 0
                return c2
            return lax.fori_loop(lo_ref[e], hi_ref[e], zero, c)
        lax.fori_loop(0, N_EXPERTS, per_expert, 0)

        def fill(t, c):
            for k in range(TOP_K):
                tok_ref[slot_ref[TOP_K * t + k]] = t
            return c
        lax.fori_loop(0, slot_ref.shape[0] // TOP_K, fill, 0, unroll=8)
        gather_start(0, 0, False)

    def stage_rows():
        gather_wait(cur)
        xb_ref[...] = xbuf[cur].astype(BF16)

    def experts():
        x = xb_ref[...]
        hid = jax.nn.silu(_dot(x, wg_ref[0])) * _dot(x, wu_ref[0])
        y_ref[...] = _dot(hid.astype(BF16), wd_ref[0])

    @pl.when(b + 1 < n_used)
    def _():
        stage_rows()
        gather_start(b + 1, 1 - cur, True)
        experts()

    @pl.when(b + 1 == n_used)
    def _():
        stage_rows()
        experts()

    @pl.when(b >= n_used)
    def _():
        y_ref[...] = jnp.zeros_like(y_ref)


def _moe(block_e, n_used, slots, pad_lo, pad_hi, xn, wg, wu, wd, n_slots):
    bm = MOE_ROWS
    wspec = lambda shape: pl.BlockSpec((1,) + shape, lambda b, be, *_: (be[b], 0, 0))
    grid_spec = pltpu.PrefetchScalarGridSpec(
        num_scalar_prefetch=5,
        grid=(n_slots // bm,),
        in_specs=[pl.BlockSpec(memory_space=pl.ANY),
                  wspec((D_MODEL, D_FF)), wspec((D_MODEL, D_FF)), wspec((D_FF, D_MODEL))],
        out_specs=pl.BlockSpec((bm, D_MODEL), lambda b, *_: (b, 0)),
        scratch_shapes=[pltpu.VMEM((2, bm, D_MODEL), F32), pltpu.VMEM((bm, D_MODEL), BF16),
                        pltpu.SMEM((n_slots,), jnp.int32), pltpu.SemaphoreType.DMA((2,))],
    )
    return pl.pallas_call(
        _moe_body,
        grid_spec=grid_spec,
        out_shape=jax.ShapeDtypeStruct((n_slots, D_MODEL), F32),
        compiler_params=pltpu.CompilerParams(dimension_semantics=("arbitrary",),
                                             vmem_limit_bytes=VMEM_LIMIT),
        name="moe",
    )(block_e, n_used, slots, pad_lo, pad_hi, xn, wg, wu, wd)


def _combine_body(slot_ref, h_ref, w_ref, ys_ref, o_ref, g_ref, sem):
    tb = COMBINE_ROWS
    i = pl.program_id(0)
    cur = i % 2

    def gather_start(block, buf):
        def one(t, c):
            for k in range(TOP_K):
                _row_copy(ys_ref, slot_ref[TOP_K * (block * tb + t) + k], g_ref.at[buf, k], t,
                          sem.at[buf]).start()
            return c
        lax.fori_loop(0, tb, one, 0, unroll=8)

    @pl.when(i == 0)
    def _():
        gather_start(0, 0)

    @pl.when(i + 1 < pl.num_programs(0))
    def _():
        gather_start(i + 1, 1 - cur)

    for k in range(TOP_K):
        pltpu.make_async_copy(ys_ref.at[pl.ds(0, tb)], g_ref.at[cur, k], sem.at[cur]).wait()
    o_ref[...] = h_ref[...] + w_ref[:, 0:1] * g_ref[cur, 0] + w_ref[:, 1:2] * g_ref[cur, 1]


def _combine(slots, h, wcol, ys):
    T = h.shape[0]
    tb = COMBINE_ROWS
    grid_spec = pltpu.PrefetchScalarGridSpec(
        num_scalar_prefetch=1,
        grid=(T // tb,),
        in_specs=[pl.BlockSpec((tb, D_MODEL), lambda i, s: (i, 0)),
                  pl.BlockSpec((tb, TOP_K), lambda i, s: (i, 0)),
                  pl.BlockSpec(memory_space=pl.ANY)],
        out_specs=pl.BlockSpec((tb, D_MODEL), lambda i, s: (i, 0)),
        scratch_shapes=[pltpu.VMEM((2, TOP_K, tb, D_MODEL), F32), pltpu.SemaphoreType.DMA((2,))],
    )
    return pl.pallas_call(
        _combine_body,
        grid_spec=grid_spec,
        out_shape=jax.ShapeDtypeStruct((T, D_MODEL), F32),
        compiler_params=pltpu.CompilerParams(dimension_semantics=("arbitrary",),
                                             vmem_limit_bytes=VMEM_LIMIT),
        name="combine",
    )(slots, h, wcol, ys)


def _rope_tables(T):
    inv_freq = jnp.power(jnp.float32(ROPE_THETA),
                         -jnp.arange(ROPE_HALF, dtype=F32) * (2.0 / ROPE_DIM))
    ang = jnp.arange(T).astype(F32)[:, None] * inv_freq[None, :]
    cos, sin = jnp.cos(ang), jnp.sin(ang)
    zeros = jnp.zeros((T, HEAD_DIM - ROPE_DIM), F32)
    zh = jnp.zeros((T, ROPE_HALF), F32)
    cos_t = jnp.concatenate([cos, cos, jnp.ones((T, HEAD_DIM - ROPE_DIM), F32)], axis=1)
    sa_t = jnp.concatenate([zh, sin, zeros], axis=1)
    sb_t = jnp.concatenate([-sin, zh, zeros], axis=1)
    return cos_t, sa_t, sb_t


def _split_hi_lo(w):
    hi = w.astype(BF16)
    lo = (w - hi.astype(F32)).astype(BF16)
    return hi, lo


def _layer(x2, ln1_w, w_in, q_norm_w, k_norm_w, attn_sink, attn_out_norm_w, gate_up_f, gate_bias_f,
           gate_up_b, gate_bias_b, gla_out_norm_w, w_out, ln2_w, w_group, b_group, w_router, b_router,
           w_gate_e, w_up_e, w_down_e):
    T = x2.shape[0]
    row = lambda v: v.reshape(1, -1).astype(F32)

    w_in_b = w_in.astype(BF16)
    c0, c1, c2, c3 = A_WIDTH, A_WIDTH + 2 * GLA_QK_WIDTH, A_WIDTH + 2 * GLA_QK_WIDTH + GLA_WIDTH, \
        A_WIDTH + 2 * GLA_QK_WIDTH + 2 * GLA_WIDTH
    wa, wqk, wv, wr, wl = (w_in_b[:, :c0], w_in_b[:, c0:c1], w_in_b[:, c1:c2], w_in_b[:, c2:c3],
                           w_in_b[:, c3:])
    cos_t, sa_t, sb_t = _rope_tables(T)
    pa, pqk, pv, pr, plr = _inproj(x2, row(ln1_w), wa, wqk, wv, wr, wl, row(q_norm_w), row(k_norm_w),
                                   cos_t, sa_t, sb_t)

    attn = _attention(pa, attn_sink.astype(F32), row(attn_out_norm_w))

    C = GLA_CHUNK
    ones = jnp.ones((C, C), F32)
    zr = jnp.zeros((GLA_RANK, GLA_QK_WIDTH), F32)
    up_f = jnp.concatenate([gate_up_f.astype(F32), zr], axis=0)
    up_b = jnp.concatenate([zr, gate_up_b.astype(F32)], axis=0)
    o_f = _gla_pass(False, pqk, pv, plr, up_f, row(gate_bias_f), jnp.tril(ones).astype(BF16))
    gla = _gla_pass(True, pqk, pv, plr, up_b, row(gate_bias_b), jnp.triu(ones).astype(BF16),
                    extra=(o_f, pr, row(gla_out_norm_w)))

    w_out_b = w_out.astype(BF16)
    wr_full = jnp.zeros((D_MODEL, ROUTER_ROWS), F32)
    wr_full = wr_full.at[:, :N_GROUPS].set(w_group.astype(F32))
    wr_full = wr_full.at[:, EXPERT_ROW0:EXPERT_ROW0 + N_EXPERTS].set(w_router.astype(F32))
    wr_hi, wr_lo = _split_hi_lo(wr_full)
    rbias = jnp.zeros((ROUTER_ROWS, 1), F32)
    rbias = rbias.at[:N_GROUPS, 0].set(b_group.astype(F32))
    rbias = rbias.at[EXPERT_ROW0:EXPERT_ROW0 + N_EXPERTS, 0].set(b_router.astype(F32))
    tm = OUT_ROWS
    earlier = jnp.triu(jnp.ones((tm, tm), F32), k=1).astype(BF16)
    h, xn, r_int, r_w, counts = _outproj_router(
        attn, gla, x2, w_out_b[:ATTN_WIDTH], w_out_b[ATTN_WIDTH:], row(ln2_w),
        jnp.concatenate([wr_hi, wr_lo], axis=1), wr_hi, rbias, earlier)

    bm = MOE_ROWS
    n_blocks = (T * TOP_K) // bm + N_EXPERTS
    cnt = counts[:, 0].astype(jnp.int32)
    padded = (cnt + bm - 1) // bm * bm
    pad_end = jnp.cumsum(padded)
    pad_start = pad_end - padded
    n_used = (pad_end[-1] // bm).astype(jnp.int32).reshape(1)
    block_e = jnp.minimum(jnp.sum(pad_end[None, :] <= (jnp.arange(n_blocks) * bm)[:, None], axis=1),
                          N_EXPERTS - 1).astype(jnp.int32)
    e_sel = r_int[0:TOP_K].T
    start_sel = jnp.sum(jnp.where(e_sel[..., None] == jnp.arange(N_EXPERTS), pad_start, 0), axis=-1)
    slots = (start_sel + r_int[TOP_K:2 * TOP_K].T).reshape(-1).astype(jnp.int32)

    ys = _moe(block_e, n_used, slots, (pad_start + cnt).astype(jnp.int32), pad_end.astype(jnp.int32), xn,
              w_gate_e.astype(BF16), w_up_e.astype(BF16), w_down_e.astype(BF16), n_blocks * bm)
    return _combine(slots, h, r_w[0:TOP_K].T, ys)


def kernel(x, ln1_w, w_in, q_norm_w, k_norm_w, attn_sink, attn_out_norm_w, gla_gate_up_f, gla_gate_bias_f,
           gla_gate_up_b, gla_gate_bias_b, gla_out_norm_w, w_out, ln2_w, w_group, b_group, w_router,
           b_router, w_gate_e, w_up_e, w_down_e):
    B, S, D = x.shape
    h = x.reshape(B * S, D)
    assert B == 1
    for l in range(ln1_w.shape[0]):
        h = _layer(h, ln1_w[l], w_in[l], q_norm_w[l], k_norm_w[l], attn_sink[l], attn_out_norm_w[l],
                   gla_gate_up_f[l], gla_gate_bias_f[l], gla_gate_up_b[l], gla_gate_bias_b[l],
                   gla_out_norm_w[l], w_out[l], ln2_w[l], w_group[l], b_group[l], w_router[l],
                   b_router[l], w_gate_e[l], w_up_e[l], w_down_e[l])
    return h.reshape(B, S, D)
```

```python
import functools

import jax
import jax.numpy as jnp
from jax import lax
from jax.experimental import pallas as pl
from jax.experimental.pallas import tpu as pltpu

F32 = jnp.float32
BF16 = jnp.bfloat16

EPS = 1e-6
D_MODEL = 2048

ATTN_Q_HEADS = 8
ATTN_KV_HEADS = 2
ATTN_GROUP = ATTN_Q_HEADS // ATTN_KV_HEADS
HEAD_DIM = 128
WINDOW = 128
ATTN_BLOCK = 128
ATTN_STEP_BLOCKS = 4
ROPE_THETA = 500000.0
ROPE_DIM = HEAD_DIM // 4
ROPE_HALF = ROPE_DIM // 2
ATTN_WIDTH = ATTN_Q_HEADS * HEAD_DIM
KV_WIDTH = ATTN_KV_HEADS * HEAD_DIM
A_WIDTH = ATTN_WIDTH + 2 * KV_WIDTH

GLA_HEADS = 4
GLA_DK = 128
GLA_DV = 256
GLA_RANK = 16
GLA_TAU = 16.0
GLA_QK_WIDTH = GLA_HEADS * GLA_DK
GLA_WIDTH = GLA_HEADS * GLA_DV
GLA_CHUNK = 128

N_GROUPS = 4
EXPERTS_PER_GROUP = 8
N_EXPERTS = N_GROUPS * EXPERTS_PER_GROUP
TOP_K = 2
D_FF = 1024
ROUTER_ROWS = 128
EXPERT_ROW0 = 8

PROJ_ROWS = 512
OUT_ROWS = 256
MOE_ROWS = 256
MOE_ROW_BUFFERS = 3
COMBINE_ROWS = 256

VMEM_LIMIT = 56 * 1024 * 1024


def _dot(a, b):
    return jnp.dot(a, b, preferred_element_type=F32)


def _dot_nt(a, b):
    return lax.dot_general(a, b, (((1,), (1,)), ((), ())), preferred_element_type=F32)


def _dot_tn(a, b):
    return lax.dot_general(a, b, (((0,), (0,)), ((), ())), preferred_element_type=F32)


def _resident(shape):
    nd = len(shape)
    return pl.BlockSpec(shape, lambda *_: (0,) * nd, pipeline_mode=pl.Buffered(1))


def _inproj_body(x_ref, ln_ref, wa_ref, wqk_ref, wv_ref, wr_ref, wl_ref, qn_ref, kn_ref,
                 cos_ref, sa_ref, sb_ref, oa_ref, oqk_ref, ov_ref, or_ref, ol_ref):
    x = x_ref[...]
    ms = jnp.mean(x * x, axis=-1, keepdims=True)
    xn = (x * lax.rsqrt(ms + EPS) * ln_ref[...]).astype(BF16)
    oqk_ref[...] = _dot(xn, wqk_ref[...]).astype(BF16)
    ov_ref[...] = _dot(xn, wv_ref[...]).astype(BF16)
    or_ref[...] = _dot(xn, wr_ref[...]).astype(BF16)
    ol_ref[...] = _dot(xn, wl_ref[...])
    acc = _dot(xn, wa_ref[...])
    cos, sa, sb = cos_ref[...], sa_ref[...], sb_ref[...]
    for c in range(ATTN_Q_HEADS + ATTN_KV_HEADS):
        xh = acc[:, c * HEAD_DIM:(c + 1) * HEAD_DIM]
        w = qn_ref[...] if c < ATTN_Q_HEADS else kn_ref[...]
        y = xh * lax.rsqrt(jnp.mean(xh * xh, axis=-1, keepdims=True) + EPS) * w
        y = (y * cos + pltpu.roll(y, ROPE_HALF, 1) * sa
             + pltpu.roll(y, HEAD_DIM - ROPE_HALF, 1) * sb)
        if c < ATTN_Q_HEADS:
            y = y * (HEAD_DIM ** -0.5)
        oa_ref[:, c * HEAD_DIM:(c + 1) * HEAD_DIM] = y.astype(BF16)
    oa_ref[:, ATTN_WIDTH + KV_WIDTH:] = acc[:, ATTN_WIDTH + KV_WIDTH:].astype(BF16)


def _inproj(x2, ln_w, wa, wqk, wv, wr, wl, qn, kn, cos_t, sa_t, sb_t):
    T = x2.shape[0]
    tm = PROJ_ROWS
    row = lambda w: pl.BlockSpec((tm, w), lambda i: (i, 0))
    return pl.pallas_call(
        _inproj_body,
        grid=(T // tm,),
        in_specs=[row(D_MODEL), _resident((1, D_MODEL)),
                  _resident(wa.shape), _resident(wqk.shape), _resident(wv.shape),
                  _resident(wr.shape), _resident(wl.shape),
                  _resident((1, HEAD_DIM)), _resident((1, HEAD_DIM)),
                  row(HEAD_DIM), row(HEAD_DIM), row(HEAD_DIM)],
        out_specs=[row(A_WIDTH), row(2 * GLA_QK_WIDTH), row(GLA_WIDTH), row(GLA_WIDTH),
                   row(2 * GLA_RANK)],
        out_shape=[jax.ShapeDtypeStruct((T, A_WIDTH), BF16),
                   jax.ShapeDtypeStruct((T, 2 * GLA_QK_WIDTH), BF16),
                   jax.ShapeDtypeStruct((T, GLA_WIDTH), BF16),
                   jax.ShapeDtypeStruct((T, GLA_WIDTH), BF16),
                   jax.ShapeDtypeStruct((T, 2 * GLA_RANK), F32)],
        compiler_params=pltpu.CompilerParams(dimension_semantics=("parallel",),
                                             vmem_limit_bytes=VMEM_LIMIT),
        name="inproj",
    )(x2, ln_w, wa, wqk, wv, wr, wl, qn, kn, cos_t, sa_t, sb_t)


def _attn_body(sink_ref, q_ref, kp_ref, kc_ref, kn_ref, vp_ref, vc_ref, vn_ref, nw_ref, o_ref):
    n = pl.program_id(0)
    nb = pl.num_programs(0)
    rows = ATTN_GROUP * ATTN_BLOCK
    keys = 3 * ATTN_BLOCK
    r = lax.broadcasted_iota(jnp.int32, (rows, keys), 0) & (ATTN_BLOCK - 1)
    c = lax.broadcasted_iota(jnp.int32, (rows, keys), 1)
    band = (c >= r + (ATTN_BLOCK - WINDOW)) & (c <= r + (ATTN_BLOCK + WINDOW))
    edge_mask = [band & ((c >= ATTN_BLOCK) | (n > 0))] + [band] * (ATTN_STEP_BLOCKS - 2) + \
                [band & ((c < 2 * ATTN_BLOCK) | (n < nb - 1))]
    def window(j, prev_ref, cur_ref, next_ref, ks):
        blocks = [prev_ref[:, ks]] + [cur_ref[i * ATTN_BLOCK:(i + 1) * ATTN_BLOCK, ks]
                                      for i in range(ATTN_STEP_BLOCKS)] + [next_ref[:, ks]]
        return jnp.concatenate(blocks[j:j + 3], axis=0)

    chains = [(j, g) for j in range(ATTN_STEP_BLOCKS) for g in range(ATTN_KV_HEADS)]
    heads_of = lambda g: range(g * ATTN_GROUP, (g + 1) * ATTN_GROUP)
    ks_of = lambda g: slice(g * HEAD_DIM, (g + 1) * HEAD_DIM)
    qrows_of = lambda j: slice(j * ATTN_BLOCK, (j + 1) * ATTN_BLOCK)
    s = [jnp.where(edge_mask[j],
                   _dot_nt(jnp.concatenate([q_ref[qrows_of(j), h * HEAD_DIM:(h + 1) * HEAD_DIM]
                                            for h in heads_of(g)], axis=0),
                           window(j, kp_ref, kc_ref, kn_ref, ks_of(g))), -jnp.inf)
         for j, g in chains]
    sink = [jnp.concatenate([jnp.full((ATTN_BLOCK, 1), sink_ref[h], F32) for h in heads_of(g)], axis=0)
            for j, g in chains]
    m = [jnp.maximum(jnp.max(si, axis=-1, keepdims=True), sk) for si, sk in zip(s, sink)]
    e = [jnp.exp(si - mi) for si, mi in zip(s, m)]
    denom = [jnp.sum(ei, axis=-1, keepdims=True) + jnp.exp(sk - mi) for ei, sk, mi in zip(e, sink, m)]
    o = [_dot(ei.astype(BF16), window(j, vp_ref, vc_ref, vn_ref, ks_of(g))) / di
         for (j, g), ei, di in zip(chains, e, denom)]
    for j in range(ATTN_STEP_BLOCKS):
        outs = [o[chains.index((j, g))][i * ATTN_BLOCK:(i + 1) * ATTN_BLOCK]
                for g in range(ATTN_KV_HEADS) for i in range(ATTN_GROUP)]
        a = jnp.concatenate(outs, axis=1)
        a = a * lax.rsqrt(jnp.mean(a * a, axis=-1, keepdims=True) + EPS) * nw_ref[...]
        o_ref[qrows_of(j), :] = a.astype(BF16)


def _attention(pa, sink, norm_w):
    T = pa.shape[0]
    sb = ATTN_STEP_BLOCKS
    nb = T // ATTN_BLOCK
    kcol = ATTN_WIDTH // KV_WIDTH
    vcol = kcol + 1
    prev = lambda col: pl.BlockSpec((ATTN_BLOCK, KV_WIDTH), lambda n, s: (jnp.maximum(sb * n - 1, 0), col))
    nxt = lambda col: pl.BlockSpec((ATTN_BLOCK, KV_WIDTH),
                                   lambda n, s: (jnp.minimum(sb * n + sb, nb - 1), col))
    cur = lambda col: pl.BlockSpec((sb * ATTN_BLOCK, KV_WIDTH), lambda n, s: (n, col))
    grid_spec = pltpu.PrefetchScalarGridSpec(
        num_scalar_prefetch=1,
        grid=(nb // sb,),
        in_specs=[pl.BlockSpec((sb * ATTN_BLOCK, ATTN_WIDTH), lambda n, s: (n, 0)),
                  prev(kcol), cur(kcol), nxt(kcol), prev(vcol), cur(vcol), nxt(vcol),
                  pl.BlockSpec((1, ATTN_WIDTH), lambda n, s: (0, 0))],
        out_specs=pl.BlockSpec((sb * ATTN_BLOCK, ATTN_WIDTH), lambda n, s: (n, 0)),
    )
    return pl.pallas_call(
        _attn_body,
        grid_spec=grid_spec,
        out_shape=jax.ShapeDtypeStruct((T, ATTN_WIDTH), BF16),
        compiler_params=pltpu.CompilerParams(dimension_semantics=("parallel",),
                                             vmem_limit_bytes=VMEM_LIMIT),
        name="attention",
    )(sink, pa, pa, pa, pa, pa, pa, pa, norm_w)


def _split3(x):
    hi = x.astype(BF16)
    r1 = x - hi.astype(F32)
    mid = r1.astype(BF16)
    lo = (r1 - mid.astype(F32)).astype(BF16)
    return hi, mid, lo


def _gla_body(reverse, final, *refs):
    if final:
        (q_ref, k_ref, v_ref, lr_ref, up_ref, bias_ref, tri_ref, of_ref, gr_ref, nw_ref,
         o_ref, st_ref) = refs
    else:
        q_ref, k_ref, v_ref, lr_ref, up_ref, bias_ref, tri_ref, o_ref, st_ref = refs
    C = GLA_CHUNK

    @pl.when(pl.program_id(0) == 0)
    def _():
        st_ref[...] = jnp.zeros_like(st_ref)

    lr = lr_ref[...]
    lr_hi = lr.astype(BF16)
    lr_lo = (lr - lr_hi.astype(F32)).astype(BF16)
    g = _dot(jnp.concatenate([lr_hi, lr_lo, lr_hi], axis=1), up_ref[...]) + bias_ref[...]
    la = jax.nn.log_sigmoid(g) * (1.0 / GLA_TAU)
    tri = tri_ref[...]
    hi, mid, lo = _split3(la)
    b = _dot(tri, hi) + _dot(tri, mid) + _dot(tri, lo)
    end = 0 if reverse else C - 1
    b_end = b[end:end + 1]
    b_mid = b[C // 2:C // 2 + 1]
    scale = GLA_DK ** -0.5
    q = q_ref[...].astype(F32) * scale
    k = k_ref[...].astype(F32)
    q_in = (q * jnp.exp(b - b_mid)).astype(BF16)
    k_in = (k * jnp.exp(b_mid - b)).astype(BF16)
    q_dec = (q * jnp.exp(b)).astype(BF16)
    k_end = (k * jnp.exp(b_end - b)).astype(BF16)
    decay = jnp.exp(b_end)
    ri = lax.broadcasted_iota(jnp.int32, (C, C), 0)
    ci = lax.broadcasted_iota(jnp.int32, (C, C), 1)
    causal = (ri <= ci) if reverse else (ri >= ci)
    heads = range(GLA_HEADS)
    ks = [slice(h * GLA_DK, (h + 1) * GLA_DK) for h in heads]
    vs = [slice(h * GLA_DV, (h + 1) * GLA_DV) for h in heads]
    att = [jnp.where(causal, _dot_nt(q_in[:, ks[h]], k_in[:, ks[h]]), 0.0).astype(BF16) for h in heads]
    st = [st_ref[h] for h in heads]
    inter = [_dot_nt(q_dec[:, ks[h]], st[h].astype(BF16)) for h in heads]
    o = [_dot(att[h], v_ref[:, vs[h]]) + inter[h] for h in heads]
    for h in heads:
        st_ref[h] = st[h] * decay[:, ks[h]] + _dot_tn(v_ref[:, vs[h]], k_end[:, ks[h]])
    if final:
        tot = [o[h] + of_ref[:, vs[h]] for h in heads]
        inv = [lax.rsqrt(jnp.mean(t * t, axis=-1, keepdims=True) + EPS) for t in tot]
        for h in heads:
            y = tot[h] * inv[h] * nw_ref[...]
            o_ref[:, vs[h]] = (y * jax.nn.silu(gr_ref[:, vs[h]].astype(F32))).astype(BF16)
    else:
        for h in heads:
            o_ref[:, vs[h]] = o[h]


def _gla_pass(reverse, pqk, pv, plr, up, bias, tri, extra=None):
    T = pqk.shape[0]
    C = GLA_CHUNK
    n = T // C
    final = extra is not None
    blk = (lambda i: n - 1 - i) if reverse else (lambda i: i)
    row = lambda w, col=0: pl.BlockSpec((C, w), lambda i: (blk(i), col))
    in_specs = [row(GLA_QK_WIDTH, 0), row(GLA_QK_WIDTH, 1), row(GLA_WIDTH), row(2 * GLA_RANK),
                _resident(up.shape), _resident(bias.shape), _resident(tri.shape)]
    args = [pqk, pqk, pv, plr, up, bias, tri]
    if final:
        o_f, pr, norm_w = extra
        in_specs += [row(GLA_WIDTH), row(GLA_WIDTH), _resident(norm_w.shape)]
        args += [o_f, pr, norm_w]
    return pl.pallas_call(
        functools.partial(_gla_body, reverse, final),
        grid=(n,),
        in_specs=in_specs,
        out_specs=row(GLA_WIDTH),
        out_shape=jax.ShapeDtypeStruct((T, GLA_WIDTH), BF16 if final else F32),
        scratch_shapes=[pltpu.VMEM((GLA_HEADS, GLA_DV, GLA_DK), F32)],
        compiler_params=pltpu.CompilerParams(dimension_semantics=("arbitrary",),
                                             vmem_limit_bytes=VMEM_LIMIT),
        name="gla_bwd" if reverse else "gla_fwd",
    )(*args)


def _outproj_body(a_ref, g_ref, x_ref, wa_ref, wg_ref, ln_ref, wr_hl_ref, wr_hi_ref, rb_ref, tri_ref,
                  h_ref, xn_ref, ri_ref, rw_ref, cnt_ref):
    tm = x_ref.shape[0]

    @pl.when(pl.program_id(0) == 0)
    def _():
        cnt_ref[...] = jnp.zeros_like(cnt_ref)

    h = x_ref[...] + _dot(a_ref[...], wa_ref[...]) + _dot(g_ref[...], wg_ref[...])
    h_ref[...] = h
    xn = h * lax.rsqrt(jnp.mean(h * h, axis=-1, keepdims=True) + EPS) * ln_ref[...]
    xn_ref[...] = xn

    x_hi = xn.astype(BF16)
    x_lo = (xn - x_hi.astype(F32)).astype(BF16)
    l2 = _dot(x_hi, wr_hl_ref[...])
    logits = l2[:, :ROUTER_ROWS] + l2[:, ROUTER_ROWS:] + _dot(x_lo, wr_hi_ref[...])
    lt = jnp.transpose(logits) + rb_ref[...]

    gl = lt[0:N_GROUPS]
    gmax = jnp.max(gl, axis=0, keepdims=True)
    gi = lax.broadcasted_iota(jnp.int32, gl.shape, 0).astype(F32)
    g_sel = jnp.min(jnp.where(gl == gmax, gi, float(N_GROUPS)), axis=0, keepdims=True)
    g_gate = 1.0 / jnp.sum(jnp.exp(gl - gmax), axis=0, keepdims=True)

    el = lt[EXPERT_ROW0:EXPERT_ROW0 + N_EXPERTS]
    ei_int = lax.broadcasted_iota(jnp.int32, el.shape, 0)
    ei = ei_int.astype(F32)
    grp = (ei_int >> 3).astype(F32)
    cand = jnp.where(grp == g_sel, el, -jnp.inf)
    v1 = jnp.max(cand, axis=0, keepdims=True)
    e1 = jnp.min(jnp.where(cand == v1, ei, float(N_EXPERTS)), axis=0, keepdims=True)
    cand2 = jnp.where(ei == e1, -jnp.inf, cand)
    v2 = jnp.max(cand2, axis=0, keepdims=True)
    e2 = jnp.min(jnp.where(cand2 == v2, ei, float(N_EXPERTS)), axis=0, keepdims=True)
    d = jnp.exp(v2 - v1)
    w1 = g_gate / (1.0 + d)
    w2 = g_gate * d / (1.0 + d)

    oh1 = (ei == e1).astype(F32)
    oh2 = (ei == e2).astype(F32)
    cnt = oh1 + oh2
    before = _dot(cnt.astype(BF16), tri_ref[...]) + cnt_ref[:, 0:1]
    r1 = jnp.sum(oh1 * before, axis=0, keepdims=True)
    r2 = jnp.sum(oh2 * before, axis=0, keepdims=True)
    cnt_ref[...] = cnt_ref[...] + jnp.sum(cnt, axis=1, keepdims=True)

    ri_ref[...] = jnp.concatenate([e1, e2, r1, r2, jnp.zeros((4, tm), F32)], axis=0).astype(jnp.int32)
    rw_ref[...] = jnp.concatenate([w1, w2, jnp.zeros((6, tm), F32)], axis=0)


def _outproj_router(attn, gla, x2, w_attn, w_gla, ln_w, wr_hl, wr_hi, rbias, tri):
    T = x2.shape[0]
    tm = OUT_ROWS
    row = lambda w: pl.BlockSpec((tm, w), lambda i: (i, 0))
    col = lambda r: pl.BlockSpec((r, tm), lambda i: (0, i))
    return pl.pallas_call(
        _outproj_body,
        grid=(T // tm,),
        in_specs=[row(ATTN_WIDTH), row(GLA_WIDTH), row(D_MODEL),
                  _resident(w_attn.shape), _resident(w_gla.shape), _resident(ln_w.shape),
                  _resident(wr_hl.shape), _resident(wr_hi.shape), _resident(rbias.shape),
                  _resident(tri.shape)],
        out_specs=[row(D_MODEL), row(D_MODEL), col(8), col(8),
                   pl.BlockSpec((N_EXPERTS, 128), lambda i: (0, 0))],
        out_shape=[jax.ShapeDtypeStruct((T, D_MODEL), F32),
                   jax.ShapeDtypeStruct((T, D_MODEL), F32),
                   jax.ShapeDtypeStruct((8, T), jnp.int32),
                   jax.ShapeDtypeStruct((8, T), F32),
                   jax.ShapeDtypeStruct((N_EXPERTS, 128), F32)],
        compiler_params=pltpu.CompilerParams(dimension_semantics=("arbitrary",),
                                             vmem_limit_bytes=VMEM_LIMIT),
        name="outproj_router",
    )(attn, gla, x2, w_attn, w_gla, ln_w, wr_hl, wr_hi, rbias, tri)


def _row_copy(src, s, dst, d, sem):
    return pltpu.make_async_copy(src.at[pl.ds(s, 1)], dst.at[pl.ds(d, 1)], sem)


def _moe_body(be_ref, nu_ref, slot_ref, lo_ref, hi_ref, xn_ref, wg_ref, wu_ref, wd_ref, y_ref,
              xbuf, xb_ref, tok_ref, sem):
    bm = MOE_ROWS
    b = pl.program_id(0)
    n_used = nu_ref[0]
    cur = b % MOE_ROW_BUFFERS

    def gather_start(block, buf, unrolled):
        def one(i):
            _row_copy(xn_ref, tok_ref[block * bm + i], xbuf.at[buf], i, sem.at[buf]).start()
        if unrolled:
            for i in range(bm):
                one(i)
        else:
            lax.fori_loop(0, bm, lambda i, c: (one(i), c)[1], 0)

    def gather_wait(buf):
        pltpu.make_async_copy(xn_ref.at[pl.ds(0, bm)], xbuf.at[buf], sem.at[buf]).wait()

    @pl.when(b == 0)
    def _():
        def per_expert(e, c):
            def zero(s, c2):
                tok_ref[s] = 0
                return c2
            return lax.fori_loop(lo_ref[e], hi_ref[e], zero, c)
        lax.fori_loop(0, N_EXPERTS, per_expert, 0)

        def fill(t, c):
            for k in range(TOP_K):
                tok_ref[slot_ref[TOP_K * t + k]] = t
            return c
        lax.fori_loop(0, slot_ref.shape[0] // TOP_K, fill, 0, unroll=8)
        gather_start(0, 0, False)

        @pl.when(n_used > 1)
        def _():
            gather_start(1, 1, False)

    def stage_rows():
        gather_wait(cur)
        xb_ref[...] = xbuf[cur].astype(BF16)

    def experts():
        x = xb_ref[...]
        hid = jax.nn.silu(_dot(x, wg_ref[0])) * _dot(x, wu_ref[0])
        y_ref[...] = _dot(hid.astype(BF16), wd_ref[0])

    @pl.when(b + 2 < n_used)
    def _():
        stage_rows()
        gather_start(b + 2, (b + 2) % MOE_ROW_BUFFERS, True)
        experts()

    @pl.when((b < n_used) & (b + 2 >= n_used))
    def _():
        stage_rows()
        experts()

    @pl.when(b >= n_used)
    def _():
        y_ref[...] = jnp.zeros_like(y_ref)


def _moe(block_e, n_used, slots, pad_lo, pad_hi, xn, wg, wu, wd, n_slots):
    bm = MOE_ROWS
    wspec = lambda shape: pl.BlockSpec((1,) + shape, lambda b, be, *_: (be[b], 0, 0))
    grid_spec = pltpu.PrefetchScalarGridSpec(
        num_scalar_prefetch=5,
        grid=(n_slots // bm,),
        in_specs=[pl.BlockSpec(memory_space=pl.ANY),
                  wspec((D_MODEL, D_FF)), wspec((D_MODEL, D_FF)), wspec((D_FF, D_MODEL))],
        out_specs=pl.BlockSpec((bm, D_MODEL), lambda b, *_: (b, 0)),
        scratch_shapes=[pltpu.VMEM((MOE_ROW_BUFFERS, bm, D_MODEL), F32), pltpu.VMEM((bm, D_MODEL), BF16),
                        pltpu.SMEM((n_slots,), jnp.int32),
                        pltpu.SemaphoreType.DMA((MOE_ROW_BUFFERS,))],
    )
    return pl.pallas_call(
        _moe_body,
        grid_spec=grid_spec,
        out_shape=jax.ShapeDtypeStruct((n_slots, D_MODEL), F32),
        compiler_params=pltpu.CompilerParams(dimension_semantics=("arbitrary",),
                                             vmem_limit_bytes=VMEM_LIMIT),
        name="moe",
    )(block_e, n_used, slots, pad_lo, pad_hi, xn, wg, wu, wd)


def _combine_body(slot_ref, h_ref, w_ref, ys_ref, o_ref, g_ref, sem):
    tb = COMBINE_ROWS
    i = pl.program_id(0)
    cur = i % 2

    def gather_start(block, buf):
        def one(t, c):
            for k in range(TOP_K):
                _row_copy(ys_ref, slot_ref[TOP_K * (block * tb + t) + k], g_ref.at[buf, k], t,
                          sem.at[buf]).start()
            return c
        lax.fori_loop(0, tb, one, 0, unroll=8)

    @pl.when(i == 0)
    def _():
        gather_start(0, 0)

    @pl.when(i + 1 < pl.num_programs(0))
    def _():
        gather_start(i + 1, 1 - cur)

    for k in range(TOP_K):
        pltpu.make_async_copy(ys_ref.at[pl.ds(0, tb)], g_ref.at[cur, k], sem.at[cur]).wait()
    o_ref[...] = h_ref[...] + w_ref[:, 0:1] * g_ref[cur, 0] + w_ref[:, 1:2] * g_ref[cur, 1]


def _combine(slots, h, wcol, ys):
    T = h.shape[0]
    tb = COMBINE_ROWS
    grid_spec = pltpu.PrefetchScalarGridSpec(
        num_scalar_prefetch=1,
        grid=(T // tb,),
        in_specs=[pl.BlockSpec((tb, D_MODEL), lambda i, s: (i, 0)),
                  pl.BlockSpec((tb, TOP_K), lambda i, s: (i, 0)),
                  pl.BlockSpec(memory_space=pl.ANY)],
        out_specs=pl.BlockSpec((tb, D_MODEL), lambda i, s: (i, 0)),
        scratch_shapes=[pltpu.VMEM((2, TOP_K, tb, D_MODEL), F32), pltpu.SemaphoreType.DMA((2,))],
    )
    return pl.pallas_call(
        _combine_body,
        grid_spec=grid_spec,
        out_shape=jax.ShapeDtypeStruct((T, D_MODEL), F32),
        compiler_params=pltpu.CompilerParams(dimension_semantics=("arbitrary",),
                                             vmem_limit_bytes=VMEM_LIMIT),
        name="combine",
    )(slots, h, wcol, ys)


def _rope_tables(T):
    inv_freq = jnp.power(jnp.float32(ROPE_THETA),
                         -jnp.arange(ROPE_HALF, dtype=F32) * (2.0 / ROPE_DIM))
    ang = jnp.arange(T).astype(F32)[:, None] * inv_freq[None, :]
    cos, sin = jnp.cos(ang), jnp.sin(ang)
    zeros = jnp.zeros((T, HEAD_DIM - ROPE_DIM), F32)
    zh = jnp.zeros((T, ROPE_HALF), F32)
    cos_t = jnp.concatenate([cos, cos, jnp.ones((T, HEAD_DIM - ROPE_DIM), F32)], axis=1)
    sa_t = jnp.concatenate([zh, sin, zeros], axis=1)
    sb_t = jnp.concatenate([-sin, zh, zeros], axis=1)
    return cos_t, sa_t, sb_t


def _split_hi_lo(w):
    hi = w.astype(BF16)
    lo = (w - hi.astype(F32)).astype(BF16)
    return hi, lo


def _layer(x2, ln1_w, w_in, q_norm_w, k_norm_w, attn_sink, attn_out_norm_w, gate_up_f, gate_bias_f,
           gate_up_b, gate_bias_b, gla_out_norm_w, w_out, ln2_w, w_group, b_group, w_router, b_router,
           w_gate_e, w_up_e, w_down_e):
    T = x2.shape[0]
    row = lambda v: v.reshape(1, -1).astype(F32)

    w_in_b = w_in.astype(BF16)
    c0, c1, c2, c3 = A_WIDTH, A_WIDTH + 2 * GLA_QK_WIDTH, A_WIDTH + 2 * GLA_QK_WIDTH + GLA_WIDTH, \
        A_WIDTH + 2 * GLA_QK_WIDTH + 2 * GLA_WIDTH
    wa, wqk, wv, wr, wl = (w_in_b[:, :c0], w_in_b[:, c0:c1], w_in_b[:, c1:c2], w_in_b[:, c2:c3],
                           w_in_b[:, c3:])
    cos_t, sa_t, sb_t = _rope_tables(T)
    pa, pqk, pv, pr, plr = _inproj(x2, row(ln1_w), wa, wqk, wv, wr, wl, row(q_norm_w), row(k_norm_w),
                                   cos_t, sa_t, sb_t)

    attn = _attention(pa, attn_sink.astype(F32), row(attn_out_norm_w))

    C = GLA_CHUNK
    ones = jnp.ones((C, C), F32)
    zr = jnp.zeros((GLA_RANK, GLA_QK_WIDTH), F32)
    up_f = jnp.concatenate([gate_up_f.astype(F32), zr], axis=0)
    up_b = jnp.concatenate([zr, gate_up_b.astype(F32)], axis=0)

    def up_pieces(up):
        hi, lo = _split_hi_lo(up)
        return jnp.concatenate([hi, hi, lo], axis=0)

    o_f = _gla_pass(False, pqk, pv, plr, up_pieces(up_f), row(gate_bias_f), jnp.tril(ones).astype(BF16))
    gla = _gla_pass(True, pqk, pv, plr, up_pieces(up_b), row(gate_bias_b), jnp.triu(ones).astype(BF16),
                    extra=(o_f, pr, row(gla_out_norm_w)))

    w_out_b = w_out.astype(BF16)
    wr_full = jnp.zeros((D_MODEL, ROUTER_ROWS), F32)
    wr_full = wr_full.at[:, :N_GROUPS].set(w_group.astype(F32))
    wr_full = wr_full.at[:, EXPERT_ROW0:EXPERT_ROW0 + N_EXPERTS].set(w_router.astype(F32))
    wr_hi, wr_lo = _split_hi_lo(wr_full)
    rbias = jnp.zeros((ROUTER_ROWS, 1), F32)
    rbias = rbias.at[:N_GROUPS, 0].set(b_group.astype(F32))
    rbias = rbias.at[EXPERT_ROW0:EXPERT_ROW0 + N_EXPERTS, 0].set(b_router.astype(F32))
    tm = OUT_ROWS
    earlier = jnp.triu(jnp.ones((tm, tm), F32), k=1).astype(BF16)
    h, xn, r_int, r_w, counts = _outproj_router(
        attn, gla, x2, w_out_b[:ATTN_WIDTH], w_out_b[ATTN_WIDTH:], row(ln2_w),
        jnp.concatenate([wr_hi, wr_lo], axis=1), wr_hi, rbias, earlier)

    bm = MOE_ROWS
    n_blocks = (T * TOP_K) // bm + N_EXPERTS
    cnt = counts[:, 0].astype(jnp.int32)
    padded = (cnt + bm - 1) // bm * bm
    pad_end = jnp.cumsum(padded)
    pad_start = pad_end - padded
    n_used = (pad_end[-1] // bm).astype(jnp.int32).reshape(1)
    block_e = jnp.minimum(jnp.sum(pad_end[None, :] <= (jnp.arange(n_blocks) * bm)[:, None], axis=1),
                          N_EXPERTS - 1).astype(jnp.int32)
    e_sel = r_int[0:TOP_K].T
    start_sel = jnp.sum(jnp.where(e_sel[..., None] == jnp.arange(N_EXPERTS), pad_start, 0), axis=-1)
    slots = (start_sel + r_int[TOP_K:2 * TOP_K].T).reshape(-1).astype(jnp.int32)

    ys = _moe(block_e, n_used, slots, (pad_start + cnt).astype(jnp.int32), pad_end.astype(jnp.int32), xn,
              w_gate_e.astype(BF16), w_up_e.astype(BF16), w_down_e.astype(BF16), n_blocks * bm)
    return _combine(slots, h, r_w[0:TOP_K].T, ys)


def kernel(x, ln1_w, w_in, q_norm_w, k_norm_w, attn_sink, attn_out_norm_w, gla_gate_up_f, gla_gate_bias_f,
           gla_gate_up_b, gla_gate_bias_b, gla_out_norm_w, w_out, ln2_w, w_group, b_group, w_router,
           b_router, w_gate_e, w_up_e, w_down_e):
    B, S, D = x.shape
    h = x.reshape(B * S, D)
    assert B == 1
    for l in range(ln1_w.shape[0]):
        h = _layer(h, ln1_w[l], w_in[l], q_norm_w[l], k_norm_w[l], attn_sink[l], attn_out_norm_w[l],
                   gla_gate_up_f[l], gla_gate_bias_f[l], gla_gate_up_b[l], gla_gate_bias_b[l],
                   gla_out_norm_w[l], w_out[l], ln2_w[l], w_group[l], b_group[l], w_router[l],
                   b_router[l], w_gate_e[l], w_up_e[l], w_down_e[l])
    return h.reshape(B, S, D)
```

```python
import functools

import jax
import jax.numpy as jnp
from jax import lax
from jax.experimental import pallas as pl
from jax.experimental.pallas import tpu as pltpu

F32 = jnp.float32
BF16 = jnp.bfloat16

EPS = 1e-6
D_MODEL = 2048

ATTN_Q_HEADS = 8
ATTN_KV_HEADS = 2
ATTN_GROUP = ATTN_Q_HEADS // ATTN_KV_HEADS
HEAD_DIM = 128
WINDOW = 128
ATTN_BLOCK = 128
ATTN_STEP_BLOCKS = 4
ROPE_THETA = 500000.0
ROPE_DIM = HEAD_DIM // 4
ROPE_HALF = ROPE_DIM // 2
ATTN_WIDTH = ATTN_Q_HEADS * HEAD_DIM
KV_WIDTH = ATTN_KV_HEADS * HEAD_DIM
A_WIDTH = ATTN_WIDTH + 2 * KV_WIDTH

GLA_HEADS = 4
GLA_DK = 128
GLA_DV = 256
GLA_RANK = 16
GLA_TAU = 16.0
GLA_QK_WIDTH = GLA_HEADS * GLA_DK
GLA_WIDTH = GLA_HEADS * GLA_DV
GLA_CHUNK = 128

N_GROUPS = 4
EXPERTS_PER_GROUP = 8
N_EXPERTS = N_GROUPS * EXPERTS_PER_GROUP
TOP_K = 2
D_FF = 1024
ROUTER_ROWS = 128
EXPERT_ROW0 = 8

PROJ_ROWS = 512
OUT_ROWS = 256
MOE_ROWS = 256
MOE_ROW_BUFFERS = 3
COMBINE_ROWS = 256

VMEM_LIMIT = 56 * 1024 * 1024


def _dot(a, b):
    return jnp.dot(a, b, preferred_element_type=F32)


def _dot_nt(a, b):
    return lax.dot_general(a, b, (((1,), (1,)), ((), ())), preferred_element_type=F32)


def _dot_tn(a, b):
    return lax.dot_general(a, b, (((0,), (0,)), ((), ())), preferred_element_type=F32)


def _pack_bf16_pairs(x):
    n = x.shape[1] // 2
    lo = lax.bitcast_convert_type(x[:, :n].astype(F32), jnp.uint32) >> 16
    hi = lax.bitcast_convert_type(x[:, n:].astype(F32), jnp.uint32) & jnp.uint32(0xFFFF0000)
    return lo | hi


def _unpack_bf16_pairs(w):
    lo = lax.bitcast_convert_type(w << 16, F32).astype(BF16)
    hi = lax.bitcast_convert_type(w & jnp.uint32(0xFFFF0000), F32).astype(BF16)
    return lo, hi


def _resident(shape):
    nd = len(shape)
    return pl.BlockSpec(shape, lambda *_: (0,) * nd, pipeline_mode=pl.Buffered(1))


def _inproj_body(x_ref, ln_ref, wa_ref, wqk_ref, wv_ref, wr_ref, wl_ref, qn_ref, kn_ref,
                 cos_ref, sa_ref, sb_ref, oa_ref, oqk_ref, ov_ref, or_ref, ol_ref):
    x = x_ref[...]
    ms = jnp.mean(x * x, axis=-1, keepdims=True)
    xn = (x * lax.rsqrt(ms + EPS) * ln_ref[...]).astype(BF16)
    oqk_ref[...] = _dot(xn, wqk_ref[...]).astype(BF16)
    ov_ref[...] = _dot(xn, wv_ref[...]).astype(BF16)
    or_ref[...] = _dot(xn, wr_ref[...]).astype(BF16)
    ol_ref[...] = _dot(xn, wl_ref[...])
    acc = _dot(xn, wa_ref[...])
    cos, sa, sb = cos_ref[...], sa_ref[...], sb_ref[...]
    for c in range(ATTN_Q_HEADS + ATTN_KV_HEADS):
        xh = acc[:, c * HEAD_DIM:(c + 1) * HEAD_DIM]
        w = qn_ref[...] if c < ATTN_Q_HEADS else kn_ref[...]
        y = xh * lax.rsqrt(jnp.mean(xh * xh, axis=-1, keepdims=True) + EPS) * w
        y = (y * cos + pltpu.roll(y, ROPE_HALF, 1) * sa
             + pltpu.roll(y, HEAD_DIM - ROPE_HALF, 1) * sb)
        if c < ATTN_Q_HEADS:
            y = y * (HEAD_DIM ** -0.5)
        oa_ref[:, c * HEAD_DIM:(c + 1) * HEAD_DIM] = y.astype(BF16)
    oa_ref[:, ATTN_WIDTH + KV_WIDTH:] = acc[:, ATTN_WIDTH + KV_WIDTH:].astype(BF16)


def _inproj(x2, ln_w, wa, wqk, wv, wr, wl, qn, kn, cos_t, sa_t, sb_t):
    T = x2.shape[0]
    tm = PROJ_ROWS
    row = lambda w: pl.BlockSpec((tm, w), lambda i: (i, 0))
    return pl.pallas_call(
        _inproj_body,
        grid=(T // tm,),
        in_specs=[row(D_MODEL), _resident((1, D_MODEL)),
                  _resident(wa.shape), _resident(wqk.shape), _resident(wv.shape),
                  _resident(wr.shape), _resident(wl.shape),
                  _resident((1, HEAD_DIM)), _resident((1, HEAD_DIM)),
                  row(HEAD_DIM), row(HEAD_DIM), row(HEAD_DIM)],
        out_specs=[row(A_WIDTH), row(2 * GLA_QK_WIDTH), row(GLA_WIDTH), row(GLA_WIDTH),
                   row(2 * GLA_RANK)],
        out_shape=[jax.ShapeDtypeStruct((T, A_WIDTH), BF16),
                   jax.ShapeDtypeStruct((T, 2 * GLA_QK_WIDTH), BF16),
                   jax.ShapeDtypeStruct((T, GLA_WIDTH), BF16),
                   jax.ShapeDtypeStruct((T, GLA_WIDTH), BF16),
                   jax.ShapeDtypeStruct((T, 2 * GLA_RANK), F32)],
        compiler_params=pltpu.CompilerParams(dimension_semantics=("parallel",),
                                             vmem_limit_bytes=VMEM_LIMIT),
        name="inproj",
    )(x2, ln_w, wa, wqk, wv, wr, wl, qn, kn, cos_t, sa_t, sb_t)


def _attn_body(sink_ref, q_ref, kp_ref, kc_ref, kn_ref, vp_ref, vc_ref, vn_ref, nw_ref, o_ref):
    n = pl.program_id(0)
    nb = pl.num_programs(0)
    rows = ATTN_GROUP * ATTN_BLOCK
    keys = 3 * ATTN_BLOCK
    r = lax.broadcasted_iota(jnp.int32, (rows, keys), 0) & (ATTN_BLOCK - 1)
    c = lax.broadcasted_iota(jnp.int32, (rows, keys), 1)
    band = (c >= r + (ATTN_BLOCK - WINDOW)) & (c <= r + (ATTN_BLOCK + WINDOW))
    edge_mask = [band & ((c >= ATTN_BLOCK) | (n > 0))] + [band] * (ATTN_STEP_BLOCKS - 2) + \
                [band & ((c < 2 * ATTN_BLOCK) | (n < nb - 1))]
    def window(j, prev_ref, cur_ref, next_ref, ks):
        blocks = [prev_ref[:, ks]] + [cur_ref[i * ATTN_BLOCK:(i + 1) * ATTN_BLOCK, ks]
                                      for i in range(ATTN_STEP_BLOCKS)] + [next_ref[:, ks]]
        return jnp.concatenate(blocks[j:j + 3], axis=0)

    chains = [(j, g) for j in range(ATTN_STEP_BLOCKS) for g in range(ATTN_KV_HEADS)]
    heads_of = lambda g: range(g * ATTN_GROUP, (g + 1) * ATTN_GROUP)
    ks_of = lambda g: slice(g * HEAD_DIM, (g + 1) * HEAD_DIM)
    qrows_of = lambda j: slice(j * ATTN_BLOCK, (j + 1) * ATTN_BLOCK)
    s = [jnp.where(edge_mask[j],
                   _dot_nt(jnp.concatenate([q_ref[qrows_of(j), h * HEAD_DIM:(h + 1) * HEAD_DIM]
                                            for h in heads_of(g)], axis=0),
                           window(j, kp_ref, kc_ref, kn_ref, ks_of(g))), -jnp.inf)
         for j, g in chains]
    sink = [jnp.concatenate([jnp.full((ATTN_BLOCK, 1), sink_ref[h], F32) for h in heads_of(g)], axis=0)
            for j, g in chains]
    m = [jnp.maximum(jnp.max(si, axis=-1, keepdims=True), sk) for si, sk in zip(s, sink)]
    e = [jnp.exp(si - mi) for si, mi in zip(s, m)]
    denom = [jnp.sum(ei, axis=-1, keepdims=True) + jnp.exp(sk - mi) for ei, sk, mi in zip(e, sink, m)]
    o = [_dot(ei.astype(BF16), window(j, vp_ref, vc_ref, vn_ref, ks_of(g))) / di
         for (j, g), ei, di in zip(chains, e, denom)]
    for j in range(ATTN_STEP_BLOCKS):
        outs = [o[chains.index((j, g))][i * ATTN_BLOCK:(i + 1) * ATTN_BLOCK]
                for g in range(ATTN_KV_HEADS) for i in range(ATTN_GROUP)]
        a = jnp.concatenate(outs, axis=1)
        a = a * lax.rsqrt(jnp.mean(a * a, axis=-1, keepdims=True) + EPS) * nw_ref[...]
        o_ref[qrows_of(j), :] = a.astype(BF16)


def _attention(pa, sink, norm_w):
    T = pa.shape[0]
    sb = ATTN_STEP_BLOCKS
    nb = T // ATTN_BLOCK
    kcol = ATTN_WIDTH // KV_WIDTH
    vcol = kcol + 1
    prev = lambda col: pl.BlockSpec((ATTN_BLOCK, KV_WIDTH), lambda n, s: (jnp.maximum(sb * n - 1, 0), col))
    nxt = lambda col: pl.BlockSpec((ATTN_BLOCK, KV_WIDTH),
                                   lambda n, s: (jnp.minimum(sb * n + sb, nb - 1), col))
    cur = lambda col: pl.BlockSpec((sb * ATTN_BLOCK, KV_WIDTH), lambda n, s: (n, col))
    grid_spec = pltpu.PrefetchScalarGridSpec(
        num_scalar_prefetch=1,
        grid=(nb // sb,),
        in_specs=[pl.BlockSpec((sb * ATTN_BLOCK, ATTN_WIDTH), lambda n, s: (n, 0)),
                  prev(kcol), cur(kcol), nxt(kcol), prev(vcol), cur(vcol), nxt(vcol),
                  pl.BlockSpec((1, ATTN_WIDTH), lambda n, s: (0, 0))],
        out_specs=pl.BlockSpec((sb * ATTN_BLOCK, ATTN_WIDTH), lambda n, s: (n, 0)),
    )
    return pl.pallas_call(
        _attn_body,
        grid_spec=grid_spec,
        out_shape=jax.ShapeDtypeStruct((T, ATTN_WIDTH), BF16),
        compiler_params=pltpu.CompilerParams(dimension_semantics=("parallel",),
                                             vmem_limit_bytes=VMEM_LIMIT),
        name="attention",
    )(sink, pa, pa, pa, pa, pa, pa, pa, norm_w)


def _split3(x):
    hi = x.astype(BF16)
    r1 = x - hi.astype(F32)
    mid = r1.astype(BF16)
    lo = (r1 - mid.astype(F32)).astype(BF16)
    return hi, mid, lo


def _gla_body(reverse, final, *refs):
    if final:
        (q_ref, k_ref, v_ref, lr_ref, up_ref, bias_ref, tri_ref, of_ref, gr_ref, nw_ref, wcast_ref,
         o_ref, wcast_out_ref, st_ref) = refs
    else:
        (q_ref, k_ref, v_ref, lr_ref, up_ref, bias_ref, tri_ref, wcast_ref,
         o_ref, wcast_out_ref, st_ref) = refs
    C = GLA_CHUNK
    wcast_out_ref[...] = wcast_ref[...].astype(BF16)

    @pl.when(pl.program_id(0) == 0)
    def _():
        st_ref[...] = jnp.zeros_like(st_ref)

    lr = lr_ref[...]
    lr_hi = lr.astype(BF16)
    lr_lo = (lr - lr_hi.astype(F32)).astype(BF16)
    g = _dot(jnp.concatenate([lr_hi, lr_lo, lr_hi], axis=1), up_ref[...]) + bias_ref[...]
    la = jax.nn.log_sigmoid(g) * (1.0 / GLA_TAU)
    tri = tri_ref[...]
    hi, mid, lo = _split3(la)
    b = _dot(tri, hi) + _dot(tri, mid) + _dot(tri, lo)
    end = 0 if reverse else C - 1
    b_end = b[end:end + 1]
    b_mid = b[C // 2:C // 2 + 1]
    scale = GLA_DK ** -0.5
    q = q_ref[...].astype(F32) * scale
    k = k_ref[...].astype(F32)
    q_in = (q * jnp.exp(b - b_mid)).astype(BF16)
    k_in = (k * jnp.exp(b_mid - b)).astype(BF16)
    q_dec = (q * jnp.exp(b)).astype(BF16)
    k_end = (k * jnp.exp(b_end - b)).astype(BF16)
    decay = jnp.exp(b_end)
    ri = lax.broadcasted_iota(jnp.int32, (C, C), 0)
    ci = lax.broadcasted_iota(jnp.int32, (C, C), 1)
    causal = (ri <= ci) if reverse else (ri >= ci)
    heads = range(GLA_HEADS)
    ks = [slice(h * GLA_DK, (h + 1) * GLA_DK) for h in heads]
    vs = [slice(h * GLA_DV, (h + 1) * GLA_DV) for h in heads]
    att = [jnp.where(causal, _dot_nt(q_in[:, ks[h]], k_in[:, ks[h]]), 0.0).astype(BF16) for h in heads]
    st = [st_ref[h] for h in heads]
    inter = [_dot_nt(q_dec[:, ks[h]], st[h].astype(BF16)) for h in heads]
    o = [_dot(att[h], v_ref[:, vs[h]]) + inter[h] for h in heads]
    for h in heads:
        st_ref[h] = st[h] * decay[:, ks[h]] + _dot_tn(v_ref[:, vs[h]], k_end[:, ks[h]])
    if final:
        tot = [o[h] + of_ref[:, vs[h]] for h in heads]
        inv = [lax.rsqrt(jnp.mean(t * t, axis=-1, keepdims=True) + EPS) for t in tot]
        for h in heads:
            y = tot[h] * inv[h] * nw_ref[...]
            o_ref[:, vs[h]] = (y * jax.nn.silu(gr_ref[:, vs[h]].astype(F32))).astype(BF16)
    else:
        for h in heads:
            o_ref[:, vs[h]] = o[h]


def _cast_slab_spec(w2d, n_steps):
    rows = w2d.shape[0] // n_steps
    assert rows * n_steps == w2d.shape[0] and rows % 16 == 0
    return pl.BlockSpec((rows, w2d.shape[1]), lambda i, *_: (i, 0))


def _gla_pass(reverse, pqk, pv, plr, up, bias, tri, wcast, extra=None):
    T = pqk.shape[0]
    C = GLA_CHUNK
    n = T // C
    final = extra is not None
    blk = (lambda i: n - 1 - i) if reverse else (lambda i: i)
    row = lambda w, col=0: pl.BlockSpec((C, w), lambda i: (blk(i), col))
    in_specs = [row(GLA_QK_WIDTH, 0), row(GLA_QK_WIDTH, 1), row(GLA_WIDTH), row(2 * GLA_RANK),
                _resident(up.shape), _resident(bias.shape), _resident(tri.shape)]
    args = [pqk, pqk, pv, plr, up, bias, tri]
    if final:
        o_f, pr, norm_w = extra
        in_specs += [row(GLA_WIDTH), row(GLA_WIDTH), _resident(norm_w.shape)]
        args += [o_f, pr, norm_w]
    in_specs.append(_cast_slab_spec(wcast, n))
    args.append(wcast)
    return pl.pallas_call(
        functools.partial(_gla_body, reverse, final),
        grid=(n,),
        in_specs=in_specs,
        out_specs=[row(GLA_WIDTH), _cast_slab_spec(wcast, n)],
        out_shape=[jax.ShapeDtypeStruct((T, GLA_WIDTH), BF16 if final else F32),
                   jax.ShapeDtypeStruct(wcast.shape, BF16)],
        scratch_shapes=[pltpu.VMEM((GLA_HEADS, GLA_DV, GLA_DK), F32)],
        compiler_params=pltpu.CompilerParams(dimension_semantics=("arbitrary",),
                                             vmem_limit_bytes=VMEM_LIMIT),
        name="gla_bwd" if reverse else "gla_fwd",
    )(*args)


def _outproj_body(a_ref, g_ref, x_ref, wa_ref, wg_ref, ln_ref, wr_hl_ref, wr_hi_ref, rb_ref, tri_ref,
                  wcast_ref, h_ref, xn_ref, ri_ref, rw_ref, cnt_ref, wcast_out_ref):
    tm = x_ref.shape[0]
    wcast_out_ref[...] = wcast_ref[...].astype(BF16)

    @pl.when(pl.program_id(0) == 0)
    def _():
        cnt_ref[...] = jnp.zeros_like(cnt_ref)

    h = x_ref[...] + _dot(a_ref[...], wa_ref[...]) + _dot(g_ref[...], wg_ref[...])
    h_ref[...] = h
    xn = h * lax.rsqrt(jnp.mean(h * h, axis=-1, keepdims=True) + EPS) * ln_ref[...]
    x_hi = xn.astype(BF16)
    xn_ref[...] = _pack_bf16_pairs(x_hi)
    x_lo = (xn - x_hi.astype(F32)).astype(BF16)
    l2 = _dot(x_hi, wr_hl_ref[...])
    logits = l2[:, :ROUTER_ROWS] + l2[:, ROUTER_ROWS:] + _dot(x_lo, wr_hi_ref[...])
    lt = jnp.transpose(logits) + rb_ref[...]

    gl = lt[0:N_GROUPS]
    gmax = jnp.max(gl, axis=0, keepdims=True)
    gi = lax.broadcasted_iota(jnp.int32, gl.shape, 0).astype(F32)
    g_sel = jnp.min(jnp.where(gl == gmax, gi, float(N_GROUPS)), axis=0, keepdims=True)
    g_gate = 1.0 / jnp.sum(jnp.exp(gl - gmax), axis=0, keepdims=True)

    el = lt[EXPERT_ROW0:EXPERT_ROW0 + N_EXPERTS]
    ei_int = lax.broadcasted_iota(jnp.int32, el.shape, 0)
    ei = ei_int.astype(F32)
    grp = (ei_int >> 3).astype(F32)
    cand = jnp.where(grp == g_sel, el, -jnp.inf)
    v1 = jnp.max(cand, axis=0, keepdims=True)
    e1 = jnp.min(jnp.where(cand == v1, ei, float(N_EXPERTS)), axis=0, keepdims=True)
    cand2 = jnp.where(ei == e1, -jnp.inf, cand)
    v2 = jnp.max(cand2, axis=0, keepdims=True)
    e2 = jnp.min(jnp.where(cand2 == v2, ei, float(N_EXPERTS)), axis=0, keepdims=True)
    d = jnp.exp(v2 - v1)
    w1 = g_gate / (1.0 + d)
    w2 = g_gate * d / (1.0 + d)

    oh1 = (ei == e1).astype(F32)
    oh2 = (ei == e2).astype(F32)
    cnt = oh1 + oh2
    before = _dot(cnt.astype(BF16), tri_ref[...]) + cnt_ref[:, 0:1]
    r1 = jnp.sum(oh1 * before, axis=0, keepdims=True)
    r2 = jnp.sum(oh2 * before, axis=0, keepdims=True)
    cnt_ref[...] = cnt_ref[...] + jnp.sum(cnt, axis=1, keepdims=True)

    ri_ref[...] = jnp.concatenate([e1, e2, r1, r2, jnp.zeros((4, tm), F32)], axis=0).astype(jnp.int32)
    rw_ref[...] = jnp.concatenate([w1, w2, jnp.zeros((6, tm), F32)], axis=0)


def _outproj_router(attn, gla, x2, w_attn, w_gla, ln_w, wr_hl, wr_hi, rbias, tri, wcast):
    T = x2.shape[0]
    tm = OUT_ROWS
    n = T // tm
    row = lambda w: pl.BlockSpec((tm, w), lambda i: (i, 0))
    col = lambda r: pl.BlockSpec((r, tm), lambda i: (0, i))
    return pl.pallas_call(
        _outproj_body,
        grid=(T // tm,),
        in_specs=[row(ATTN_WIDTH), row(GLA_WIDTH), row(D_MODEL),
                  _resident(w_attn.shape), _resident(w_gla.shape), _resident(ln_w.shape),
                  _resident(wr_hl.shape), _resident(wr_hi.shape), _resident(rbias.shape),
                  _resident(tri.shape), _cast_slab_spec(wcast, n)],
        out_specs=[row(D_MODEL), row(D_MODEL // 2), col(8), col(8),
                   pl.BlockSpec((N_EXPERTS, 128), lambda i: (0, 0)), _cast_slab_spec(wcast, n)],
        out_shape=[jax.ShapeDtypeStruct((T, D_MODEL), F32),
                   jax.ShapeDtypeStruct((T, D_MODEL // 2), jnp.uint32),
                   jax.ShapeDtypeStruct((8, T), jnp.int32),
                   jax.ShapeDtypeStruct((8, T), F32),
                   jax.ShapeDtypeStruct((N_EXPERTS, 128), F32),
                   jax.ShapeDtypeStruct(wcast.shape, BF16)],
        compiler_params=pltpu.CompilerParams(dimension_semantics=("arbitrary",),
                                             vmem_limit_bytes=VMEM_LIMIT),
        name="outproj_router",
    )(attn, gla, x2, w_attn, w_gla, ln_w, wr_hl, wr_hi, rbias, tri, wcast)


def _row_copy(src, s, dst, d, sem):
    return pltpu.make_async_copy(src.at[pl.ds(s, 1)], dst.at[pl.ds(d, 1)], sem)


def _moe_body(be_ref, nu_ref, slot_ref, lo_ref, hi_ref, xn_ref, wg_ref, wu_ref, wd_ref, y_ref,
              xbuf, xb_ref, tok_ref, sem):
    bm = MOE_ROWS
    b = pl.program_id(0)
    n_used = nu_ref[0]
    cur = b % MOE_ROW_BUFFERS

    def gather_start(block, buf, unrolled):
        def one(i):
            _row_copy(xn_ref, tok_ref[block * bm + i], xbuf.at[buf], i, sem.at[buf]).start()
        if unrolled:
            for i in range(bm):
                one(i)
        else:
            lax.fori_loop(0, bm, lambda i, c: (one(i), c)[1], 0)

    def gather_wait(buf):
        pltpu.make_async_copy(xn_ref.at[pl.ds(0, bm)], xbuf.at[buf], sem.at[buf]).wait()

    @pl.when(b == 0)
    def _():
        def per_expert(e, c):
            def zero(s, c2):
                tok_ref[s] = 0
                return c2
            return lax.fori_loop(lo_ref[e], hi_ref[e], zero, c)
        lax.fori_loop(0, N_EXPERTS, per_expert, 0)

        def fill(t, c):
            for k in range(TOP_K):
                tok_ref[slot_ref[TOP_K * t + k]] = t
            return c
        lax.fori_loop(0, slot_ref.shape[0] // TOP_K, fill, 0, unroll=8)
        gather_start(0, 0, False)

        @pl.when(n_used > 1)
        def _():
            gather_start(1, 1, False)

    def stage_rows():
        gather_wait(cur)
        lo, hi = _unpack_bf16_pairs(xbuf[cur])
        xb_ref[:, :D_MODEL // 2] = lo
        xb_ref[:, D_MODEL // 2:] = hi

    def experts():
        x = xb_ref[...]
        hid = jax.nn.silu(_dot(x, wg_ref[0])) * _dot(x, wu_ref[0])
        y_ref[...] = _dot(hid.astype(BF16), wd_ref[0])

    @pl.when(b + 2 < n_used)
    def _():
        stage_rows()
        gather_start(b + 2, (b + 2) % MOE_ROW_BUFFERS, True)
        experts()

    @pl.when((b < n_used) & (b + 2 >= n_used))
    def _():
        stage_rows()
        experts()

    @pl.when(b >= n_used)
    def _():
        y_ref[...] = jnp.zeros_like(y_ref)


def _moe(block_e, n_used, slots, pad_lo, pad_hi, xn, wg, wu, wd, n_slots):
    bm = MOE_ROWS
    wspec = lambda shape: pl.BlockSpec((1,) + shape, lambda b, be, *_: (be[b], 0, 0))
    grid_spec = pltpu.PrefetchScalarGridSpec(
        num_scalar_prefetch=5,
        grid=(n_slots // bm,),
        in_specs=[pl.BlockSpec(memory_space=pl.ANY),
                  wspec((D_MODEL, D_FF)), wspec((D_MODEL, D_FF)), wspec((D_FF, D_MODEL))],
        out_specs=pl.BlockSpec((bm, D_MODEL), lambda b, *_: (b, 0)),
        scratch_shapes=[pltpu.VMEM((MOE_ROW_BUFFERS, bm, D_MODEL // 2), jnp.uint32),
                        pltpu.VMEM((bm, D_MODEL), BF16),
                        pltpu.SMEM((n_slots,), jnp.int32),
                        pltpu.SemaphoreType.DMA((MOE_ROW_BUFFERS,))],
    )
    return pl.pallas_call(
        _moe_body,
        grid_spec=grid_spec,
        out_shape=jax.ShapeDtypeStruct((n_slots, D_MODEL), F32),
        compiler_params=pltpu.CompilerParams(dimension_semantics=("arbitrary",),
                                             vmem_limit_bytes=VMEM_LIMIT),
        name="moe",
    )(block_e, n_used, slots, pad_lo, pad_hi, xn, wg, wu, wd)


def _combine_body(slot_ref, h_ref, w_ref, ys_ref, o_ref, g_ref, sem):
    tb = COMBINE_ROWS
    i = pl.program_id(0)
    cur = i % 2

    def gather_start(block, buf):
        def one(t, c):
            for k in range(TOP_K):
                _row_copy(ys_ref, slot_ref[TOP_K * (block * tb + t) + k], g_ref.at[buf, k], t,
                          sem.at[buf]).start()
            return c
        lax.fori_loop(0, tb, one, 0, unroll=8)

    @pl.when(i == 0)
    def _():
        gather_start(0, 0)

    @pl.when(i + 1 < pl.num_programs(0))
    def _():
        gather_start(i + 1, 1 - cur)

    for k in range(TOP_K):
        pltpu.make_async_copy(ys_ref.at[pl.ds(0, tb)], g_ref.at[cur, k], sem.at[cur]).wait()
    o_ref[...] = h_ref[...] + w_ref[:, 0:1] * g_ref[cur, 0] + w_ref[:, 1:2] * g_ref[cur, 1]


def _combine(slots, h, wcol, ys):
    T = h.shape[0]
    tb = COMBINE_ROWS
    grid_spec = pltpu.PrefetchScalarGridSpec(
        num_scalar_prefetch=1,
        grid=(T // tb,),
        in_specs=[pl.BlockSpec((tb, D_MODEL), lambda i, s: (i, 0)),
                  pl.BlockSpec((tb, TOP_K), lambda i, s: (i, 0)),
                  pl.BlockSpec(memory_space=pl.ANY)],
        out_specs=pl.BlockSpec((tb, D_MODEL), lambda i, s: (i, 0)),
        scratch_shapes=[pltpu.VMEM((2, TOP_K, tb, D_MODEL), F32), pltpu.SemaphoreType.DMA((2,))],
    )
    return pl.pallas_call(
        _combine_body,
        grid_spec=grid_spec,
        out_shape=jax.ShapeDtypeStruct((T, D_MODEL), F32),
        compiler_params=pltpu.CompilerParams(dimension_semantics=("arbitrary",),
                                             vmem_limit_bytes=VMEM_LIMIT),
        name="combine",
    )(slots, h, wcol, ys)


def _rope_tables(T):
    inv_freq = jnp.power(jnp.float32(ROPE_THETA),
                         -jnp.arange(ROPE_HALF, dtype=F32) * (2.0 / ROPE_DIM))
    ang = jnp.arange(T).astype(F32)[:, None] * inv_freq[None, :]
    cos, sin = jnp.cos(ang), jnp.sin(ang)
    zeros = jnp.zeros((T, HEAD_DIM - ROPE_DIM), F32)
    zh = jnp.zeros((T, ROPE_HALF), F32)
    cos_t = jnp.concatenate([cos, cos, jnp.ones((T, HEAD_DIM - ROPE_DIM), F32)], axis=1)
    sa_t = jnp.concatenate([zh, sin, zeros], axis=1)
    sb_t = jnp.concatenate([-sin, zh, zeros], axis=1)
    return cos_t, sa_t, sb_t


def _split_hi_lo(w):
    hi = w.astype(BF16)
    lo = (w - hi.astype(F32)).astype(BF16)
    return hi, lo


def _layer(x2, ln1_w, w_in, q_norm_w, k_norm_w, attn_sink, attn_out_norm_w, gate_up_f, gate_bias_f,
           gate_up_b, gate_bias_b, gla_out_norm_w, w_out, ln2_w, w_group, b_group, w_router, b_router,
           w_gate_e, w_up_e, w_down_e):
    T = x2.shape[0]
    row = lambda v: v.reshape(1, -1).astype(F32)

    w_in_b = w_in.astype(BF16)
    c0, c1, c2, c3 = A_WIDTH, A_WIDTH + 2 * GLA_QK_WIDTH, A_WIDTH + 2 * GLA_QK_WIDTH + GLA_WIDTH, \
        A_WIDTH + 2 * GLA_QK_WIDTH + 2 * GLA_WIDTH
    wa, wqk, wv, wr, wl = (w_in_b[:, :c0], w_in_b[:, c0:c1], w_in_b[:, c1:c2], w_in_b[:, c2:c3],
                           w_in_b[:, c3:])
    cos_t, sa_t, sb_t = _rope_tables(T)
    pa, pqk, pv, pr, plr = _inproj(x2, row(ln1_w), wa, wqk, wv, wr, wl, row(q_norm_w), row(k_norm_w),
                                   cos_t, sa_t, sb_t)

    attn = _attention(pa, attn_sink.astype(F32), row(attn_out_norm_w))

    C = GLA_CHUNK
    ones = jnp.ones((C, C), F32)
    zr = jnp.zeros((GLA_RANK, GLA_QK_WIDTH), F32)
    up_f = jnp.concatenate([gate_up_f.astype(F32), zr], axis=0)
    up_b = jnp.concatenate([zr, gate_up_b.astype(F32)], axis=0)

    def up_pieces(up):
        hi, lo = _split_hi_lo(up)
        return jnp.concatenate([hi, hi, lo], axis=0)

    as_rows = lambda w: w.astype(F32).reshape(-1, w.shape[-1])
    o_f, wg_b = _gla_pass(False, pqk, pv, plr, up_pieces(up_f), row(gate_bias_f),
                          jnp.tril(ones).astype(BF16), as_rows(w_gate_e))
    gla, wu_b = _gla_pass(True, pqk, pv, plr, up_pieces(up_b), row(gate_bias_b),
                          jnp.triu(ones).astype(BF16), as_rows(w_up_e), extra=(o_f, pr, row(gla_out_norm_w)))

    w_out_b = w_out.astype(BF16)
    wr_full = jnp.zeros((D_MODEL, ROUTER_ROWS), F32)
    wr_full = wr_full.at[:, :N_GROUPS].set(w_group.astype(F32))
    wr_full = wr_full.at[:, EXPERT_ROW0:EXPERT_ROW0 + N_EXPERTS].set(w_router.astype(F32))
    wr_hi, wr_lo = _split_hi_lo(wr_full)
    rbias = jnp.zeros((ROUTER_ROWS, 1), F32)
    rbias = rbias.at[:N_GROUPS, 0].set(b_group.astype(F32))
    rbias = rbias.at[EXPERT_ROW0:EXPERT_ROW0 + N_EXPERTS, 0].set(b_router.astype(F32))
    tm = OUT_ROWS
    earlier = jnp.triu(jnp.ones((tm, tm), F32), k=1).astype(BF16)
    h, xn, r_int, r_w, counts, wd_b = _outproj_router(
        attn, gla, x2, w_out_b[:ATTN_WIDTH], w_out_b[ATTN_WIDTH:], row(ln2_w),
        jnp.concatenate([wr_hi, wr_lo], axis=1), wr_hi, rbias, earlier, as_rows(w_down_e))

    bm = MOE_ROWS
    n_blocks = (T * TOP_K) // bm + N_EXPERTS
    cnt = counts[:, 0].astype(jnp.int32)
    padded = (cnt + bm - 1) // bm * bm
    pad_end = jnp.cumsum(padded)
    pad_start = pad_end - padded
    n_used = (pad_end[-1] // bm).astype(jnp.int32).reshape(1)
    block_e = jnp.minimum(jnp.sum(pad_end[None, :] <= (jnp.arange(n_blocks) * bm)[:, None], axis=1),
                          N_EXPERTS - 1).astype(jnp.int32)
    e_sel = r_int[0:TOP_K].T
    start_sel = jnp.sum(jnp.where(e_sel[..., None] == jnp.arange(N_EXPERTS), pad_start, 0), axis=-1)
    slots = (start_sel + r_int[TOP_K:2 * TOP_K].T).reshape(-1).astype(jnp.int32)

    ys = _moe(block_e, n_used, slots, (pad_start + cnt).astype(jnp.int32), pad_end.astype(jnp.int32), xn,
              wg_b.reshape(w_gate_e.shape), wu_b.reshape(w_up_e.shape), wd_b.reshape(w_down_e.shape),
              n_blocks * bm)
    return _combine(slots, h, r_w[0:TOP_K].T, ys)


def kernel(x, ln1_w, w_in, q_norm_w, k_norm_w, attn_sink, attn_out_norm_w, gla_gate_up_f, gla_gate_bias_f,
           gla_gate_up_b, gla_gate_bias_b, gla_out_norm_w, w_out, ln2_w, w_group, b_group, w_router,
           b_router, w_gate_e, w_up_e, w_down_e):
    B, S, D = x.shape
    h = x.reshape(B * S, D)
    assert B == 1
    for l in range(ln1_w.shape[0]):
        h = _layer(h, ln1_w[l], w_in[l], q_norm_w[l], k_norm_w[l], attn_sink[l], attn_out_norm_w[l],
                   gla_gate_up_f[l], gla_gate_bias_f[l], gla_gate_up_b[l], gla_gate_bias_b[l],
                   gla_out_norm_w[l], w_out[l], ln2_w[l], w_group[l], b_group[l], w_router[l],
                   b_router[l], w_gate_e[l], w_up_e[l], w_down_e[l])
    return h.reshape(B, S, D)
```

```python
import functools
from typing import NamedTuple

import jax
import jax.numpy as jnp
from jax import lax
from jax.experimental import pallas as pl
from jax.experimental.pallas import tpu as pltpu

F32 = jnp.float32
BF16 = jnp.bfloat16

EPS = 1e-6
D_MODEL = 2048

ATTN_Q_HEADS = 8
ATTN_KV_HEADS = 2
ATTN_GROUP = ATTN_Q_HEADS // ATTN_KV_HEADS
HEAD_DIM = 128
WINDOW = 128
ATTN_BLOCK = 128
ATTN_STEP_BLOCKS = 4
ROPE_THETA = 500000.0
ROPE_DIM = HEAD_DIM // 4
ROPE_HALF = ROPE_DIM // 2
ATTN_WIDTH = ATTN_Q_HEADS * HEAD_DIM
KV_WIDTH = ATTN_KV_HEADS * HEAD_DIM
A_WIDTH = ATTN_WIDTH + 2 * KV_WIDTH

GLA_HEADS = 4
GLA_DK = 128
GLA_DV = 256
GLA_RANK = 16
GLA_TAU = 16.0
GLA_QK_WIDTH = GLA_HEADS * GLA_DK
GLA_WIDTH = GLA_HEADS * GLA_DV
GLA_CHUNK = 128

N_GROUPS = 4
EXPERTS_PER_GROUP = 8
N_EXPERTS = N_GROUPS * EXPERTS_PER_GROUP
TOP_K = 2
D_FF = 1024
ROUTER_ROWS = 128
EXPERT_ROW0 = 8

PROJ_ROWS = 256
OUT_ROWS = 256
MOE_ROWS = 256
MOE_ROW_BUFFERS = 3
COMBINE_ROWS = 256

VMEM_LIMIT = 56 * 1024 * 1024


def _dot(a, b):
    return jnp.dot(a, b, preferred_element_type=F32)


def _dot_nt(a, b):
    return lax.dot_general(a, b, (((1,), (1,)), ((), ())), preferred_element_type=F32)


def _dot_tn(a, b):
    return lax.dot_general(a, b, (((0,), (0,)), ((), ())), preferred_element_type=F32)


def _pack_bf16_pairs(x):
    n = x.shape[1] // 2
    lo = lax.bitcast_convert_type(x[:, :n].astype(F32), jnp.uint32) >> 16
    hi = lax.bitcast_convert_type(x[:, n:].astype(F32), jnp.uint32) & jnp.uint32(0xFFFF0000)
    return lo | hi


def _unpack_bf16_pairs(w):
    lo = lax.bitcast_convert_type(w << 16, F32).astype(BF16)
    hi = lax.bitcast_convert_type(w & jnp.uint32(0xFFFF0000), F32).astype(BF16)
    return lo, hi


class CastJob(NamedTuple):
    array: jax.Array
    in_spec: pl.BlockSpec
    out_spec: pl.BlockSpec
    out_shape: jax.ShapeDtypeStruct


def _cast_rows_job(w, n_steps):
    w2 = w.astype(F32).reshape(-1, w.shape[-1])
    rows = w2.shape[0] // n_steps
    assert rows * n_steps == w2.shape[0] and rows % 16 == 0
    spec = pl.BlockSpec((rows, w2.shape[1]), lambda i, *_: (i, 0))
    return CastJob(w2, spec, spec, jax.ShapeDtypeStruct(w2.shape, BF16))


def _cast_half_rows_job(w, half, n_steps):
    E, R, C = w.shape
    per_e = n_steps // E
    assert per_e * E == n_steps
    rows = (R // 2) // per_e
    assert rows * per_e * 2 == R and rows % 16 == 0
    in_spec = pl.BlockSpec((1, rows, C), lambda i, *_: (i // per_e, half * per_e + i % per_e, 0))
    out_spec = pl.BlockSpec((1, rows, C), lambda i, *_: (i // per_e, i % per_e, 0))
    return CastJob(w.astype(F32), in_spec, out_spec, jax.ShapeDtypeStruct((E, R // 2, C), BF16))


def _resident(shape):
    nd = len(shape)
    return pl.BlockSpec(shape, lambda *_: (0,) * nd, pipeline_mode=pl.Buffered(1))


def _inproj_body(x_ref, ln_ref, wa_ref, wqk_ref, wv_ref, wr_ref, wl_ref, qn_ref, kn_ref,
                 cos_ref, sa_ref, sb_ref, wcast_ref, oa_ref, oqk_ref, ov_ref, or_ref, ol_ref, wcast_out_ref):
    wcast_out_ref[...] = wcast_ref[...].astype(BF16)
    x = x_ref[...]
    ms = jnp.mean(x * x, axis=-1, keepdims=True)
    xn = (x * lax.rsqrt(ms + EPS) * ln_ref[...]).astype(BF16)
    oqk_ref[...] = _dot(xn, wqk_ref[...]).astype(BF16)
    ov_ref[...] = _dot(xn, wv_ref[...]).astype(BF16)
    or_ref[...] = _dot(xn, wr_ref[...]).astype(BF16)
    ol_ref[...] = _dot(xn, wl_ref[...])
    acc = _dot(xn, wa_ref[...])
    cos, sa, sb = cos_ref[...], sa_ref[...], sb_ref[...]
    for c in range(ATTN_Q_HEADS + ATTN_KV_HEADS):
        xh = acc[:, c * HEAD_DIM:(c + 1) * HEAD_DIM]
        w = qn_ref[...] if c < ATTN_Q_HEADS else kn_ref[...]
        y = xh * lax.rsqrt(jnp.mean(xh * xh, axis=-1, keepdims=True) + EPS) * w
        y = (y * cos + pltpu.roll(y, ROPE_HALF, 1) * sa
             + pltpu.roll(y, HEAD_DIM - ROPE_HALF, 1) * sb)
        if c < ATTN_Q_HEADS:
            y = y * (HEAD_DIM ** -0.5)
        oa_ref[:, c * HEAD_DIM:(c + 1) * HEAD_DIM] = y.astype(BF16)
    oa_ref[:, ATTN_WIDTH + KV_WIDTH:] = acc[:, ATTN_WIDTH + KV_WIDTH:].astype(BF16)


def _inproj(x2, ln_w, wa, wqk, wv, wr, wl, qn, kn, cos_t, sa_t, sb_t, cast):
    T = x2.shape[0]
    tm = PROJ_ROWS
    row = lambda w: pl.BlockSpec((tm, w), lambda i: (i, 0))
    return pl.pallas_call(
        _inproj_body,
        grid=(T // tm,),
        in_specs=[row(D_MODEL), _resident((1, D_MODEL)),
                  _resident(wa.shape), _resident(wqk.shape), _resident(wv.shape),
                  _resident(wr.shape), _resident(wl.shape),
                  _resident((1, HEAD_DIM)), _resident((1, HEAD_DIM)),
                  row(HEAD_DIM), row(HEAD_DIM), row(HEAD_DIM), cast.in_spec],
        out_specs=[row(A_WIDTH), row(2 * GLA_QK_WIDTH), row(GLA_WIDTH), row(GLA_WIDTH),
                   row(2 * GLA_RANK), cast.out_spec],
        out_shape=[jax.ShapeDtypeStruct((T, A_WIDTH), BF16),
                   jax.ShapeDtypeStruct((T, 2 * GLA_QK_WIDTH), BF16),
                   jax.ShapeDtypeStruct((T, GLA_WIDTH), BF16),
                   jax.ShapeDtypeStruct((T, GLA_WIDTH), BF16),
                   jax.ShapeDtypeStruct((T, 2 * GLA_RANK), F32), cast.out_shape],
        compiler_params=pltpu.CompilerParams(dimension_semantics=("parallel",),
                                             vmem_limit_bytes=VMEM_LIMIT),
        name="inproj",
    )(x2, ln_w, wa, wqk, wv, wr, wl, qn, kn, cos_t, sa_t, sb_t, cast.array)


def _attn_body(sink_ref, q_ref, kp_ref, kc_ref, kn_ref, vp_ref, vc_ref, vn_ref, nw_ref, wcast_ref,
               o_ref, wcast_out_ref):
    wcast_out_ref[...] = wcast_ref[...].astype(BF16)
    n = pl.program_id(0)
    nb = pl.num_programs(0)
    rows = ATTN_GROUP * ATTN_BLOCK
    keys = 3 * ATTN_BLOCK
    r = lax.broadcasted_iota(jnp.int32, (rows, keys), 0) & (ATTN_BLOCK - 1)
    c = lax.broadcasted_iota(jnp.int32, (rows, keys), 1)
    band = (c >= r + (ATTN_BLOCK - WINDOW)) & (c <= r + (ATTN_BLOCK + WINDOW))
    edge_mask = [band & ((c >= ATTN_BLOCK) | (n > 0))] + [band] * (ATTN_STEP_BLOCKS - 2) + \
                [band & ((c < 2 * ATTN_BLOCK) | (n < nb - 1))]

    def window(j, prev_ref, cur_ref, next_ref, ks):
        blocks = [prev_ref[:, ks]] + [cur_ref[i * ATTN_BLOCK:(i + 1) * ATTN_BLOCK, ks]
                                      for i in range(ATTN_STEP_BLOCKS)] + [next_ref[:, ks]]
        return jnp.concatenate(blocks[j:j + 3], axis=0)

    chains = [(j, g) for j in range(ATTN_STEP_BLOCKS) for g in range(ATTN_KV_HEADS)]
    heads_of = lambda g: range(g * ATTN_GROUP, (g + 1) * ATTN_GROUP)
    ks_of = lambda g: slice(g * HEAD_DIM, (g + 1) * HEAD_DIM)
    qrows_of = lambda j: slice(j * ATTN_BLOCK, (j + 1) * ATTN_BLOCK)
    s = [jnp.where(edge_mask[j],
                   _dot_nt(jnp.concatenate([q_ref[qrows_of(j), h * HEAD_DIM:(h + 1) * HEAD_DIM]
                                            for h in heads_of(g)], axis=0),
                           window(j, kp_ref, kc_ref, kn_ref, ks_of(g))), -jnp.inf)
         for j, g in chains]
    sink = [jnp.concatenate([jnp.full((ATTN_BLOCK, 1), sink_ref[h], F32) for h in heads_of(g)], axis=0)
            for j, g in chains]
    m = [jnp.maximum(jnp.max(si, axis=-1, keepdims=True), sk) for si, sk in zip(s, sink)]
    e = [jnp.exp(si - mi) for si, mi in zip(s, m)]
    denom = [jnp.sum(ei, axis=-1, keepdims=True) + jnp.exp(sk - mi) for ei, sk, mi in zip(e, sink, m)]
    o = [_dot(ei.astype(BF16), window(j, vp_ref, vc_ref, vn_ref, ks_of(g))) / di
         for (j, g), ei, di in zip(chains, e, denom)]
    for j in range(ATTN_STEP_BLOCKS):
        outs = [o[chains.index((j, g))][i * ATTN_BLOCK:(i + 1) * ATTN_BLOCK]
                for g in range(ATTN_KV_HEADS) for i in range(ATTN_GROUP)]
        a = jnp.concatenate(outs, axis=1)
        a = a * lax.rsqrt(jnp.mean(a * a, axis=-1, keepdims=True) + EPS) * nw_ref[...]
        o_ref[qrows_of(j), :] = a.astype(BF16)


def _attention(pa, sink, norm_w, cast):
    T = pa.shape[0]
    sb = ATTN_STEP_BLOCKS
    nb = T // ATTN_BLOCK
    kcol = ATTN_WIDTH // KV_WIDTH
    vcol = kcol + 1
    prev = lambda col: pl.BlockSpec((ATTN_BLOCK, KV_WIDTH), lambda n, s: (jnp.maximum(sb * n - 1, 0), col))
    nxt = lambda col: pl.BlockSpec((ATTN_BLOCK, KV_WIDTH),
                                   lambda n, s: (jnp.minimum(sb * n + sb, nb - 1), col))
    cur = lambda col: pl.BlockSpec((sb * ATTN_BLOCK, KV_WIDTH), lambda n, s: (n, col))
    grid_spec = pltpu.PrefetchScalarGridSpec(
        num_scalar_prefetch=1,
        grid=(nb // sb,),
        in_specs=[pl.BlockSpec((sb * ATTN_BLOCK, ATTN_WIDTH), lambda n, s: (n, 0)),
                  prev(kcol), cur(kcol), nxt(kcol), prev(vcol), cur(vcol), nxt(vcol),
                  pl.BlockSpec((1, ATTN_WIDTH), lambda n, s: (0, 0)), cast.in_spec],
        out_specs=[pl.BlockSpec((sb * ATTN_BLOCK, ATTN_WIDTH), lambda n, s: (n, 0)), cast.out_spec],
    )
    return pl.pallas_call(
        _attn_body,
        grid_spec=grid_spec,
        out_shape=[jax.ShapeDtypeStruct((T, ATTN_WIDTH), BF16), cast.out_shape],
        compiler_params=pltpu.CompilerParams(dimension_semantics=("parallel",),
                                             vmem_limit_bytes=VMEM_LIMIT),
        name="attention",
    )(sink, pa, pa, pa, pa, pa, pa, pa, norm_w, cast.array)


def _split3(x):
    hi = x.astype(BF16)
    r1 = x - hi.astype(F32)
    mid = r1.astype(BF16)
    lo = (r1 - mid.astype(F32)).astype(BF16)
    return hi, mid, lo


def _gla_body(reverse, final, *refs):
    if final:
        (q_ref, k_ref, v_ref, lr_ref, up_ref, bias_ref, tri_ref, of_ref, gr_ref, nw_ref, wcast_ref,
         o_ref, wcast_out_ref, st_ref) = refs
    else:
        (q_ref, k_ref, v_ref, lr_ref, up_ref, bias_ref, tri_ref, wcast_ref,
         o_ref, wcast_out_ref, st_ref) = refs
    C = GLA_CHUNK
    wcast_out_ref[...] = wcast_ref[...].astype(BF16)

    @pl.when(pl.program_id(0) == 0)
    def _():
        st_ref[...] = jnp.zeros_like(st_ref)

    lr = lr_ref[...]
    lr_hi = lr.astype(BF16)
    lr_lo = (lr - lr_hi.astype(F32)).astype(BF16)
    g = _dot(jnp.concatenate([lr_hi, lr_lo, lr_hi], axis=1), up_ref[...]) + bias_ref[...]
    la = jax.nn.log_sigmoid(g) * (1.0 / GLA_TAU)
    tri = tri_ref[...]
    hi, mid, lo = _split3(la)
    b = _dot(tri, hi) + _dot(tri, mid) + _dot(tri, lo)
    end = 0 if reverse else C - 1
    b_end = b[end:end + 1]
    b_mid = b[C // 2:C // 2 + 1]
    scale = GLA_DK ** -0.5
    q = q_ref[...].astype(F32) * scale
    k = k_ref[...].astype(F32)
    q_in = (q * jnp.exp(b - b_mid)).astype(BF16)
    k_in = (k * jnp.exp(b_mid - b)).astype(BF16)
    q_dec = (q * jnp.exp(b)).astype(BF16)
    k_end = (k * jnp.exp(b_end - b)).astype(BF16)
    decay = jnp.exp(b_end)
    ri = lax.broadcasted_iota(jnp.int32, (C, C), 0)
    ci = lax.broadcasted_iota(jnp.int32, (C, C), 1)
    causal = (ri <= ci) if reverse else (ri >= ci)
    heads = range(GLA_HEADS)
    ks = [slice(h * GLA_DK, (h + 1) * GLA_DK) for h in heads]
    vs = [slice(h * GLA_DV, (h + 1) * GLA_DV) for h in heads]
    att = [jnp.where(causal, _dot_nt(q_in[:, ks[h]], k_in[:, ks[h]]), 0.0).astype(BF16) for h in heads]
    st = [st_ref[h] for h in heads]
    inter = [_dot_nt(q_dec[:, ks[h]], st[h].astype(BF16)) for h in heads]
    o = [_dot(att[h], v_ref[:, vs[h]]) + inter[h] for h in heads]
    for h in heads:
        st_ref[h] = st[h] * decay[:, ks[h]] + _dot_tn(v_ref[:, vs[h]], k_end[:, ks[h]])
    if final:
        tot = [o[h] + of_ref[:, vs[h]] for h in heads]
        inv = [lax.rsqrt(jnp.mean(t * t, axis=-1, keepdims=True) + EPS) for t in tot]
        for h in heads:
            y = tot[h] * inv[h] * nw_ref[...]
            o_ref[:, vs[h]] = (y * jax.nn.silu(gr_ref[:, vs[h]].astype(F32))).astype(BF16)
    else:
        for h in heads:
            o_ref[:, vs[h]] = o[h]


def _gla_pass(reverse, pqk, pv, plr, up, bias, tri, cast, extra=None):
    T = pqk.shape[0]
    C = GLA_CHUNK
    n = T // C
    final = extra is not None
    blk = (lambda i: n - 1 - i) if reverse else (lambda i: i)
    row = lambda w, col=0: pl.BlockSpec((C, w), lambda i: (blk(i), col))
    in_specs = [row(GLA_QK_WIDTH, 0), row(GLA_QK_WIDTH, 1), row(GLA_WIDTH), row(2 * GLA_RANK),
                _resident(up.shape), _resident(bias.shape), _resident(tri.shape)]
    args = [pqk, pqk, pv, plr, up, bias, tri]
    if final:
        o_f, pr, norm_w = extra
        in_specs += [row(GLA_WIDTH), row(GLA_WIDTH), _resident(norm_w.shape)]
        args += [o_f, pr, norm_w]
    in_specs.append(cast.in_spec)
    args.append(cast.array)
    return pl.pallas_call(
        functools.partial(_gla_body, reverse, final),
        grid=(n,),
        in_specs=in_specs,
        out_specs=[row(GLA_WIDTH), cast.out_spec],
        out_shape=[jax.ShapeDtypeStruct((T, GLA_WIDTH), BF16 if final else F32), cast.out_shape],
        scratch_shapes=[pltpu.VMEM((GLA_HEADS, GLA_DV, GLA_DK), F32)],
        compiler_params=pltpu.CompilerParams(dimension_semantics=("arbitrary",),
                                             vmem_limit_bytes=VMEM_LIMIT),
        name="gla_bwd" if reverse else "gla_fwd",
    )(*args)


def _outproj_body(a_ref, g_ref, x_ref, wa_ref, wg_ref, ln_ref, wr_hl_ref, wr_hi_ref, rb_ref, tri_ref,
                  h_ref, xn_ref, ri_ref, rw_ref, cnt_ref):
    tm = x_ref.shape[0]

    @pl.when(pl.program_id(0) == 0)
    def _():
        cnt_ref[...] = jnp.zeros_like(cnt_ref)

    h = x_ref[...] + _dot(a_ref[...], wa_ref[...]) + _dot(g_ref[...], wg_ref[...])
    h_ref[...] = h
    xn = h * lax.rsqrt(jnp.mean(h * h, axis=-1, keepdims=True) + EPS) * ln_ref[...]
    x_hi = xn.astype(BF16)
    xn_ref[...] = _pack_bf16_pairs(x_hi)
    x_lo = (xn - x_hi.astype(F32)).astype(BF16)
    l2 = _dot(x_hi, wr_hl_ref[...])
    logits = l2[:, :ROUTER_ROWS] + l2[:, ROUTER_ROWS:] + _dot(x_lo, wr_hi_ref[...])
    lt = jnp.transpose(logits) + rb_ref[...]

    gl = lt[0:N_GROUPS]
    gmax = jnp.max(gl, axis=0, keepdims=True)
    gi = lax.broadcasted_iota(jnp.int32, gl.shape, 0).astype(F32)
    g_sel = jnp.min(jnp.where(gl == gmax, gi, float(N_GROUPS)), axis=0, keepdims=True)
    g_gate = 1.0 / jnp.sum(jnp.exp(gl - gmax), axis=0, keepdims=True)

    el = lt[EXPERT_ROW0:EXPERT_ROW0 + N_EXPERTS]
    ei_int = lax.broadcasted_iota(jnp.int32, el.shape, 0)
    ei = ei_int.astype(F32)
    grp = (ei_int >> 3).astype(F32)
    cand = jnp.where(grp == g_sel, el, -jnp.inf)
    v1 = jnp.max(cand, axis=0, keepdims=True)
    e1 = jnp.min(jnp.where(cand == v1, ei, float(N_EXPERTS)), axis=0, keepdims=True)
    cand2 = jnp.where(ei == e1, -jnp.inf, cand)
    v2 = jnp.max(cand2, axis=0, keepdims=True)
    e2 = jnp.min(jnp.where(cand2 == v2, ei, float(N_EXPERTS)), axis=0, keepdims=True)
    d = jnp.exp(v2 - v1)
    w1 = g_gate / (1.0 + d)
    w2 = g_gate * d / (1.0 + d)

    oh1 = (ei == e1).astype(F32)
    oh2 = (ei == e2).astype(F32)
    cnt = oh1 + oh2
    before = _dot(cnt.astype(BF16), tri_ref[...]) + cnt_ref[:, 0:1]
    r1 = jnp.sum(oh1 * before, axis=0, keepdims=True)
    r2 = jnp.sum(oh2 * before, axis=0, keepdims=True)
    cnt_ref[...] = cnt_ref[...] + jnp.sum(cnt, axis=1, keepdims=True)

    ri_ref[...] = jnp.concatenate([e1, e2, r1, r2, jnp.zeros((4, tm), F32)], axis=0).astype(jnp.int32)
    rw_ref[...] = jnp.concatenate([w1, w2, jnp.zeros((6, tm), F32)], axis=0)


def _outproj_router(attn, gla, x2, w_attn, w_gla, ln_w, wr_hl, wr_hi, rbias, tri):
    T = x2.shape[0]
    tm = OUT_ROWS
    row = lambda w: pl.BlockSpec((tm, w), lambda i: (i, 0))
    col = lambda r: pl.BlockSpec((r, tm), lambda i: (0, i))
    return pl.pallas_call(
        _outproj_body,
        grid=(T // tm,),
        in_specs=[row(ATTN_WIDTH), row(GLA_WIDTH), row(D_MODEL),
                  _resident(w_attn.shape), _resident(w_gla.shape), _resident(ln_w.shape),
                  _resident(wr_hl.shape), _resident(wr_hi.shape), _resident(rbias.shape),
                  _resident(tri.shape)],
        out_specs=[row(D_MODEL), row(D_MODEL // 2), col(8), col(8),
                   pl.BlockSpec((N_EXPERTS, 128), lambda i: (0, 0))],
        out_shape=[jax.ShapeDtypeStruct((T, D_MODEL), F32),
                   jax.ShapeDtypeStruct((T, D_MODEL // 2), jnp.uint32),
                   jax.ShapeDtypeStruct((8, T), jnp.int32),
                   jax.ShapeDtypeStruct((8, T), F32),
                   jax.ShapeDtypeStruct((N_EXPERTS, 128), F32)],
        compiler_params=pltpu.CompilerParams(dimension_semantics=("arbitrary",),
                                             vmem_limit_bytes=VMEM_LIMIT),
        name="outproj_router",
    )(attn, gla, x2, w_attn, w_gla, ln_w, wr_hl, wr_hi, rbias, tri)


def _row_copy(src, s, dst, d, sem):
    return pltpu.make_async_copy(src.at[pl.ds(s, 1)], dst.at[pl.ds(d, 1)], sem)


def _moe_body(be_ref, nu_ref, slot_ref, lo_ref, hi_ref, xn_ref, wg_ref, wu_ref, wd_top_ref, wd_bot_ref,
              y_ref, xbuf, xb_ref, tok_ref, sem):
    bm = MOE_ROWS
    b = pl.program_id(0)
    n_used = nu_ref[0]
    cur = b % MOE_ROW_BUFFERS

    def gather_start(block, buf, unrolled):
        def one(i):
            _row_copy(xn_ref, tok_ref[block * bm + i], xbuf.at[buf], i, sem.at[buf]).start()
        if unrolled:
            for i in range(bm):
                one(i)
        else:
            lax.fori_loop(0, bm, lambda i, c: (one(i), c)[1], 0)

    def gather_wait(buf):
        pltpu.make_async_copy(xn_ref.at[pl.ds(0, bm)], xbuf.at[buf], sem.at[buf]).wait()

    @pl.when(b == 0)
    def _():
        def per_expert(e, c):
            def zero(s, c2):
                tok_ref[s] = 0
                return c2
            return lax.fori_loop(lo_ref[e], hi_ref[e], zero, c)
        lax.fori_loop(0, N_EXPERTS, per_expert, 0)

        def fill(t, c):
            for k in range(TOP_K):
                tok_ref[slot_ref[TOP_K * t + k]] = t
            return c
        lax.fori_loop(0, slot_ref.shape[0] // TOP_K, fill, 0, unroll=8)
        gather_start(0, 0, False)

        @pl.when(n_used > 1)
        def _():
            gather_start(1, 1, False)

    def stage_rows():
        gather_wait(cur)
        lo, hi = _unpack_bf16_pairs(xbuf[cur])
        xb_ref[:, :D_MODEL // 2] = lo
        xb_ref[:, D_MODEL // 2:] = hi

    def experts():
        x = xb_ref[...]
        hid = jax.nn.silu(_dot(x, wg_ref[0])) * _dot(x, wu_ref[0])
        hid = hid.astype(BF16)
        half = D_FF // 2
        y_ref[...] = _dot(hid[:, :half], wd_top_ref[0]) + _dot(hid[:, half:], wd_bot_ref[0])

    @pl.when(b + 2 < n_used)
    def _():
        stage_rows()
        gather_start(b + 2, (b + 2) % MOE_ROW_BUFFERS, True)
        experts()

    @pl.when((b < n_used) & (b + 2 >= n_used))
    def _():
        stage_rows()
        experts()

    @pl.when(b >= n_used)
    def _():
        y_ref[...] = jnp.zeros_like(y_ref)


def _moe(block_e, n_used, slots, pad_lo, pad_hi, xn, wg, wu, wd_top, wd_bot, n_slots):
    bm = MOE_ROWS
    wspec = lambda shape: pl.BlockSpec((1,) + shape, lambda b, be, *_: (be[b], 0, 0))
    grid_spec = pltpu.PrefetchScalarGridSpec(
        num_scalar_prefetch=5,
        grid=(n_slots // bm,),
        in_specs=[pl.BlockSpec(memory_space=pl.ANY),
                  wspec((D_MODEL, D_FF)), wspec((D_MODEL, D_FF)),
                  wspec((D_FF // 2, D_MODEL)), wspec((D_FF // 2, D_MODEL))],
        out_specs=pl.BlockSpec((bm, D_MODEL), lambda b, *_: (b, 0)),
        scratch_shapes=[pltpu.VMEM((MOE_ROW_BUFFERS, bm, D_MODEL // 2), jnp.uint32),
                        pltpu.VMEM((bm, D_MODEL), BF16),
                        pltpu.SMEM((n_slots,), jnp.int32),
                        pltpu.SemaphoreType.DMA((MOE_ROW_BUFFERS,))],
    )
    return pl.pallas_call(
        _moe_body,
        grid_spec=grid_spec,
        out_shape=jax.ShapeDtypeStruct((n_slots, D_MODEL), F32),
        compiler_params=pltpu.CompilerParams(dimension_semantics=("arbitrary",),
                                             vmem_limit_bytes=VMEM_LIMIT),
        name="moe",
    )(block_e, n_used, slots, pad_lo, pad_hi, xn, wg, wu, wd_top, wd_bot)


def _combine_body(slot_ref, h_ref, w_ref, ys_ref, o_ref, g_ref, sem):
    tb = COMBINE_ROWS
    i = pl.program_id(0)
    cur = i % 2

    def gather_start(block, buf):
        def one(t, c):
            for k in range(TOP_K):
                _row_copy(ys_ref, slot_ref[TOP_K * (block * tb + t) + k], g_ref.at[buf, k], t,
                          sem.at[buf]).start()
            return c
        lax.fori_loop(0, tb, one, 0, unroll=8)

    @pl.when(i == 0)
    def _():
        gather_start(0, 0)

    @pl.when(i + 1 < pl.num_programs(0))
    def _():
        gather_start(i + 1, 1 - cur)

    for k in range(TOP_K):
        pltpu.make_async_copy(ys_ref.at[pl.ds(0, tb)], g_ref.at[cur, k], sem.at[cur]).wait()
    o_ref[...] = h_ref[...] + w_ref[:, 0:1] * g_ref[cur, 0] + w_ref[:, 1:2] * g_ref[cur, 1]


def _combine(slots, h, wcol, ys):
    T = h.shape[0]
    tb = COMBINE_ROWS
    grid_spec = pltpu.PrefetchScalarGridSpec(
        num_scalar_prefetch=1,
        grid=(T // tb,),
        in_specs=[pl.BlockSpec((tb, D_MODEL), lambda i, s: (i, 0)),
                  pl.BlockSpec((tb, TOP_K), lambda i, s: (i, 0)),
                  pl.BlockSpec(memory_space=pl.ANY)],
        out_specs=pl.BlockSpec((tb, D_MODEL), lambda i, s: (i, 0)),
        scratch_shapes=[pltpu.VMEM((2, TOP_K, tb, D_MODEL), F32), pltpu.SemaphoreType.DMA((2,))],
    )
    return pl.pallas_call(
        _combine_body,
        grid_spec=grid_spec,
        out_shape=jax.ShapeDtypeStruct((T, D_MODEL), F32),
        compiler_params=pltpu.CompilerParams(dimension_semantics=("arbitrary",),
                                             vmem_limit_bytes=VMEM_LIMIT),
        name="combine",
    )(slots, h, wcol, ys)


def _rope_tables(T):
    inv_freq = jnp.power(jnp.float32(ROPE_THETA),
                         -jnp.arange(ROPE_HALF, dtype=F32) * (2.0 / ROPE_DIM))
    ang = jnp.arange(T).astype(F32)[:, None] * inv_freq[None, :]
    cos, sin = jnp.cos(ang), jnp.sin(ang)
    zeros = jnp.zeros((T, HEAD_DIM - ROPE_DIM), F32)
    zh = jnp.zeros((T, ROPE_HALF), F32)
    cos_t = jnp.concatenate([cos, cos, jnp.ones((T, HEAD_DIM - ROPE_DIM), F32)], axis=1)
    sa_t = jnp.concatenate([zh, sin, zeros], axis=1)
    sb_t = jnp.concatenate([-sin, zh, zeros], axis=1)
    return cos_t, sa_t, sb_t


def _split_hi_lo(w):
    hi = w.astype(BF16)
    lo = (w - hi.astype(F32)).astype(BF16)
    return hi, lo


def _layer(x2, ln1_w, w_in, q_norm_w, k_norm_w, attn_sink, attn_out_norm_w, gate_up_f, gate_bias_f,
           gate_up_b, gate_bias_b, gla_out_norm_w, w_out, ln2_w, w_group, b_group, w_router, b_router,
           w_gate_e, w_up_e, w_down_e):
    T = x2.shape[0]
    row = lambda v: v.reshape(1, -1).astype(F32)

    w_in_b = w_in.astype(BF16)
    c0, c1, c2, c3 = A_WIDTH, A_WIDTH + 2 * GLA_QK_WIDTH, A_WIDTH + 2 * GLA_QK_WIDTH + GLA_WIDTH, \
        A_WIDTH + 2 * GLA_QK_WIDTH + 2 * GLA_WIDTH
    wa, wqk, wv, wr, wl = (w_in_b[:, :c0], w_in_b[:, c0:c1], w_in_b[:, c1:c2], w_in_b[:, c2:c3],
                           w_in_b[:, c3:])
    cos_t, sa_t, sb_t = _rope_tables(T)
    pa, pqk, pv, pr, plr, wg_b = _inproj(x2, row(ln1_w), wa, wqk, wv, wr, wl, row(q_norm_w), row(k_norm_w),
                                         cos_t, sa_t, sb_t, _cast_rows_job(w_gate_e, T // PROJ_ROWS))

    attn, wu_b = _attention(pa, attn_sink.astype(F32), row(attn_out_norm_w),
                            _cast_rows_job(w_up_e, T // (ATTN_STEP_BLOCKS * ATTN_BLOCK)))

    C = GLA_CHUNK
    ones = jnp.ones((C, C), F32)
    zr = jnp.zeros((GLA_RANK, GLA_QK_WIDTH), F32)
    up_f = jnp.concatenate([gate_up_f.astype(F32), zr], axis=0)
    up_b = jnp.concatenate([zr, gate_up_b.astype(F32)], axis=0)

    def up_pieces(up):
        hi, lo = _split_hi_lo(up)
        return jnp.concatenate([hi, hi, lo], axis=0)

    n_chunks = T // C
    o_f, wd_top = _gla_pass(False, pqk, pv, plr, up_pieces(up_f), row(gate_bias_f),
                            jnp.tril(ones).astype(BF16), _cast_half_rows_job(w_down_e, 0, n_chunks))
    gla, wd_bot = _gla_pass(True, pqk, pv, plr, up_pieces(up_b), row(gate_bias_b),
                            jnp.triu(ones).astype(BF16), _cast_half_rows_job(w_down_e, 1, n_chunks),
                            extra=(o_f, pr, row(gla_out_norm_w)))

    w_out_b = w_out.astype(BF16)
    wr_full = jnp.zeros((D_MODEL, ROUTER_ROWS), F32)
    wr_full = wr_full.at[:, :N_GROUPS].set(w_group.astype(F32))
    wr_full = wr_full.at[:, EXPERT_ROW0:EXPERT_ROW0 + N_EXPERTS].set(w_router.astype(F32))
    wr_hi, wr_lo = _split_hi_lo(wr_full)
    rbias = jnp.zeros((ROUTER_ROWS, 1), F32)
    rbias = rbias.at[:N_GROUPS, 0].set(b_group.astype(F32))
    rbias = rbias.at[EXPERT_ROW0:EXPERT_ROW0 + N_EXPERTS, 0].set(b_router.astype(F32))
    tm = OUT_ROWS
    earlier = jnp.triu(jnp.ones((tm, tm), F32), k=1).astype(BF16)
    h, xn, r_int, r_w, counts = _outproj_router(
        attn, gla, x2, w_out_b[:ATTN_WIDTH], w_out_b[ATTN_WIDTH:], row(ln2_w),
        jnp.concatenate([wr_hi, wr_lo], axis=1), wr_hi, rbias, earlier)

    bm = MOE_ROWS
    n_blocks = (T * TOP_K) // bm + N_EXPERTS
    cnt = counts[:, 0].astype(jnp.int32)
    padded = (cnt + bm - 1) // bm * bm
    pad_end = jnp.cumsum(padded)
    pad_start = pad_end - padded
    n_used = (pad_end[-1] // bm).astype(jnp.int32).reshape(1)
    block_e = jnp.minimum(jnp.sum(pad_end[None, :] <= (jnp.arange(n_blocks) * bm)[:, None], axis=1),
                          N_EXPERTS - 1).astype(jnp.int32)
    e_sel = r_int[0:TOP_K].T
    start_sel = jnp.sum(jnp.where(e_sel[..., None] == jnp.arange(N_EXPERTS), pad_start, 0), axis=-1)
    slots = (start_sel + r_int[TOP_K:2 * TOP_K].T).reshape(-1).astype(jnp.int32)

    ys = _moe(block_e, n_used, slots, (pad_start + cnt).astype(jnp.int32), pad_end.astype(jnp.int32), xn,
              wg_b.reshape(w_gate_e.shape), wu_b.reshape(w_up_e.shape), wd_top, wd_bot, n_blocks * bm)
    return _combine(slots, h, r_w[0:TOP_K].T, ys)


def kernel(x, ln1_w, w_in, q_norm_w, k_norm_w, attn_sink, attn_out_norm_w, gla_gate_up_f, gla_gate_bias_f,
           gla_gate_up_b, gla_gate_bias_b, gla_out_norm_w, w_out, ln2_w, w_group, b_group, w_router,
           b_router, w_gate_e, w_up_e, w_down_e):
    B, S, D = x.shape
    h = x.reshape(B * S, D)
    assert B == 1
    for l in range(ln1_w.shape[0]):
        h = _layer(h, ln1_w[l], w_in[l], q_norm_w[l], k_norm_w[l], attn_sink[l], attn_out_norm_w[l],
                   gla_gate_up_f[l], gla_gate_bias_f[l], gla_gate_up_b[l], gla_gate_bias_b[l],
                   gla_out_norm_w[l], w_out[l], ln2_w[l], w_group[l], b_group[l], w_router[l],
                   b_router[l], w_gate_e[l], w_up_e[l], w_down_e[l])
    return h.reshape(B, S, D)
```

```python
import functools
from typing import NamedTuple

import jax
import jax.numpy as jnp
from jax import lax
from jax.experimental import pallas as pl
from jax.experimental.pallas import tpu as pltpu

F32 = jnp.float32
BF16 = jnp.bfloat16

EPS = 1e-6
D_MODEL = 2048

ATTN_Q_HEADS = 8
ATTN_KV_HEADS = 2
ATTN_GROUP = ATTN_Q_HEADS // ATTN_KV_HEADS
HEAD_DIM = 128
WINDOW = 128
ATTN_BLOCK = 128
ATTN_STEP_BLOCKS = 4
ROPE_THETA = 500000.0
ROPE_DIM = HEAD_DIM // 4
ROPE_HALF = ROPE_DIM // 2
ATTN_WIDTH = ATTN_Q_HEADS * HEAD_DIM
KV_WIDTH = ATTN_KV_HEADS * HEAD_DIM
A_WIDTH = ATTN_WIDTH + 2 * KV_WIDTH

GLA_HEADS = 4
GLA_DK = 128
GLA_DV = 256
GLA_RANK = 16
GLA_TAU = 16.0
GLA_QK_WIDTH = GLA_HEADS * GLA_DK
GLA_WIDTH = GLA_HEADS * GLA_DV
GLA_CHUNK = 128

N_GROUPS = 4
EXPERTS_PER_GROUP = 8
N_EXPERTS = N_GROUPS * EXPERTS_PER_GROUP
TOP_K = 2
D_FF = 1024
ROUTER_ROWS = 128
EXPERT_ROW0 = 8

PROJ_ROWS = 256
OUT_ROWS = 256
MOE_ROWS = 256
MOE_ROW_BUFFERS = 3
COMBINE_ROWS = 256

VMEM_LIMIT = 56 * 1024 * 1024


def _dot(a, b):
    return jnp.dot(a, b, preferred_element_type=F32)


def _dot_nt(a, b):
    return lax.dot_general(a, b, (((1,), (1,)), ((), ())), preferred_element_type=F32)


def _dot_tn(a, b):
    return lax.dot_general(a, b, (((0,), (0,)), ((), ())), preferred_element_type=F32)


def _pack_bf16_pairs(x):
    n = x.shape[1] // 2
    lo = lax.bitcast_convert_type(x[:, :n].astype(F32), jnp.uint32) >> 16
    hi = lax.bitcast_convert_type(x[:, n:].astype(F32), jnp.uint32) & jnp.uint32(0xFFFF0000)
    return lo | hi


def _unpack_bf16_pairs(w):
    lo = lax.bitcast_convert_type(w << 16, F32)
    hi = lax.bitcast_convert_type(w & jnp.uint32(0xFFFF0000), F32)
    return lo, hi


class CastJob(NamedTuple):
    array: jax.Array
    in_spec: pl.BlockSpec
    out_spec: pl.BlockSpec
    out_shape: jax.ShapeDtypeStruct


def _cast_rows_job(w, n_steps):
    w2 = w.astype(F32).reshape(-1, w.shape[-1])
    rows = w2.shape[0] // n_steps
    assert rows * n_steps == w2.shape[0] and rows % 16 == 0
    spec = pl.BlockSpec((rows, w2.shape[1]), lambda i, *_: (i, 0))
    return CastJob(w2, spec, spec, jax.ShapeDtypeStruct(w2.shape, BF16))


def _cast_half_rows_job(w, half, n_steps):
    E, R, C = w.shape
    per_e = n_steps // E
    assert per_e * E == n_steps
    rows = (R // 2) // per_e
    assert rows * per_e * 2 == R and rows % 16 == 0
    in_spec = pl.BlockSpec((1, rows, C), lambda i, *_: (i // per_e, half * per_e + i % per_e, 0))
    out_spec = pl.BlockSpec((1, rows, C), lambda i, *_: (i // per_e, i % per_e, 0))
    return CastJob(w.astype(F32), in_spec, out_spec, jax.ShapeDtypeStruct((E, R // 2, C), BF16))


def _resident(shape):
    nd = len(shape)
    return pl.BlockSpec(shape, lambda *_: (0,) * nd, pipeline_mode=pl.Buffered(1))


def _inproj_body(x_ref, ln_ref, wa_ref, wqk_ref, wv_ref, wr_ref, wl_ref, qn_ref, kn_ref,
                 step_cos_ref, step_sin_ref, row_cos_ref, row_sin_ref, wcast_ref,
                 oa_ref, oqk_ref, ov_ref, or_ref, ol_ref, wcast_out_ref):
    wcast_out_ref[...] = wcast_ref[...].astype(BF16)
    x = x_ref[...]
    ms = jnp.mean(x * x, axis=-1, keepdims=True)
    xn = (x * lax.rsqrt(ms + EPS) * ln_ref[...]).astype(BF16)
    oqk_ref[...] = _dot(xn, wqk_ref[...]).astype(BF16)
    ov_ref[...] = _dot(xn, wv_ref[...]).astype(BF16)
    or_ref[...] = _dot(xn, wr_ref[...]).astype(BF16)
    ol_ref[...] = _dot(xn, wl_ref[...])
    acc = _dot(xn, wa_ref[...])
    ca, sn = step_cos_ref[0], step_sin_ref[0]
    cb, sb_ = row_cos_ref[...], row_sin_ref[...]
    cos = cb * ca - sb_ * sn
    sin = sb_ * ca + cb * sn
    lane = lax.broadcasted_iota(jnp.int32, (1, HEAD_DIM), 1)
    sa = jnp.where((lane >= ROPE_HALF) & (lane < ROPE_DIM), sin, 0.0)
    sb = jnp.where(lane < ROPE_HALF, -sin, 0.0)
    for c in range(ATTN_Q_HEADS + ATTN_KV_HEADS):
        xh = acc[:, c * HEAD_DIM:(c + 1) * HEAD_DIM]
        w = qn_ref[...] if c < ATTN_Q_HEADS else kn_ref[...]
        y = xh * lax.rsqrt(jnp.mean(xh * xh, axis=-1, keepdims=True) + EPS) * w
        y = (y * cos + pltpu.roll(y, ROPE_HALF, 1) * sa
             + pltpu.roll(y, HEAD_DIM - ROPE_HALF, 1) * sb)
        if c < ATTN_Q_HEADS:
            y = y * (HEAD_DIM ** -0.5)
        oa_ref[:, c * HEAD_DIM:(c + 1) * HEAD_DIM] = y.astype(BF16)
    oa_ref[:, ATTN_WIDTH + KV_WIDTH:] = acc[:, ATTN_WIDTH + KV_WIDTH:].astype(BF16)


def _inproj(x2, ln_w, wa, wqk, wv, wr, wl, qn, kn, rope, cast):
    T = x2.shape[0]
    tm = PROJ_ROWS
    row = lambda w: pl.BlockSpec((tm, w), lambda i: (i, 0))
    step_row = pl.BlockSpec((1, 1, HEAD_DIM), lambda i: (i, 0, 0))
    return pl.pallas_call(
        _inproj_body,
        grid=(T // tm,),
        in_specs=[row(D_MODEL), _resident((1, D_MODEL)),
                  _resident(wa.shape), _resident(wqk.shape), _resident(wv.shape),
                  _resident(wr.shape), _resident(wl.shape),
                  _resident((1, HEAD_DIM)), _resident((1, HEAD_DIM)),
                  step_row, step_row, _resident((tm, HEAD_DIM)), _resident((tm, HEAD_DIM)), cast.in_spec],
        out_specs=[row(A_WIDTH), row(2 * GLA_QK_WIDTH), row(GLA_WIDTH), row(GLA_WIDTH),
                   row(2 * GLA_RANK), cast.out_spec],
        out_shape=[jax.ShapeDtypeStruct((T, A_WIDTH), BF16),
                   jax.ShapeDtypeStruct((T, 2 * GLA_QK_WIDTH), BF16),
                   jax.ShapeDtypeStruct((T, GLA_WIDTH), BF16),
                   jax.ShapeDtypeStruct((T, GLA_WIDTH), BF16),
                   jax.ShapeDtypeStruct((T, 2 * GLA_RANK), F32), cast.out_shape],
        compiler_params=pltpu.CompilerParams(dimension_semantics=("parallel",),
                                             vmem_limit_bytes=VMEM_LIMIT),
        name="inproj",
    )(x2, ln_w, wa, wqk, wv, wr, wl, qn, kn, *rope, cast.array)


def _attn_body(sink_ref, q_ref, kp_ref, kc_ref, kn_ref, vp_ref, vc_ref, vn_ref, nw_ref, wcast_ref,
               o_ref, wcast_out_ref):
    wcast_out_ref[...] = wcast_ref[...].astype(BF16)
    n = pl.program_id(0)
    nb = pl.num_programs(0)
    rows = ATTN_GROUP * ATTN_BLOCK
    keys = 3 * ATTN_BLOCK
    r = lax.broadcasted_iota(jnp.int32, (rows, keys), 0) & (ATTN_BLOCK - 1)
    c = lax.broadcasted_iota(jnp.int32, (rows, keys), 1)
    band = (c >= r + (ATTN_BLOCK - WINDOW)) & (c <= r + (ATTN_BLOCK + WINDOW))
    edge_mask = [band & ((c >= ATTN_BLOCK) | (n > 0))] + [band] * (ATTN_STEP_BLOCKS - 2) + \
                [band & ((c < 2 * ATTN_BLOCK) | (n < nb - 1))]

    def window(j, prev_ref, cur_ref, next_ref, ks):
        blocks = [prev_ref[:, ks]] + [cur_ref[i * ATTN_BLOCK:(i + 1) * ATTN_BLOCK, ks]
                                      for i in range(ATTN_STEP_BLOCKS)] + [next_ref[:, ks]]
        return jnp.concatenate(blocks[j:j + 3], axis=0)

    chains = [(j, g) for j in range(ATTN_STEP_BLOCKS) for g in range(ATTN_KV_HEADS)]
    heads_of = lambda g: range(g * ATTN_GROUP, (g + 1) * ATTN_GROUP)
    ks_of = lambda g: slice(g * HEAD_DIM, (g + 1) * HEAD_DIM)
    qrows_of = lambda j: slice(j * ATTN_BLOCK, (j + 1) * ATTN_BLOCK)
    s = [jnp.where(edge_mask[j],
                   _dot_nt(jnp.concatenate([q_ref[qrows_of(j), h * HEAD_DIM:(h + 1) * HEAD_DIM]
                                            for h in heads_of(g)], axis=0),
                           window(j, kp_ref, kc_ref, kn_ref, ks_of(g))), -jnp.inf)
         for j, g in chains]
    sink = [jnp.concatenate([jnp.full((ATTN_BLOCK, 1), sink_ref[h], F32) for h in heads_of(g)], axis=0)
            for j, g in chains]
    m = [jnp.maximum(jnp.max(si, axis=-1, keepdims=True), sk) for si, sk in zip(s, sink)]
    e = [jnp.exp(si - mi) for si, mi in zip(s, m)]
    denom = [jnp.sum(ei, axis=-1, keepdims=True) + jnp.exp(sk - mi) for ei, sk, mi in zip(e, sink, m)]
    o = [_dot(ei.astype(BF16), window(j, vp_ref, vc_ref, vn_ref, ks_of(g))) / di
         for (j, g), ei, di in zip(chains, e, denom)]
    for j in range(ATTN_STEP_BLOCKS):
        outs = [o[chains.index((j, g))][i * ATTN_BLOCK:(i + 1) * ATTN_BLOCK]
                for g in range(ATTN_KV_HEADS) for i in range(ATTN_GROUP)]
        a = jnp.concatenate(outs, axis=1)
        a = a * lax.rsqrt(jnp.mean(a * a, axis=-1, keepdims=True) + EPS) * nw_ref[...]
        o_ref[qrows_of(j), :] = a.astype(BF16)


def _attention(pa, sink, norm_w, cast):
    T = pa.shape[0]
    sb = ATTN_STEP_BLOCKS
    nb = T // ATTN_BLOCK
    kcol = ATTN_WIDTH // KV_WIDTH
    vcol = kcol + 1
    prev = lambda col: pl.BlockSpec((ATTN_BLOCK, KV_WIDTH), lambda n, s: (jnp.maximum(sb * n - 1, 0), col))
    nxt = lambda col: pl.BlockSpec((ATTN_BLOCK, KV_WIDTH),
                                   lambda n, s: (jnp.minimum(sb * n + sb, nb - 1), col))
    cur = lambda col: pl.BlockSpec((sb * ATTN_BLOCK, KV_WIDTH), lambda n, s: (n, col))
    grid_spec = pltpu.PrefetchScalarGridSpec(
        num_scalar_prefetch=1,
        grid=(nb // sb,),
        in_specs=[pl.BlockSpec((sb * ATTN_BLOCK, ATTN_WIDTH), lambda n, s: (n, 0)),
                  prev(kcol), cur(kcol), nxt(kcol), prev(vcol), cur(vcol), nxt(vcol),
                  pl.BlockSpec((1, ATTN_WIDTH), lambda n, s: (0, 0)), cast.in_spec],
        out_specs=[pl.BlockSpec((sb * ATTN_BLOCK, ATTN_WIDTH), lambda n, s: (n, 0)), cast.out_spec],
    )
    return pl.pallas_call(
        _attn_body,
        grid_spec=grid_spec,
        out_shape=[jax.ShapeDtypeStruct((T, ATTN_WIDTH), BF16), cast.out_shape],
        compiler_params=pltpu.CompilerParams(dimension_semantics=("parallel",),
                                             vmem_limit_bytes=VMEM_LIMIT),
        name="attention",
    )(sink, pa, pa, pa, pa, pa, pa, pa, norm_w, cast.array)


def _split3(x):
    hi = x.astype(BF16)
    r1 = x - hi.astype(F32)
    mid = r1.astype(BF16)
    lo = (r1 - mid.astype(F32)).astype(BF16)
    return hi, mid, lo


def _gla_body(reverse, final, *refs):
    if final:
        (q_ref, k_ref, v_ref, lr_ref, up_ref, bias_ref, tri_ref, of_ref, gr_ref, nw_ref, wcast_ref,
         o_ref, wcast_out_ref, st_ref) = refs
    else:
        (q_ref, k_ref, v_ref, lr_ref, up_ref, bias_ref, tri_ref, wcast_ref,
         o_ref, wcast_out_ref, st_ref) = refs
    C = GLA_CHUNK
    wcast_out_ref[...] = wcast_ref[...].astype(BF16)

    @pl.when(pl.program_id(0) == 0)
    def _():
        st_ref[...] = jnp.zeros_like(st_ref)

    lr = lr_ref[...]
    lr_hi = lr.astype(BF16)
    lr_lo = (lr - lr_hi.astype(F32)).astype(BF16)
    g = _dot(jnp.concatenate([lr_hi, lr_lo, lr_hi], axis=1), up_ref[...]) + bias_ref[...]
    la = jax.nn.log_sigmoid(g) * (1.0 / GLA_TAU)
    tri = tri_ref[...]
    hi, mid, lo = _split3(la)
    b = _dot(tri, hi) + _dot(tri, mid) + _dot(tri, lo)
    end = 0 if reverse else C - 1
    b_end = b[end:end + 1]
    b_mid = b[C // 2:C // 2 + 1]
    scale = GLA_DK ** -0.5
    q = q_ref[...].astype(F32) * scale
    k = k_ref[...].astype(F32)
    q_in = (q * jnp.exp(b - b_mid)).astype(BF16)
    k_in = (k * jnp.exp(b_mid - b)).astype(BF16)
    q_dec = (q * jnp.exp(b)).astype(BF16)
    k_end = (k * jnp.exp(b_end - b)).astype(BF16)
    decay = jnp.exp(b_end)
    ri = lax.broadcasted_iota(jnp.int32, (C, C), 0)
    ci = lax.broadcasted_iota(jnp.int32, (C, C), 1)
    causal = (ri <= ci) if reverse else (ri >= ci)
    heads = range(GLA_HEADS)
    ks = [slice(h * GLA_DK, (h + 1) * GLA_DK) for h in heads]
    vs = [slice(h * GLA_DV, (h + 1) * GLA_DV) for h in heads]
    att = [jnp.where(causal, _dot_nt(q_in[:, ks[h]], k_in[:, ks[h]]), 0.0).astype(BF16) for h in heads]
    st = [st_ref[h] for h in heads]
    inter = [_dot_nt(q_dec[:, ks[h]], st[h].astype(BF16)) for h in heads]
    o = [_dot(att[h], v_ref[:, vs[h]]) + inter[h] for h in heads]
    for h in heads:
        st_ref[h] = st[h] * decay[:, ks[h]] + _dot_tn(v_ref[:, vs[h]], k_end[:, ks[h]])
    if final:
        tot = [o[h] + of_ref[:, vs[h]] for h in heads]
        inv = [lax.rsqrt(jnp.mean(t * t, axis=-1, keepdims=True) + EPS) for t in tot]
        for h in heads:
            y = tot[h] * inv[h] * nw_ref[...]
            o_ref[:, vs[h]] = (y * jax.nn.silu(gr_ref[:, vs[h]].astype(F32))).astype(BF16)
    else:
        for h in heads:
            o_ref[:, vs[h]] = o[h]


def _gla_pass(reverse, pqk, pv, plr, up, bias, tri, cast, extra=None):
    T = pqk.shape[0]
    C = GLA_CHUNK
    n = T // C
    final = extra is not None
    blk = (lambda i: n - 1 - i) if reverse else (lambda i: i)
    row = lambda w, col=0: pl.BlockSpec((C, w), lambda i: (blk(i), col))
    in_specs = [row(GLA_QK_WIDTH, 0), row(GLA_QK_WIDTH, 1), row(GLA_WIDTH), row(2 * GLA_RANK),
                _resident(up.shape), _resident(bias.shape), _resident(tri.shape)]
    args = [pqk, pqk, pv, plr, up, bias, tri]
    if final:
        o_f, pr, norm_w = extra
        in_specs += [row(GLA_WIDTH), row(GLA_WIDTH), _resident(norm_w.shape)]
        args += [o_f, pr, norm_w]
    in_specs.append(cast.in_spec)
    args.append(cast.array)
    return pl.pallas_call(
        functools.partial(_gla_body, reverse, final),
        grid=(n,),
        in_specs=in_specs,
        out_specs=[row(GLA_WIDTH), cast.out_spec],
        out_shape=[jax.ShapeDtypeStruct((T, GLA_WIDTH), BF16 if final else F32), cast.out_shape],
        scratch_shapes=[pltpu.VMEM((GLA_HEADS, GLA_DV, GLA_DK), F32)],
        compiler_params=pltpu.CompilerParams(dimension_semantics=("arbitrary",),
                                             vmem_limit_bytes=VMEM_LIMIT),
        name="gla_bwd" if reverse else "gla_fwd",
    )(*args)


def _outproj_body(a_ref, g_ref, x_ref, wa_ref, wg_ref, ln_ref, wr_hl_ref, wr_hi_ref, rb_ref, tri_ref,
                  h_ref, xn_ref, ri_ref, rw_ref, cnt_ref):
    tm = x_ref.shape[0]

    @pl.when(pl.program_id(0) == 0)
    def _():
        cnt_ref[...] = jnp.zeros_like(cnt_ref)

    h = x_ref[...] + _dot(a_ref[...], wa_ref[...]) + _dot(g_ref[...], wg_ref[...])
    h_ref[...] = h
    xn = h * lax.rsqrt(jnp.mean(h * h, axis=-1, keepdims=True) + EPS) * ln_ref[...]
    x_hi = xn.astype(BF16)
    xn_ref[...] = _pack_bf16_pairs(x_hi)
    x_lo = (xn - x_hi.astype(F32)).astype(BF16)
    l2 = _dot(x_hi, wr_hl_ref[...])
    logits = l2[:, :ROUTER_ROWS] + l2[:, ROUTER_ROWS:] + _dot(x_lo, wr_hi_ref[...])
    lt = jnp.transpose(logits) + rb_ref[...]

    gl = lt[0:N_GROUPS]
    gmax = jnp.max(gl, axis=0, keepdims=True)
    gi = lax.broadcasted_iota(jnp.int32, gl.shape, 0).astype(F32)
    g_sel = jnp.min(jnp.where(gl == gmax, gi, float(N_GROUPS)), axis=0, keepdims=True)
    g_gate = 1.0 / jnp.sum(jnp.exp(gl - gmax), axis=0, keepdims=True)

    el = lt[EXPERT_ROW0:EXPERT_ROW0 + N_EXPERTS]
    ei_int = lax.broadcasted_iota(jnp.int32, el.shape, 0)
    ei = ei_int.astype(F32)
    grp = (ei_int >> 3).astype(F32)
    cand = jnp.where(grp == g_sel, el, -jnp.inf)
    v1 = jnp.max(cand, axis=0, keepdims=True)
    e1 = jnp.min(jnp.where(cand == v1, ei, float(N_EXPERTS)), axis=0, keepdims=True)
    cand2 = jnp.where(ei == e1, -jnp.inf, cand)
    v2 = jnp.max(cand2, axis=0, keepdims=True)
    e2 = jnp.min(jnp.where(cand2 == v2, ei, float(N_EXPERTS)), axis=0, keepdims=True)
    d = jnp.exp(v2 - v1)
    w1 = g_gate / (1.0 + d)
    w2 = g_gate * d / (1.0 + d)

    oh1 = (ei == e1).astype(F32)
    oh2 = (ei == e2).astype(F32)
    cnt = oh1 + oh2
    before = _dot(cnt.astype(BF16), tri_ref[...]) + cnt_ref[:, 0:1]
    r1 = jnp.sum(oh1 * before, axis=0, keepdims=True)
    r2 = jnp.sum(oh2 * before, axis=0, keepdims=True)
    cnt_ref[...] = cnt_ref[...] + jnp.sum(cnt, axis=1, keepdims=True)

    ri_ref[...] = jnp.concatenate([e1, e2, r1, r2, jnp.zeros((4, tm), F32)], axis=0).astype(jnp.int32)
    rw_ref[...] = jnp.concatenate([w1, w2, jnp.zeros((6, tm), F32)], axis=0)


def _outproj_router(attn, gla, x2, w_attn, w_gla, ln_w, wr_hl, wr_hi, rbias, tri):
    T = x2.shape[0]
    tm = OUT_ROWS
    row = lambda w: pl.BlockSpec((tm, w), lambda i: (i, 0))
    col = lambda r: pl.BlockSpec((r, tm), lambda i: (0, i))
    return pl.pallas_call(
        _outproj_body,
        grid=(T // tm,),
        in_specs=[row(ATTN_WIDTH), row(GLA_WIDTH), row(D_MODEL),
                  _resident(w_attn.shape), _resident(w_gla.shape), _resident(ln_w.shape),
                  _resident(wr_hl.shape), _resident(wr_hi.shape), _resident(rbias.shape),
                  _resident(tri.shape)],
        out_specs=[row(D_MODEL), row(D_MODEL // 2), col(8), col(8),
                   pl.BlockSpec((N_EXPERTS, 128), lambda i: (0, 0))],
        out_shape=[jax.ShapeDtypeStruct((T, D_MODEL), F32),
                   jax.ShapeDtypeStruct((T, D_MODEL // 2), jnp.uint32),
                   jax.ShapeDtypeStruct((8, T), jnp.int32),
                   jax.ShapeDtypeStruct((8, T), F32),
                   jax.ShapeDtypeStruct((N_EXPERTS, 128), F32)],
        compiler_params=pltpu.CompilerParams(dimension_semantics=("arbitrary",),
                                             vmem_limit_bytes=VMEM_LIMIT),
        name="outproj_router",
    )(attn, gla, x2, w_attn, w_gla, ln_w, wr_hl, wr_hi, rbias, tri)


def _row_copy_start(src, s, dst, d, sem, queue):
    pltpu.async_copy(src.at[pl.ds(s, 1)], dst.at[pl.ds(d, 1)], sem, priority=queue)


def _moe_body(be_ref, nu_ref, slot_ref, lo_ref, hi_ref, seq_ref, nxt_ref, xn_ref, wg_ref, wu_ref,
              wd_top_ref, wd_bot_ref, y_ref, xbuf, xb_ref, tok_ref, wg_buf, wu_buf, wd_top_buf, wd_bot_buf,
              sem, wsem):
    bm = MOE_ROWS
    b = pl.program_id(0)
    n_used = nu_ref[0]
    cur = b % MOE_ROW_BUFFERS
    expert = be_ref[b]
    wslot = seq_ref[expert] % 2
    first_of_expert = (b == 0) | (be_ref[jnp.maximum(b - 1, 0)] != expert)

    def weight_copies(e, slot):
        return [pltpu.make_async_copy(src.at[e], dst.at[slot], wsem.at[slot])
                for src, dst in ((wg_ref, wg_buf), (wu_ref, wu_buf),
                                 (wd_top_ref, wd_top_buf), (wd_bot_ref, wd_bot_buf))]

    def gather_start(block, buf, unrolled):
        def one(i, queue):
            _row_copy_start(xn_ref, tok_ref[block * bm + i], xbuf.at[buf], i, sem.at[buf], queue)
        if unrolled:
            for i in range(bm):
                one(i, i % 2)
        else:
            def pair(p, c):
                one(2 * p, 0)
                one(2 * p + 1, 1)
                return c
            lax.fori_loop(0, bm // 2, pair, 0)

    def gather_wait(buf):
        pltpu.make_async_copy(xn_ref.at[pl.ds(0, bm)], xbuf.at[buf], sem.at[buf]).wait()

    @pl.when(b == 0)
    def _():
        def per_expert(e, c):
            def zero(s, c2):
                tok_ref[s] = 0
                return c2
            return lax.fori_loop(lo_ref[e], hi_ref[e], zero, c)
        lax.fori_loop(0, N_EXPERTS, per_expert, 0)

        def fill(t, c):
            for k in range(TOP_K):
                tok_ref[slot_ref[TOP_K * t + k]] = t
            return c
        lax.fori_loop(0, slot_ref.shape[0] // TOP_K, fill, 0, unroll=8)
        gather_start(0, 0, False)

        @pl.when(n_used > 1)
        def _():
            gather_start(1, 1, False)

    @pl.when((b < n_used) & first_of_expert)
    def _():
        @pl.when(b == 0)
        def _():
            for c in weight_copies(expert, wslot):
                c.start()
        for c in weight_copies(expert, wslot):
            c.wait()
        following = nxt_ref[expert]

        @pl.when(following >= 0)
        def _():
            for c in weight_copies(following, 1 - wslot):
                c.start()

    def stage_rows():
        gather_wait(cur)
        lo, hi = _unpack_bf16_pairs(xbuf[cur])
        xb_ref[:, :D_MODEL // 2] = lo.astype(BF16)
        xb_ref[:, D_MODEL // 2:] = hi.astype(BF16)

    def experts():
        x = xb_ref[...]
        hid = jax.nn.silu(_dot(x, wg_buf[wslot])) * _dot(x, wu_buf[wslot])
        hid = hid.astype(BF16)
        half = D_FF // 2
        y = _dot(hid[:, :half], wd_top_buf[wslot]) + _dot(hid[:, half:], wd_bot_buf[wslot])
        y_ref[...] = _pack_bf16_pairs(y.astype(BF16))

    @pl.when(b + 2 < n_used)
    def _():
        stage_rows()
        gather_start(b + 2, (b + 2) % MOE_ROW_BUFFERS, True)
        experts()

    @pl.when((b < n_used) & (b + 2 >= n_used))
    def _():
        stage_rows()
        experts()

    @pl.when(b >= n_used)
    def _():
        y_ref[...] = jnp.zeros_like(y_ref)


def _moe(block_e, n_used, slots, pad_lo, pad_hi, expert_seq, expert_next, xn, wg, wu, wd_top, wd_bot, n_slots):
    bm = MOE_ROWS
    any_spec = pl.BlockSpec(memory_space=pl.ANY)
    grid_spec = pltpu.PrefetchScalarGridSpec(
        num_scalar_prefetch=7,
        grid=(n_slots // bm,),
        in_specs=[any_spec] * 5,
        out_specs=pl.BlockSpec((bm, D_MODEL // 2), lambda b, *_: (b, 0)),
        scratch_shapes=[pltpu.VMEM((MOE_ROW_BUFFERS, bm, D_MODEL // 2), jnp.uint32),
                        pltpu.VMEM((bm, D_MODEL), BF16),
                        pltpu.SMEM((n_slots,), jnp.int32),
                        pltpu.VMEM((2, D_MODEL, D_FF), BF16), pltpu.VMEM((2, D_MODEL, D_FF), BF16),
                        pltpu.VMEM((2, D_FF // 2, D_MODEL), BF16), pltpu.VMEM((2, D_FF // 2, D_MODEL), BF16),
                        pltpu.SemaphoreType.DMA((MOE_ROW_BUFFERS,)), pltpu.SemaphoreType.DMA((2,))],
    )
    return pl.pallas_call(
        _moe_body,
        grid_spec=grid_spec,
        out_shape=jax.ShapeDtypeStruct((n_slots, D_MODEL // 2), jnp.uint32),
        compiler_params=pltpu.CompilerParams(dimension_semantics=("arbitrary",),
                                             vmem_limit_bytes=VMEM_LIMIT),
        name="moe",
    )(block_e, n_used, slots, pad_lo, pad_hi, expert_seq, expert_next, xn, wg, wu, wd_top, wd_bot)


def _combine_body(slot_ref, h_ref, w_ref, ys_ref, o_ref, g_ref, sem):
    tb = COMBINE_ROWS
    i = pl.program_id(0)
    cur = i % 2

    def gather_start(block, buf):
        def one(t, c):
            for k in range(TOP_K):
                _row_copy_start(ys_ref, slot_ref[TOP_K * (block * tb + t) + k], g_ref.at[buf, k], t,
                                sem.at[buf], k)
            return c
        lax.fori_loop(0, tb, one, 0, unroll=8)

    @pl.when(i == 0)
    def _():
        gather_start(0, 0)

    @pl.when(i + 1 < pl.num_programs(0))
    def _():
        gather_start(i + 1, 1 - cur)

    for k in range(TOP_K):
        pltpu.make_async_copy(ys_ref.at[pl.ds(0, tb)], g_ref.at[cur, k], sem.at[cur]).wait()
    lo0, hi0 = _unpack_bf16_pairs(g_ref[cur, 0])
    lo1, hi1 = _unpack_bf16_pairs(g_ref[cur, 1])
    w0, w1 = w_ref[:, 0:1], w_ref[:, 1:2]
    half = D_MODEL // 2
    o_ref[:, :half] = h_ref[:, :half] + w0 * lo0 + w1 * lo1
    o_ref[:, half:] = h_ref[:, half:] + w0 * hi0 + w1 * hi1


def _combine(slots, h, wcol, ys):
    T = h.shape[0]
    tb = COMBINE_ROWS
    grid_spec = pltpu.PrefetchScalarGridSpec(
        num_scalar_prefetch=1,
        grid=(T // tb,),
        in_specs=[pl.BlockSpec((tb, D_MODEL), lambda i, s: (i, 0)),
                  pl.BlockSpec((tb, TOP_K), lambda i, s: (i, 0)),
                  pl.BlockSpec(memory_space=pl.ANY)],
        out_specs=pl.BlockSpec((tb, D_MODEL), lambda i, s: (i, 0)),
        scratch_shapes=[pltpu.VMEM((2, TOP_K, tb, D_MODEL // 2), jnp.uint32),
                        pltpu.SemaphoreType.DMA((2,))],
    )
    return pl.pallas_call(
        _combine_body,
        grid_spec=grid_spec,
        out_shape=jax.ShapeDtypeStruct((T, D_MODEL), F32),
        compiler_params=pltpu.CompilerParams(dimension_semantics=("arbitrary",),
                                             vmem_limit_bytes=VMEM_LIMIT),
        name="combine",
    )(slots, h, wcol, ys)


def _rope_tables(T, rows):
    inv_freq = jnp.power(jnp.float32(ROPE_THETA),
                         -jnp.arange(ROPE_HALF, dtype=F32) * (2.0 / ROPE_DIM))

    def tables(pos):
        ang = pos.astype(F32)[:, None] * inv_freq[None, :]
        rest = (pos.shape[0], HEAD_DIM - ROPE_DIM)
        cos = jnp.concatenate([jnp.cos(ang), jnp.cos(ang), jnp.ones(rest, F32)], axis=1)
        sin = jnp.concatenate([jnp.sin(ang), jnp.sin(ang), jnp.zeros(rest, F32)], axis=1)
        return cos, sin

    step_cos, step_sin = tables(jnp.arange(T // rows) * rows)
    row_cos, row_sin = tables(jnp.arange(rows))
    return step_cos[:, None, :], step_sin[:, None, :], row_cos, row_sin


def _split_hi_lo(w):
    hi = w.astype(BF16)
    lo = (w - hi.astype(F32)).astype(BF16)
    return hi, lo


def _layer(x2, ln1_w, w_in, q_norm_w, k_norm_w, attn_sink, attn_out_norm_w, gate_up_f, gate_bias_f,
           gate_up_b, gate_bias_b, gla_out_norm_w, w_out, ln2_w, w_group, b_group, w_router, b_router,
           w_gate_e, w_up_e, w_down_e):
    T = x2.shape[0]
    row = lambda v: v.reshape(1, -1).astype(F32)

    c0, c1, c2, c3 = A_WIDTH, A_WIDTH + 2 * GLA_QK_WIDTH, A_WIDTH + 2 * GLA_QK_WIDTH + GLA_WIDTH, \
        A_WIDTH + 2 * GLA_QK_WIDTH + 2 * GLA_WIDTH
    wa, wqk, wv, wr, wl = (w_in[:, a:b].astype(BF16) for a, b in
                           ((0, c0), (c0, c1), (c1, c2), (c2, c3), (c3, w_in.shape[1])))
    pa, pqk, pv, pr, plr, wg_b = _inproj(x2, row(ln1_w), wa, wqk, wv, wr, wl, row(q_norm_w), row(k_norm_w),
                                         _rope_tables(T, PROJ_ROWS), _cast_rows_job(w_gate_e, T // PROJ_ROWS))

    attn, wu_b = _attention(pa, attn_sink.astype(F32), row(attn_out_norm_w),
                            _cast_rows_job(w_up_e, T // (ATTN_STEP_BLOCKS * ATTN_BLOCK)))

    C = GLA_CHUNK
    ones = jnp.ones((C, C), F32)
    zr = jnp.zeros((GLA_RANK, GLA_QK_WIDTH), F32)
    up_f = jnp.concatenate([gate_up_f.astype(F32), zr], axis=0)
    up_b = jnp.concatenate([zr, gate_up_b.astype(F32)], axis=0)

    def up_pieces(up):
        hi, lo = _split_hi_lo(up)
        return jnp.concatenate([hi, hi, lo], axis=0)

    n_chunks = T // C
    o_f, wd_top = _gla_pass(False, pqk, pv, plr, up_pieces(up_f), row(gate_bias_f),
                            jnp.tril(ones).astype(BF16), _cast_half_rows_job(w_down_e, 0, n_chunks))
    gla, wd_bot = _gla_pass(True, pqk, pv, plr, up_pieces(up_b), row(gate_bias_b),
                            jnp.triu(ones).astype(BF16), _cast_half_rows_job(w_down_e, 1, n_chunks),
                            extra=(o_f, pr, row(gla_out_norm_w)))

    wr_full = jnp.zeros((D_MODEL, ROUTER_ROWS), F32)
    wr_full = wr_full.at[:, :N_GROUPS].set(w_group.astype(F32))
    wr_full = wr_full.at[:, EXPERT_ROW0:EXPERT_ROW0 + N_EXPERTS].set(w_router.astype(F32))
    wr_hi, wr_lo = _split_hi_lo(wr_full)
    rbias = jnp.zeros((ROUTER_ROWS, 1), F32)
    rbias = rbias.at[:N_GROUPS, 0].set(b_group.astype(F32))
    rbias = rbias.at[EXPERT_ROW0:EXPERT_ROW0 + N_EXPERTS, 0].set(b_router.astype(F32))
    tm = OUT_ROWS
    earlier = jnp.triu(jnp.ones((tm, tm), F32), k=1).astype(BF16)
    h, xn, r_int, r_w, counts = _outproj_router(
        attn, gla, x2, w_out[:ATTN_WIDTH].astype(BF16), w_out[ATTN_WIDTH:].astype(BF16), row(ln2_w),
        jnp.concatenate([wr_hi, wr_lo], axis=1), wr_hi, rbias, earlier)

    bm = MOE_ROWS
    n_blocks = (T * TOP_K) // bm + N_EXPERTS
    cnt = counts[:, 0].astype(jnp.int32)
    padded = (cnt + bm - 1) // bm * bm
    pad_end = jnp.cumsum(padded)
    pad_start = pad_end - padded
    n_used = (pad_end[-1] // bm).astype(jnp.int32).reshape(1)
    block_e = jnp.minimum(jnp.sum(pad_end[None, :] <= (jnp.arange(n_blocks) * bm)[:, None], axis=1),
                          N_EXPERTS - 1).astype(jnp.int32)
    e_sel = r_int[0:TOP_K].T
    start_sel = jnp.sum(jnp.where(e_sel[..., None] == jnp.arange(N_EXPERTS), pad_start, 0), axis=-1)
    slots = (start_sel + r_int[TOP_K:2 * TOP_K].T).reshape(-1).astype(jnp.int32)

    nonempty = cnt > 0
    expert_seq = (jnp.cumsum(nonempty) - 1).astype(jnp.int32)
    ids = jnp.arange(N_EXPERTS)
    later = (ids[None, :] > ids[:, None]) & nonempty[None, :]
    expert_next = jnp.where(later.any(axis=1), jnp.argmax(later, axis=1), -1).astype(jnp.int32)
    ys = _moe(block_e, n_used, slots, (pad_start + cnt).astype(jnp.int32), pad_end.astype(jnp.int32),
              expert_seq, expert_next, xn,
              wg_b.reshape(w_gate_e.shape), wu_b.reshape(w_up_e.shape), wd_top, wd_bot, n_blocks * bm)
    return _combine(slots, h, r_w[0:TOP_K].T, ys)


def kernel(x, ln1_w, w_in, q_norm_w, k_norm_w, attn_sink, attn_out_norm_w, gla_gate_up_f, gla_gate_bias_f,
           gla_gate_up_b, gla_gate_bias_b, gla_out_norm_w, w_out, ln2_w, w_group, b_group, w_router,
           b_router, w_gate_e, w_up_e, w_down_e):
    B, S, D = x.shape
    h = x.reshape(B * S, D)
    assert B == 1
    for l in range(ln1_w.shape[0]):
        h = _layer(h, ln1_w[l], w_in[l], q_norm_w[l], k_norm_w[l], attn_sink[l], attn_out_norm_w[l],
                   gla_gate_up_f[l], gla_gate_bias_f[l], gla_gate_up_b[l], gla_gate_bias_b[l],
                   gla_out_norm_w[l], w_out[l], ln2_w[l], w_group[l], b_group[l], w_router[l],
                   b_router[l], w_gate_e[l], w_up_e[l], w_down_e[l])
    return h.reshape(B, S, D)
```

```python
import functools
from typing import NamedTuple

import jax
import jax.numpy as jnp
from jax import lax
from jax.experimental import pallas as pl
from jax.experimental.pallas import tpu as pltpu

F32 = jnp.float32
BF16 = jnp.bfloat16

EPS = 1e-6
D_MODEL = 2048

ATTN_Q_HEADS = 8
ATTN_KV_HEADS = 2
ATTN_GROUP = ATTN_Q_HEADS // ATTN_KV_HEADS
HEAD_DIM = 128
WINDOW = 128
ATTN_BLOCK = 128
ATTN_STEP_BLOCKS = 4
ROPE_THETA = 500000.0
ROPE_DIM = HEAD_DIM // 4
ROPE_HALF = ROPE_DIM // 2
ATTN_WIDTH = ATTN_Q_HEADS * HEAD_DIM
KV_WIDTH = ATTN_KV_HEADS * HEAD_DIM
A_WIDTH = ATTN_WIDTH + 2 * KV_WIDTH

GLA_HEADS = 4
GLA_DK = 128
GLA_DV = 256
GLA_RANK = 16
GLA_TAU = 16.0
GLA_QK_WIDTH = GLA_HEADS * GLA_DK
GLA_WIDTH = GLA_HEADS * GLA_DV
GLA_CHUNK = 128
GLA_STEP_CHUNKS = 4

N_GROUPS = 4
EXPERTS_PER_GROUP = 8
N_EXPERTS = N_GROUPS * EXPERTS_PER_GROUP
TOP_K = 2
D_FF = 1024
ROUTER_ROWS = 128
EXPERT_ROW0 = 8

PROJ_ROWS = 256
OUT_ROWS = 256
MOE_ROWS = 256
MOE_ROW_BUFFERS = 3
MOE_CAST_ROWS = 128
COMBINE_ROWS = 256

VMEM_LIMIT = 56 * 1024 * 1024


def _dot(a, b):
    return jnp.dot(a, b, preferred_element_type=F32)


def _dot_nt(a, b):
    return lax.dot_general(a, b, (((1,), (1,)), ((), ())), preferred_element_type=F32)


def _dot_tn(a, b):
    return lax.dot_general(a, b, (((0,), (0,)), ((), ())), preferred_element_type=F32)


def _pack_bf16_pairs(x):
    n = x.shape[1] // 2
    lo = lax.bitcast_convert_type(x[:, :n].astype(F32), jnp.uint32) >> 16
    hi = lax.bitcast_convert_type(x[:, n:].astype(F32), jnp.uint32) & jnp.uint32(0xFFFF0000)
    return lo | hi


def _unpack_bf16_pairs(w):
    lo = lax.bitcast_convert_type(w << 16, F32)
    hi = lax.bitcast_convert_type(w & jnp.uint32(0xFFFF0000), F32)
    return lo, hi


class CastJob(NamedTuple):
    array: jax.Array
    in_spec: pl.BlockSpec
    out_spec: pl.BlockSpec
    out_shape: jax.ShapeDtypeStruct


def _cast_rows_job(w, n_steps):
    w2 = w.astype(F32).reshape(-1, w.shape[-1])
    rows = w2.shape[0] // n_steps
    assert rows * n_steps == w2.shape[0] and rows % 16 == 0
    spec = pl.BlockSpec((rows, w2.shape[1]), lambda i, *_: (i, 0))
    return CastJob(w2, spec, spec, jax.ShapeDtypeStruct(w2.shape, BF16))


def _resident(shape):
    nd = len(shape)
    return pl.BlockSpec(shape, lambda *_: (0,) * nd, pipeline_mode=pl.Buffered(1))


def _inproj_body(x_ref, ln_ref, wa_ref, wqk_ref, wv_ref, wr_ref, wl_ref, qn_ref, kn_ref,
                 step_cos_ref, step_sin_ref, row_cos_ref, row_sin_ref, wcast_ref,
                 oa_ref, oqk_ref, ov_ref, or_ref, ol_ref, wcast_out_ref):
    wcast_out_ref[...] = wcast_ref[...].astype(BF16)
    x = x_ref[...]
    ms = jnp.mean(x * x, axis=-1, keepdims=True)
    xn = (x * lax.rsqrt(ms + EPS) * ln_ref[...]).astype(BF16)
    acc = _dot(xn, wa_ref[...])
    ca, sn = step_cos_ref[0], step_sin_ref[0]
    cb, sb_ = row_cos_ref[...], row_sin_ref[...]
    cos = cb * ca - sb_ * sn
    sin = sb_ * ca + cb * sn
    lane = lax.broadcasted_iota(jnp.int32, (1, HEAD_DIM), 1)
    sa = jnp.where((lane >= ROPE_HALF) & (lane < ROPE_DIM), sin, 0.0)
    sb = jnp.where(lane < ROPE_HALF, -sin, 0.0)
    for c in range(ATTN_Q_HEADS + ATTN_KV_HEADS):
        xh = acc[:, c * HEAD_DIM:(c + 1) * HEAD_DIM]
        w = qn_ref[...] if c < ATTN_Q_HEADS else kn_ref[...]
        y = xh * lax.rsqrt(jnp.mean(xh * xh, axis=-1, keepdims=True) + EPS) * w
        y = (y * cos + pltpu.roll(y, ROPE_HALF, 1) * sa
             + pltpu.roll(y, HEAD_DIM - ROPE_HALF, 1) * sb)
        if c < ATTN_Q_HEADS:
            y = y * (HEAD_DIM ** -0.5)
        oa_ref[:, c * HEAD_DIM:(c + 1) * HEAD_DIM] = y.astype(BF16)
    oa_ref[:, ATTN_WIDTH + KV_WIDTH:] = acc[:, ATTN_WIDTH + KV_WIDTH:].astype(BF16)
    oqk_ref[...] = _dot(xn, wqk_ref[...]).astype(BF16)
    ov_ref[...] = _dot(xn, wv_ref[...]).astype(BF16)
    or_ref[...] = _dot(xn, wr_ref[...]).astype(BF16)
    ol_ref[...] = _dot(xn, wl_ref[...])


def _inproj(x2, ln_w, wa, wqk, wv, wr, wl, qn, kn, rope, cast):
    T = x2.shape[0]
    tm = PROJ_ROWS
    row = lambda w: pl.BlockSpec((tm, w), lambda i: (i, 0))
    step_row = pl.BlockSpec((1, 1, HEAD_DIM), lambda i: (i, 0, 0))
    return pl.pallas_call(
        _inproj_body,
        grid=(T // tm,),
        in_specs=[row(D_MODEL), _resident((1, D_MODEL)),
                  _resident(wa.shape), _resident(wqk.shape), _resident(wv.shape),
                  _resident(wr.shape), _resident(wl.shape),
                  _resident((1, HEAD_DIM)), _resident((1, HEAD_DIM)),
                  step_row, step_row, _resident((tm, HEAD_DIM)), _resident((tm, HEAD_DIM)), cast.in_spec],
        out_specs=[row(A_WIDTH), row(2 * GLA_QK_WIDTH), row(GLA_WIDTH), row(GLA_WIDTH),
                   row(2 * GLA_RANK), cast.out_spec],
        out_shape=[jax.ShapeDtypeStruct((T, A_WIDTH), BF16),
                   jax.ShapeDtypeStruct((T, 2 * GLA_QK_WIDTH), BF16),
                   jax.ShapeDtypeStruct((T, GLA_WIDTH), BF16),
                   jax.ShapeDtypeStruct((T, GLA_WIDTH), BF16),
                   jax.ShapeDtypeStruct((T, 2 * GLA_RANK), F32), cast.out_shape],
        compiler_params=pltpu.CompilerParams(dimension_semantics=("parallel",),
                                             vmem_limit_bytes=VMEM_LIMIT),
        name="inproj",
    )(x2, ln_w, wa, wqk, wv, wr, wl, qn, kn, *rope, cast.array)


def _attn_body(sink_ref, q_ref, kp_ref, kc_ref, kn_ref, vp_ref, vc_ref, vn_ref, nw_ref, wcast_ref,
               o_ref, wcast_out_ref):
    wcast_out_ref[...] = wcast_ref[...].astype(BF16)
    n = pl.program_id(0)
    nb = pl.num_programs(0)
    rows = ATTN_GROUP * ATTN_BLOCK
    keys = 3 * ATTN_BLOCK
    r = lax.broadcasted_iota(jnp.int32, (rows, keys), 0) & (ATTN_BLOCK - 1)
    c = lax.broadcasted_iota(jnp.int32, (rows, keys), 1)
    band = (c >= r + (ATTN_BLOCK - WINDOW)) & (c <= r + (ATTN_BLOCK + WINDOW))
    edge_mask = [band & ((c >= ATTN_BLOCK) | (n > 0))] + [band] * (ATTN_STEP_BLOCKS - 2) + \
                [band & ((c < 2 * ATTN_BLOCK) | (n < nb - 1))]

    def window(j, prev_ref, cur_ref, next_ref, ks):
        blocks = [prev_ref[:, ks]] + [cur_ref[i * ATTN_BLOCK:(i + 1) * ATTN_BLOCK, ks]
                                      for i in range(ATTN_STEP_BLOCKS)] + [next_ref[:, ks]]
        return jnp.concatenate(blocks[j:j + 3], axis=0)

    chains = [(j, g) for j in range(ATTN_STEP_BLOCKS) for g in range(ATTN_KV_HEADS)]
    heads_of = lambda g: range(g * ATTN_GROUP, (g + 1) * ATTN_GROUP)
    ks_of = lambda g: slice(g * HEAD_DIM, (g + 1) * HEAD_DIM)
    qrows_of = lambda j: slice(j * ATTN_BLOCK, (j + 1) * ATTN_BLOCK)
    s = [jnp.where(edge_mask[j],
                   _dot_nt(jnp.concatenate([q_ref[qrows_of(j), h * HEAD_DIM:(h + 1) * HEAD_DIM]
                                            for h in heads_of(g)], axis=0),
                           window(j, kp_ref, kc_ref, kn_ref, ks_of(g))), -jnp.inf)
         for j, g in chains]
    sink = [jnp.concatenate([jnp.full((ATTN_BLOCK, 1), sink_ref[h], F32) for h in heads_of(g)], axis=0)
            for j, g in chains]
    m = [jnp.maximum(jnp.max(si, axis=-1, keepdims=True), sk) for si, sk in zip(s, sink)]
    e = [jnp.exp(si - mi) for si, mi in zip(s, m)]
    denom = [jnp.sum(ei, axis=-1, keepdims=True) + jnp.exp(sk - mi) for ei, sk, mi in zip(e, sink, m)]
    o = [_dot(ei.astype(BF16), window(j, vp_ref, vc_ref, vn_ref, ks_of(g))) / di
         for (j, g), ei, di in zip(chains, e, denom)]
    for j in range(ATTN_STEP_BLOCKS):
        outs = [o[chains.index((j, g))][i * ATTN_BLOCK:(i + 1) * ATTN_BLOCK]
                for g in range(ATTN_KV_HEADS) for i in range(ATTN_GROUP)]
        a = jnp.concatenate(outs, axis=1)
        a = a * lax.rsqrt(jnp.mean(a * a, axis=-1, keepdims=True) + EPS) * nw_ref[...]
        o_ref[qrows_of(j), :] = a.astype(BF16)


def _attention(pa, sink, norm_w, cast):
    T = pa.shape[0]
    sb = ATTN_STEP_BLOCKS
    nb = T // ATTN_BLOCK
    kcol = ATTN_WIDTH // KV_WIDTH
    vcol = kcol + 1
    prev = lambda col: pl.BlockSpec((ATTN_BLOCK, KV_WIDTH), lambda n, s: (jnp.maximum(sb * n - 1, 0), col))
    nxt = lambda col: pl.BlockSpec((ATTN_BLOCK, KV_WIDTH),
                                   lambda n, s: (jnp.minimum(sb * n + sb, nb - 1), col))
    cur = lambda col: pl.BlockSpec((sb * ATTN_BLOCK, KV_WIDTH), lambda n, s: (n, col))
    grid_spec = pltpu.PrefetchScalarGridSpec(
        num_scalar_prefetch=1,
        grid=(nb // sb,),
        in_specs=[pl.BlockSpec((sb * ATTN_BLOCK, ATTN_WIDTH), lambda n, s: (n, 0)),
                  prev(kcol), cur(kcol), nxt(kcol), prev(vcol), cur(vcol), nxt(vcol),
                  pl.BlockSpec((1, ATTN_WIDTH), lambda n, s: (0, 0)), cast.in_spec],
        out_specs=[pl.BlockSpec((sb * ATTN_BLOCK, ATTN_WIDTH), lambda n, s: (n, 0)), cast.out_spec],
    )
    return pl.pallas_call(
        _attn_body,
        grid_spec=grid_spec,
        out_shape=[jax.ShapeDtypeStruct((T, ATTN_WIDTH), BF16), cast.out_shape],
        compiler_params=pltpu.CompilerParams(dimension_semantics=("parallel",),
                                             vmem_limit_bytes=VMEM_LIMIT),
        name="attention",
    )(sink, pa, pa, pa, pa, pa, pa, pa, norm_w, cast.array)


def _gla_body(reverse, final, *refs):
    if final:
        (q_ref, k_ref, v_ref, lr_ref, up_ref, bias_ref, tri_ref, of_ref, gr_ref, nw_ref,
         o_ref, st_ref) = refs
    else:
        q_ref, k_ref, v_ref, lr_ref, up_ref, bias_ref, tri_ref, o_ref, st_ref = refs
    C = GLA_CHUNK
    R = GLA_STEP_CHUNKS

    @pl.when(pl.program_id(0) == 0)
    def _():
        st_ref[...] = jnp.zeros_like(st_ref)

    chunks = range(R)
    order = tuple(reversed(chunks)) if reverse else tuple(chunks)
    rows = [slice(s * C, (s + 1) * C) for s in chunks]
    heads = range(GLA_HEADS)
    ks = [slice(h * GLA_DK, (h + 1) * GLA_DK) for h in heads]
    vs = [slice(h * GLA_DV, (h + 1) * GLA_DV) for h in heads]

    lr = lr_ref[...]
    lr_hi = lr.astype(BF16)
    lr_lo = (lr - lr_hi.astype(F32)).astype(BF16)
    g = _dot(jnp.concatenate([lr_hi, lr_lo, lr_hi], axis=1), up_ref[...]) + bias_ref[...]
    la = jax.nn.log_sigmoid(g) * (1.0 / GLA_TAU)
    tri = tri_ref[...]
    hi, lo = _split_hi_lo(la)
    b = [_dot(tri, hi[rows[s]]) + _dot(tri, lo[rows[s]]) for s in chunks]
    end = 0 if reverse else C - 1
    b_end = [b[s][end:end + 1] for s in chunks]
    b_mid = [b[s][C // 2:C // 2 + 1] for s in chunks]
    scale = GLA_DK ** -0.5
    q = [q_ref[rows[s], :].astype(F32) * scale for s in chunks]
    k = [k_ref[rows[s], :].astype(F32) for s in chunks]
    q_in = [(q[s] * jnp.exp(b[s] - b_mid[s])).astype(BF16) for s in chunks]
    k_in = [(k[s] * jnp.exp(b_mid[s] - b[s])).astype(BF16) for s in chunks]
    q_dec = [(q[s] * jnp.exp(b[s])).astype(BF16) for s in chunks]
    k_end = [(k[s] * jnp.exp(b_end[s] - b[s])).astype(BF16) for s in chunks]
    decay = [jnp.exp(b_end[s]) for s in chunks]
    ri = lax.broadcasted_iota(jnp.int32, (C, C), 0)
    ci = lax.broadcasted_iota(jnp.int32, (C, C), 1)
    causal = (ri <= ci) if reverse else (ri >= ci)
    att = [[jnp.where(causal, _dot_nt(q_in[s][:, ks[h]], k_in[s][:, ks[h]]), 0.0).astype(BF16)
            for h in heads] for s in chunks]
    intra = [[_dot(att[s][h], v_ref[rows[s], vs[h]]) for h in heads] for s in chunks]
    gain = [[_dot_tn(v_ref[rows[s], vs[h]], k_end[s][:, ks[h]]) for h in heads] for s in chunks]
    st = [st_ref[h] for h in heads]
    o = [[None] * GLA_HEADS for _ in chunks]
    for s in order:
        for h in heads:
            o[s][h] = intra[s][h] + _dot_nt(q_dec[s][:, ks[h]], st[h].astype(BF16))
            st[h] = st[h] * decay[s][:, ks[h]] + gain[s][h]
    for h in heads:
        st_ref[h] = st[h]
    for s in chunks:
        if final:
            tot = [o[s][h] + of_ref[rows[s], vs[h]] for h in heads]
            inv = [lax.rsqrt(jnp.mean(t * t, axis=-1, keepdims=True) + EPS) for t in tot]
            for h in heads:
                y = tot[h] * inv[h] * nw_ref[...]
                o_ref[rows[s], vs[h]] = (y * jax.nn.silu(gr_ref[rows[s], vs[h]].astype(F32))).astype(BF16)
        else:
            for h in heads:
                o_ref[rows[s], vs[h]] = o[s][h]


def _gla_pass(reverse, pqk, pv, plr, up, bias, tri, extra=None):
    T = pqk.shape[0]
    rows = GLA_STEP_CHUNKS * GLA_CHUNK
    n = T // rows
    final = extra is not None
    blk = (lambda i: n - 1 - i) if reverse else (lambda i: i)
    row = lambda w, col=0: pl.BlockSpec((rows, w), lambda i: (blk(i), col))
    in_specs = [row(GLA_QK_WIDTH, 0), row(GLA_QK_WIDTH, 1), row(GLA_WIDTH), row(2 * GLA_RANK),
                _resident(up.shape), _resident(bias.shape), _resident(tri.shape)]
    args = [pqk, pqk, pv, plr, up, bias, tri]
    if final:
        o_f, pr, norm_w = extra
        in_specs += [row(GLA_WIDTH), row(GLA_WIDTH), _resident(norm_w.shape)]
        args += [o_f, pr, norm_w]
    return pl.pallas_call(
        functools.partial(_gla_body, reverse, final),
        grid=(n,),
        in_specs=in_specs,
        out_specs=row(GLA_WIDTH),
        out_shape=jax.ShapeDtypeStruct((T, GLA_WIDTH), BF16 if final else F32),
        scratch_shapes=[pltpu.VMEM((GLA_HEADS, GLA_DV, GLA_DK), F32)],
        compiler_params=pltpu.CompilerParams(dimension_semantics=("arbitrary",),
                                             vmem_limit_bytes=VMEM_LIMIT),
        name="gla_bwd" if reverse else "gla_fwd",
    )(*args)


def _outproj_body(a_ref, g_ref, x_ref, w_ref, ln_ref, wr_hl_ref, wr_hi_ref, rb_ref, tri_ref,
                  h_ref, xn_ref, ri_ref, rw_ref, cnt_ref):
    tm = x_ref.shape[0]

    @pl.when(pl.program_id(0) == 0)
    def _():
        cnt_ref[...] = jnp.zeros_like(cnt_ref)

    h = x_ref[...] + _dot(jnp.concatenate([a_ref[...], g_ref[...]], axis=1), w_ref[...])
    h_ref[...] = h
    xn = h * lax.rsqrt(jnp.mean(h * h, axis=-1, keepdims=True) + EPS) * ln_ref[...]
    x_hi = xn.astype(BF16)
    xn_ref[...] = _pack_bf16_pairs(x_hi)
    x_lo = (xn - x_hi.astype(F32)).astype(BF16)
    l2 = _dot(x_hi, wr_hl_ref[...])
    logits = l2[:, :ROUTER_ROWS] + l2[:, ROUTER_ROWS:] + _dot(x_lo, wr_hi_ref[...])
    lt = jnp.transpose(logits) + rb_ref[...]

    gl = lt[0:N_GROUPS]
    gmax = jnp.max(gl, axis=0, keepdims=True)
    gi = lax.broadcasted_iota(jnp.int32, gl.shape, 0).astype(F32)
    g_sel = jnp.min(jnp.where(gl == gmax, gi, float(N_GROUPS)), axis=0, keepdims=True)
    g_gate = 1.0 / jnp.sum(jnp.exp(gl - gmax), axis=0, keepdims=True)

    el = lt[EXPERT_ROW0:EXPERT_ROW0 + N_EXPERTS]
    ei_int = lax.broadcasted_iota(jnp.int32, el.shape, 0)
    ei = ei_int.astype(F32)
    grp = (ei_int >> 3).astype(F32)
    cand = jnp.where(grp == g_sel, el, -jnp.inf)
    v1 = jnp.max(cand, axis=0, keepdims=True)
    e1 = jnp.min(jnp.where(cand == v1, ei, float(N_EXPERTS)), axis=0, keepdims=True)
    cand2 = jnp.where(ei == e1, -jnp.inf, cand)
    v2 = jnp.max(cand2, axis=0, keepdims=True)
    e2 = jnp.min(jnp.where(cand2 == v2, ei, float(N_EXPERTS)), axis=0, keepdims=True)
    d = jnp.exp(v2 - v1)
    w1 = g_gate / (1.0 + d)
    w2 = g_gate * d / (1.0 + d)

    oh1 = (ei == e1).astype(F32)
    oh2 = (ei == e2).astype(F32)
    cnt = oh1 + oh2
    before = _dot(cnt.astype(BF16), tri_ref[...]) + cnt_ref[:, 0:1]
    r1 = jnp.sum(oh1 * before, axis=0, keepdims=True)
    r2 = jnp.sum(oh2 * before, axis=0, keepdims=True)
    cnt_ref[...] = cnt_ref[...] + jnp.sum(cnt, axis=1, keepdims=True)

    ri_ref[...] = jnp.concatenate([e1, e2, r1, r2, jnp.zeros((4, tm), F32)], axis=0).astype(jnp.int32)
    rw_ref[...] = jnp.concatenate([w1, w2, jnp.zeros((6, tm), F32)], axis=0)


def _outproj_router(attn, gla, x2, w_out, ln_w, wr_hl, wr_hi, rbias, tri):
    T = x2.shape[0]
    tm = OUT_ROWS
    row = lambda w: pl.BlockSpec((tm, w), lambda i: (i, 0))
    col = lambda r: pl.BlockSpec((r, tm), lambda i: (0, i))
    return pl.pallas_call(
        _outproj_body,
        grid=(T // tm,),
        in_specs=[row(ATTN_WIDTH), row(GLA_WIDTH), row(D_MODEL),
                  _resident(w_out.shape), _resident(ln_w.shape),
                  _resident(wr_hl.shape), _resident(wr_hi.shape), _resident(rbias.shape),
                  _resident(tri.shape)],
        out_specs=[row(D_MODEL), row(D_MODEL // 2), col(8), col(8),
                   pl.BlockSpec((N_EXPERTS, 128), lambda i: (0, 0))],
        out_shape=[jax.ShapeDtypeStruct((T, D_MODEL), F32),
                   jax.ShapeDtypeStruct((T, D_MODEL // 2), jnp.uint32),
                   jax.ShapeDtypeStruct((8, T), jnp.int32),
                   jax.ShapeDtypeStruct((8, T), F32),
                   jax.ShapeDtypeStruct((N_EXPERTS, 128), F32)],
        compiler_params=pltpu.CompilerParams(dimension_semantics=("arbitrary",),
                                             vmem_limit_bytes=VMEM_LIMIT),
        name="outproj_router",
    )(attn, gla, x2, w_out, ln_w, wr_hl, wr_hi, rbias, tri)


def _row_copy_start(src, s, dst, d, sem, queue):
    pltpu.async_copy(src.at[pl.ds(s, 1)], dst.at[pl.ds(d, 1)], sem, priority=queue)


def _moe_body(be_ref, nu_ref, slot_ref, lo_ref, hi_ref, seq_ref, nxt_ref, xn_ref, wg_ref, wu_ref, wd_ref,
              y_ref, xbuf, xb_ref, tok_ref, wg_buf, wu_buf, wd_buf, wd_stage, sem, wsem, dsem):
    bm = MOE_ROWS
    b = pl.program_id(0)
    n_used = nu_ref[0]
    cur = b % MOE_ROW_BUFFERS
    expert = be_ref[b]
    wslot = seq_ref[expert] % 2
    first_of_expert = (b == 0) | (be_ref[jnp.maximum(b - 1, 0)] != expert)

    def weight_copies(e, slot):
        return [pltpu.make_async_copy(wg_ref.at[e], wg_buf.at[slot], wsem.at[slot]),
                pltpu.make_async_copy(wu_ref.at[e], wu_buf.at[slot], wsem.at[slot]),
                pltpu.make_async_copy(wd_ref.at[e], wd_stage, dsem)]

    def gather_start(block, buf, unrolled):
        def one(i, queue):
            _row_copy_start(xn_ref, tok_ref[block * bm + i], xbuf.at[buf], i, sem.at[buf], queue)
        if unrolled:
            for i in range(bm):
                one(i, i % 2)
        else:
            def pair(p, c):
                one(2 * p, 0)
                one(2 * p + 1, 1)
                return c
            lax.fori_loop(0, bm // 2, pair, 0)

    def gather_wait(buf):
        pltpu.make_async_copy(xn_ref.at[pl.ds(0, bm)], xbuf.at[buf], sem.at[buf]).wait()

    @pl.when(b == 0)
    def _():
        def per_expert(e, c):
            def zero(s, c2):
                tok_ref[s] = 0
                return c2
            return lax.fori_loop(lo_ref[e], hi_ref[e], zero, c)
        lax.fori_loop(0, N_EXPERTS, per_expert, 0)

        def fill(t, c):
            for k in range(TOP_K):
                tok_ref[slot_ref[TOP_K * t + k]] = t
            return c
        lax.fori_loop(0, slot_ref.shape[0] // TOP_K, fill, 0, unroll=8)
        gather_start(0, 0, False)

        @pl.when(n_used > 1)
        def _():
            gather_start(1, 1, False)

    @pl.when((b < n_used) & first_of_expert)
    def _():
        @pl.when(b == 0)
        def _():
            for c in weight_copies(expert, wslot):
                c.start()
        for c in weight_copies(expert, wslot):
            c.wait()
        for r in range(0, D_FF, MOE_CAST_ROWS):
            wd_buf[wslot, r:r + MOE_CAST_ROWS, :] = wd_stage[r:r + MOE_CAST_ROWS, :].astype(BF16)
        following = nxt_ref[expert]

        @pl.when(following >= 0)
        def _():
            for c in weight_copies(following, 1 - wslot):
                c.start()

    def stage_rows():
        gather_wait(cur)
        lo, hi = _unpack_bf16_pairs(xbuf[cur])
        xb_ref[:, :D_MODEL // 2] = lo.astype(BF16)
        xb_ref[:, D_MODEL // 2:] = hi.astype(BF16)

    def experts():
        x = xb_ref[...]
        hid = jax.nn.silu(_dot(x, wg_buf[wslot])) * _dot(x, wu_buf[wslot])
        y = _dot(hid.astype(BF16), wd_buf[wslot])
        y_ref[...] = _pack_bf16_pairs(y.astype(BF16))

    @pl.when(b + 2 < n_used)
    def _():
        stage_rows()
        gather_start(b + 2, (b + 2) % MOE_ROW_BUFFERS, True)
        experts()

    @pl.when((b < n_used) & (b + 2 >= n_used))
    def _():
        stage_rows()
        experts()

    @pl.when(b >= n_used)
    def _():
        y_ref[...] = jnp.zeros_like(y_ref)


def _moe(block_e, n_used, slots, pad_lo, pad_hi, expert_seq, expert_next, xn, wg, wu, wd, n_slots):
    bm = MOE_ROWS
    any_spec = pl.BlockSpec(memory_space=pl.ANY)
    grid_spec = pltpu.PrefetchScalarGridSpec(
        num_scalar_prefetch=7,
        grid=(n_slots // bm,),
        in_specs=[any_spec] * 4,
        out_specs=pl.BlockSpec((bm, D_MODEL // 2), lambda b, *_: (b, 0)),
        scratch_shapes=[pltpu.VMEM((MOE_ROW_BUFFERS, bm, D_MODEL // 2), jnp.uint32),
                        pltpu.VMEM((bm, D_MODEL), BF16),
                        pltpu.SMEM((n_slots,), jnp.int32),
                        pltpu.VMEM((2, D_MODEL, D_FF), BF16), pltpu.VMEM((2, D_MODEL, D_FF), BF16),
                        pltpu.VMEM((2, D_FF, D_MODEL), BF16), pltpu.VMEM((D_FF, D_MODEL), F32),
                        pltpu.SemaphoreType.DMA((MOE_ROW_BUFFERS,)), pltpu.SemaphoreType.DMA((2,)),
                        pltpu.SemaphoreType.DMA(())],
    )
    return pl.pallas_call(
        _moe_body,
        grid_spec=grid_spec,
        out_shape=jax.ShapeDtypeStruct((n_slots, D_MODEL // 2), jnp.uint32),
        compiler_params=pltpu.CompilerParams(dimension_semantics=("arbitrary",),
                                             vmem_limit_bytes=VMEM_LIMIT),
        name="moe",
    )(block_e, n_used, slots, pad_lo, pad_hi, expert_seq, expert_next, xn, wg, wu, wd)


def _combine_body(slot_ref, h_ref, w_ref, ys_ref, o_ref, g_ref, sem):
    tb = COMBINE_ROWS
    i = pl.program_id(0)
    cur = i % 2

    def gather_start(block, buf):
        def one(t, c):
            for k in range(TOP_K):
                _row_copy_start(ys_ref, slot_ref[TOP_K * (block * tb + t) + k], g_ref.at[buf, k], t,
                                sem.at[buf], k)
            return c
        lax.fori_loop(0, tb, one, 0, unroll=8)

    @pl.when(i == 0)
    def _():
        gather_start(0, 0)

    @pl.when(i + 1 < pl.num_programs(0))
    def _():
        gather_start(i + 1, 1 - cur)

    for k in range(TOP_K):
        pltpu.make_async_copy(ys_ref.at[pl.ds(0, tb)], g_ref.at[cur, k], sem.at[cur]).wait()
    lo0, hi0 = _unpack_bf16_pairs(g_ref[cur, 0])
    lo1, hi1 = _unpack_bf16_pairs(g_ref[cur, 1])
    w0, w1 = w_ref[:, 0:1], w_ref[:, 1:2]
    half = D_MODEL // 2
    o_ref[:, :half] = h_ref[:, :half] + w0 * lo0 + w1 * lo1
    o_ref[:, half:] = h_ref[:, half:] + w0 * hi0 + w1 * hi1


def _combine(slots, h, wcol, ys):
    T = h.shape[0]
    tb = COMBINE_ROWS
    grid_spec = pltpu.PrefetchScalarGridSpec(
        num_scalar_prefetch=1,
        grid=(T // tb,),
        in_specs=[pl.BlockSpec((tb, D_MODEL), lambda i, s: (i, 0)),
                  pl.BlockSpec((tb, TOP_K), lambda i, s: (i, 0)),
                  pl.BlockSpec(memory_space=pl.ANY)],
        out_specs=pl.BlockSpec((tb, D_MODEL), lambda i, s: (i, 0)),
        scratch_shapes=[pltpu.VMEM((2, TOP_K, tb, D_MODEL // 2), jnp.uint32),
                        pltpu.SemaphoreType.DMA((2,))],
    )
    return pl.pallas_call(
        _combine_body,
        grid_spec=grid_spec,
        out_shape=jax.ShapeDtypeStruct((T, D_MODEL), F32),
        compiler_params=pltpu.CompilerParams(dimension_semantics=("arbitrary",),
                                             vmem_limit_bytes=VMEM_LIMIT),
        name="combine",
    )(slots, h, wcol, ys)


def _rope_tables(T, rows):
    inv_freq = jnp.power(jnp.float32(ROPE_THETA),
                         -jnp.arange(ROPE_HALF, dtype=F32) * (2.0 / ROPE_DIM))

    def tables(pos):
        ang = pos.astype(F32)[:, None] * inv_freq[None, :]
        rest = (pos.shape[0], HEAD_DIM - ROPE_DIM)
        cos = jnp.concatenate([jnp.cos(ang), jnp.cos(ang), jnp.ones(rest, F32)], axis=1)
        sin = jnp.concatenate([jnp.sin(ang), jnp.sin(ang), jnp.zeros(rest, F32)], axis=1)
        return cos, sin

    step_cos, step_sin = tables(jnp.arange(T // rows) * rows)
    row_cos, row_sin = tables(jnp.arange(rows))
    return step_cos[:, None, :], step_sin[:, None, :], row_cos, row_sin


def _split_hi_lo(w):
    hi = w.astype(BF16)
    lo = (w - hi.astype(F32)).astype(BF16)
    return hi, lo


def _layer(x2, ln1_w, w_in, q_norm_w, k_norm_w, attn_sink, attn_out_norm_w, gate_up_f, gate_bias_f,
           gate_up_b, gate_bias_b, gla_out_norm_w, w_out, ln2_w, w_group, b_group, w_router, b_router,
           w_gate_e, w_up_e, w_down_e):
    T = x2.shape[0]
    row = lambda v: v.reshape(1, -1).astype(F32)

    c0, c1, c2, c3 = A_WIDTH, A_WIDTH + 2 * GLA_QK_WIDTH, A_WIDTH + 2 * GLA_QK_WIDTH + GLA_WIDTH, \
        A_WIDTH + 2 * GLA_QK_WIDTH + 2 * GLA_WIDTH
    wa, wqk, wv, wr, wl = (w_in[:, a:b].astype(BF16) for a, b in
                           ((0, c0), (c0, c1), (c1, c2), (c2, c3), (c3, w_in.shape[1])))
    pa, pqk, pv, pr, plr, wg_b = _inproj(x2, row(ln1_w), wa, wqk, wv, wr, wl, row(q_norm_w), row(k_norm_w),
                                         _rope_tables(T, PROJ_ROWS), _cast_rows_job(w_gate_e, T // PROJ_ROWS))

    attn, wu_b = _attention(pa, attn_sink.astype(F32), row(attn_out_norm_w),
                            _cast_rows_job(w_up_e, T // (ATTN_STEP_BLOCKS * ATTN_BLOCK)))

    C = GLA_CHUNK
    ones = jnp.ones((C, C), F32)
    zr = jnp.zeros((GLA_RANK, GLA_QK_WIDTH), F32)
    up_f = jnp.concatenate([gate_up_f.astype(F32), zr], axis=0)
    up_b = jnp.concatenate([zr, gate_up_b.astype(F32)], axis=0)

    def up_pieces(up):
        hi, lo = _split_hi_lo(up)
        return jnp.concatenate([hi, hi, lo], axis=0)

    o_f = _gla_pass(False, pqk, pv, plr, up_pieces(up_f), row(gate_bias_f), jnp.tril(ones).astype(BF16))
    gla = _gla_pass(True, pqk, pv, plr, up_pieces(up_b), row(gate_bias_b), jnp.triu(ones).astype(BF16),
                    extra=(o_f, pr, row(gla_out_norm_w)))

    wr_full = jnp.zeros((D_MODEL, ROUTER_ROWS), F32)
    wr_full = wr_full.at[:, :N_GROUPS].set(w_group.astype(F32))
    wr_full = wr_full.at[:, EXPERT_ROW0:EXPERT_ROW0 + N_EXPERTS].set(w_router.astype(F32))
    wr_hi, wr_lo = _split_hi_lo(wr_full)
    rbias = jnp.zeros((ROUTER_ROWS, 1), F32)
    rbias = rbias.at[:N_GROUPS, 0].set(b_group.astype(F32))
    rbias = rbias.at[EXPERT_ROW0:EXPERT_ROW0 + N_EXPERTS, 0].set(b_router.astype(F32))
    tm = OUT_ROWS
    earlier = jnp.triu(jnp.ones((tm, tm), F32), k=1).astype(BF16)
    h, xn, r_int, r_w, counts = _outproj_router(
        attn, gla, x2, w_out.astype(BF16), row(ln2_w),
        jnp.concatenate([wr_hi, wr_lo], axis=1), wr_hi, rbias, earlier)

    bm = MOE_ROWS
    n_blocks = (T * TOP_K) // bm + N_EXPERTS
    cnt = counts[:, 0].astype(jnp.int32)
    padded = (cnt + bm - 1) // bm * bm
    pad_end = jnp.cumsum(padded)
    pad_start = pad_end - padded
    n_used = (pad_end[-1] // bm).astype(jnp.int32).reshape(1)
    block_e = jnp.minimum(jnp.sum(pad_end[None, :] <= (jnp.arange(n_blocks) * bm)[:, None], axis=1),
                          N_EXPERTS - 1).astype(jnp.int32)
    e_sel = r_int[0:TOP_K].T
    start_sel = jnp.sum(jnp.where(e_sel[..., None] == jnp.arange(N_EXPERTS), pad_start, 0), axis=-1)
    slots = (start_sel + r_int[TOP_K:2 * TOP_K].T).reshape(-1).astype(jnp.int32)

    nonempty = cnt > 0
    expert_seq = (jnp.cumsum(nonempty) - 1).astype(jnp.int32)
    ids = jnp.arange(N_EXPERTS)
    later = (ids[None, :] > ids[:, None]) & nonempty[None, :]
    expert_next = jnp.where(later.any(axis=1), jnp.argmax(later, axis=1), -1).astype(jnp.int32)
    ys = _moe(block_e, n_used, slots, (pad_start + cnt).astype(jnp.int32), pad_end.astype(jnp.int32),
              expert_seq, expert_next, xn,
              wg_b.reshape(w_gate_e.shape), wu_b.reshape(w_up_e.shape), w_down_e.astype(F32), n_blocks * bm)
    return _combine(slots, h, r_w[0:TOP_K].T, ys)


def kernel(x, ln1_w, w_in, q_norm_w, k_norm_w, attn_sink, attn_out_norm_w, gla_gate_up_f, gla_gate_bias_f,
           gla_gate_up_b, gla_gate_bias_b, gla_out_norm_w, w_out, ln2_w, w_group, b_group, w_router,
           b_router, w_gate_e, w_up_e, w_down_e):
    B, S, D = x.shape
    h = x.reshape(B * S, D)
    assert B == 1
    for l in range(ln1_w.shape[0]):
        h = _layer(h, ln1_w[l], w_in[l], q_norm_w[l], k_norm_w[l], attn_sink[l], attn_out_norm_w[l],
                   gla_gate_up_f[l], gla_gate_bias_f[l], gla_gate_up_b[l], gla_gate_bias_b[l],
                   gla_out_norm_w[l], w_out[l], ln2_w[l], w_group[l], b_group[l], w_router[l],
                   b_router[l], w_gate_e[l], w_up_e[l], w_down_e[l])
    return h.reshape(B, S, D)
```

```python
import functools
from typing import NamedTuple

import jax
import jax.numpy as jnp
from jax import lax
from jax.experimental import pallas as pl
from jax.experimental.pallas import tpu as pltpu

F32 = jnp.float32
BF16 = jnp.bfloat16

EPS = 1e-6
D_MODEL = 2048

ATTN_Q_HEADS = 8
ATTN_KV_HEADS = 2
ATTN_GROUP = ATTN_Q_HEADS // ATTN_KV_HEADS
HEAD_DIM = 128
WINDOW = 128
ATTN_BLOCK = 128
ATTN_STEP_BLOCKS = 4
ROPE_THETA = 500000.0
ROPE_DIM = HEAD_DIM // 4
ROPE_HALF = ROPE_DIM // 2
ATTN_WIDTH = ATTN_Q_HEADS * HEAD_DIM
KV_WIDTH = ATTN_KV_HEADS * HEAD_DIM
A_WIDTH = ATTN_WIDTH + 2 * KV_WIDTH

GLA_HEADS = 4
GLA_DK = 128
GLA_DV = 256
GLA_RANK = 16
GLA_TAU = 16.0
GLA_QK_WIDTH = GLA_HEADS * GLA_DK
GLA_WIDTH = GLA_HEADS * GLA_DV
GLA_CHUNK = 128
GLA_STEP_CHUNKS = 4

N_GROUPS = 4
EXPERTS_PER_GROUP = 8
N_EXPERTS = N_GROUPS * EXPERTS_PER_GROUP
TOP_K = 2
D_FF = 1024
ROUTER_ROWS = 128
EXPERT_ROW0 = 8

PROJ_ROWS = 256
OUT_ROWS = 256
MOE_ROWS = 256
MOE_ROW_BUFFERS = 3
MOE_CAST_ROWS = 128
COMBINE_ROWS = 256

VMEM_LIMIT = 56 * 1024 * 1024


def _dot(a, b):
    return jnp.dot(a, b, preferred_element_type=F32)


def _dot_nt(a, b):
    return lax.dot_general(a, b, (((1,), (1,)), ((), ())), preferred_element_type=F32)


def _dot_tn(a, b):
    return lax.dot_general(a, b, (((0,), (0,)), ((), ())), preferred_element_type=F32)


def _pack_bf16_pairs(x):
    n = x.shape[1] // 2
    lo = lax.bitcast_convert_type(x[:, :n].astype(F32), jnp.uint32) >> 16
    hi = lax.bitcast_convert_type(x[:, n:].astype(F32), jnp.uint32) & jnp.uint32(0xFFFF0000)
    return lo | hi


def _unpack_bf16_pairs(w):
    lo = lax.bitcast_convert_type(w << 16, F32)
    hi = lax.bitcast_convert_type(w & jnp.uint32(0xFFFF0000), F32)
    return lo, hi


class CastJob(NamedTuple):
    array: jax.Array
    in_spec: pl.BlockSpec
    out_spec: pl.BlockSpec
    out_shape: jax.ShapeDtypeStruct


def _cast_rows_job(w, n_steps):
    w2 = w.astype(F32).reshape(-1, w.shape[-1])
    rows = w2.shape[0] // n_steps
    assert rows * n_steps == w2.shape[0] and rows % 16 == 0
    spec = pl.BlockSpec((rows, w2.shape[1]), lambda i, *_: (i, 0))
    return CastJob(w2, spec, spec, jax.ShapeDtypeStruct(w2.shape, BF16))


def _resident(shape):
    nd = len(shape)
    return pl.BlockSpec(shape, lambda *_: (0,) * nd, pipeline_mode=pl.Buffered(1))


def _inproj_body(x_ref, ln_ref, wa_ref, wqk_ref, wv_ref, wr_ref, wl_ref, qn_ref, kn_ref,
                 step_cos_ref, step_sin_ref, row_cos_ref, row_sin_ref, wcast_ref,
                 oa_ref, oqk_ref, ov_ref, or_ref, ol_ref, wcast_out_ref):
    wcast_out_ref[...] = wcast_ref[...].astype(BF16)
    x = x_ref[...]
    ms = jnp.mean(x * x, axis=-1, keepdims=True)
    xn = (x * lax.rsqrt(ms + EPS) * ln_ref[...]).astype(BF16)
    acc = _dot(xn, wa_ref[...])
    ca, sn = step_cos_ref[0], step_sin_ref[0]
    cb, sb_ = row_cos_ref[...], row_sin_ref[...]
    cos = cb * ca - sb_ * sn
    sin = sb_ * ca + cb * sn
    lane = lax.broadcasted_iota(jnp.int32, (1, HEAD_DIM), 1)
    sa = jnp.where((lane >= ROPE_HALF) & (lane < ROPE_DIM), sin, 0.0)
    sb = jnp.where(lane < ROPE_HALF, -sin, 0.0)
    for c in range(ATTN_Q_HEADS + ATTN_KV_HEADS):
        xh = acc[:, c * HEAD_DIM:(c + 1) * HEAD_DIM]
        w = qn_ref[...] if c < ATTN_Q_HEADS else kn_ref[...]
        y = xh * lax.rsqrt(jnp.mean(xh * xh, axis=-1, keepdims=True) + EPS) * w
        y = (y * cos + pltpu.roll(y, ROPE_HALF, 1) * sa
             + pltpu.roll(y, HEAD_DIM - ROPE_HALF, 1) * sb)
        if c < ATTN_Q_HEADS:
            y = y * (HEAD_DIM ** -0.5)
        oa_ref[:, c * HEAD_DIM:(c + 1) * HEAD_DIM] = y.astype(BF16)
    oa_ref[:, ATTN_WIDTH + KV_WIDTH:] = acc[:, ATTN_WIDTH + KV_WIDTH:].astype(BF16)
    oqk_ref[...] = _dot(xn, wqk_ref[...]).astype(BF16)
    ov_ref[...] = _dot(xn, wv_ref[...]).astype(BF16)
    or_ref[...] = _dot(xn, wr_ref[...]).astype(BF16)
    ol_ref[...] = _dot(xn, wl_ref[...])


def _inproj(x2, ln_w, wa, wqk, wv, wr, wl, qn, kn, rope, cast):
    T = x2.shape[0]
    tm = PROJ_ROWS
    row = lambda w: pl.BlockSpec((tm, w), lambda i: (i, 0))
    step_row = pl.BlockSpec((1, 1, HEAD_DIM), lambda i: (i, 0, 0))
    return pl.pallas_call(
        _inproj_body,
        grid=(T // tm,),
        in_specs=[row(D_MODEL), _resident((1, D_MODEL)),
                  _resident(wa.shape), _resident(wqk.shape), _resident(wv.shape),
                  _resident(wr.shape), _resident(wl.shape),
                  _resident((1, HEAD_DIM)), _resident((1, HEAD_DIM)),
                  step_row, step_row, _resident((tm, HEAD_DIM)), _resident((tm, HEAD_DIM)), cast.in_spec],
        out_specs=[row(A_WIDTH), row(2 * GLA_QK_WIDTH), row(GLA_WIDTH), row(GLA_WIDTH),
                   row(2 * GLA_RANK), cast.out_spec],
        out_shape=[jax.ShapeDtypeStruct((T, A_WIDTH), BF16),
                   jax.ShapeDtypeStruct((T, 2 * GLA_QK_WIDTH), BF16),
                   jax.ShapeDtypeStruct((T, GLA_WIDTH), BF16),
                   jax.ShapeDtypeStruct((T, GLA_WIDTH), BF16),
                   jax.ShapeDtypeStruct((T, 2 * GLA_RANK), F32), cast.out_shape],
        compiler_params=pltpu.CompilerParams(dimension_semantics=("parallel",),
                                             vmem_limit_bytes=VMEM_LIMIT),
        name="inproj",
    )(x2, ln_w, wa, wqk, wv, wr, wl, qn, kn, *rope, cast.array)


def _attn_body(sink_ref, q_ref, kp_ref, kc_ref, kn_ref, vp_ref, vc_ref, vn_ref, nw_ref, wcast_ref,
               o_ref, wcast_out_ref):
    wcast_out_ref[...] = wcast_ref[...].astype(BF16)
    n = pl.program_id(0)
    nb = pl.num_programs(0)
    rows = ATTN_GROUP * ATTN_BLOCK
    keys = 3 * ATTN_BLOCK
    r = lax.broadcasted_iota(jnp.int32, (rows, keys), 0) & (ATTN_BLOCK - 1)
    c = lax.broadcasted_iota(jnp.int32, (rows, keys), 1)
    band = (c >= r + (ATTN_BLOCK - WINDOW)) & (c <= r + (ATTN_BLOCK + WINDOW))
    edge_mask = [band & ((c >= ATTN_BLOCK) | (n > 0))] + [band] * (ATTN_STEP_BLOCKS - 2) + \
                [band & ((c < 2 * ATTN_BLOCK) | (n < nb - 1))]

    def window(j, prev_ref, cur_ref, next_ref, ks):
        blocks = [prev_ref[:, ks]] + [cur_ref[i * ATTN_BLOCK:(i + 1) * ATTN_BLOCK, ks]
                                      for i in range(ATTN_STEP_BLOCKS)] + [next_ref[:, ks]]
        return jnp.concatenate(blocks[j:j + 3], axis=0)

    chains = [(j, g) for j in range(ATTN_STEP_BLOCKS) for g in range(ATTN_KV_HEADS)]
    heads_of = lambda g: range(g * ATTN_GROUP, (g + 1) * ATTN_GROUP)
    ks_of = lambda g: slice(g * HEAD_DIM, (g + 1) * HEAD_DIM)
    qrows_of = lambda j: slice(j * ATTN_BLOCK, (j + 1) * ATTN_BLOCK)
    s = [jnp.where(edge_mask[j],
                   _dot_nt(jnp.concatenate([q_ref[qrows_of(j), h * HEAD_DIM:(h + 1) * HEAD_DIM]
                                            for h in heads_of(g)], axis=0),
                           window(j, kp_ref, kc_ref, kn_ref, ks_of(g))), -jnp.inf)
         for j, g in chains]
    lane_blocks = keys // HEAD_DIM
    sink = [jnp.concatenate([jnp.full((ATTN_BLOCK, HEAD_DIM), sink_ref[h], F32) for h in heads_of(g)], axis=0)
            for j, g in chains]
    m = [jnp.maximum(jnp.broadcast_to(jnp.max(si, axis=-1, keepdims=True), (rows, HEAD_DIM)), sk)
         for si, sk in zip(s, sink)]
    p = [jnp.concatenate([jnp.exp(si[:, c * HEAD_DIM:(c + 1) * HEAD_DIM] - mi) for c in range(lane_blocks)]
                         + [jnp.exp(sk - mi)], axis=1).astype(BF16)
         for si, sk, mi in zip(s, sink, m)]
    sink_rows = jnp.concatenate([jnp.zeros((HEAD_DIM, HEAD_DIM), BF16),
                                 jnp.full((HEAD_DIM, HEAD_DIM), 1.0 / HEAD_DIM, BF16)], axis=1)

    def values_and_ones(j, g):
        v3 = window(j, vp_ref, vc_ref, vn_ref, ks_of(g))
        return jnp.concatenate([jnp.concatenate([v3, jnp.ones((keys, HEAD_DIM), BF16)], axis=1), sink_rows],
                               axis=0)

    o = []
    for (j, g), pi in zip(chains, p):
        num_den = _dot(pi, values_and_ones(j, g))
        o.append(num_den[:, :HEAD_DIM] / num_den[:, HEAD_DIM:])
    for j in range(ATTN_STEP_BLOCKS):
        outs = [o[chains.index((j, g))][i * ATTN_BLOCK:(i + 1) * ATTN_BLOCK]
                for g in range(ATTN_KV_HEADS) for i in range(ATTN_GROUP)]
        a = jnp.concatenate(outs, axis=1)
        a = a * lax.rsqrt(jnp.mean(a * a, axis=-1, keepdims=True) + EPS) * nw_ref[...]
        o_ref[qrows_of(j), :] = a.astype(BF16)


def _attention(pa, sink, norm_w, cast):
    T = pa.shape[0]
    sb = ATTN_STEP_BLOCKS
    nb = T // ATTN_BLOCK
    kcol = ATTN_WIDTH // KV_WIDTH
    vcol = kcol + 1
    prev = lambda col: pl.BlockSpec((ATTN_BLOCK, KV_WIDTH), lambda n, s: (jnp.maximum(sb * n - 1, 0), col))
    nxt = lambda col: pl.BlockSpec((ATTN_BLOCK, KV_WIDTH),
                                   lambda n, s: (jnp.minimum(sb * n + sb, nb - 1), col))
    cur = lambda col: pl.BlockSpec((sb * ATTN_BLOCK, KV_WIDTH), lambda n, s: (n, col))
    grid_spec = pltpu.PrefetchScalarGridSpec(
        num_scalar_prefetch=1,
        grid=(nb // sb,),
        in_specs=[pl.BlockSpec((sb * ATTN_BLOCK, ATTN_WIDTH), lambda n, s: (n, 0)),
                  prev(kcol), cur(kcol), nxt(kcol), prev(vcol), cur(vcol), nxt(vcol),
                  pl.BlockSpec((1, ATTN_WIDTH), lambda n, s: (0, 0)), cast.in_spec],
        out_specs=[pl.BlockSpec((sb * ATTN_BLOCK, ATTN_WIDTH), lambda n, s: (n, 0)), cast.out_spec],
    )
    return pl.pallas_call(
        _attn_body,
        grid_spec=grid_spec,
        out_shape=[jax.ShapeDtypeStruct((T, ATTN_WIDTH), BF16), cast.out_shape],
        compiler_params=pltpu.CompilerParams(dimension_semantics=("parallel",),
                                             vmem_limit_bytes=VMEM_LIMIT),
        name="attention",
    )(sink, pa, pa, pa, pa, pa, pa, pa, norm_w, cast.array)


def _gla_body(reverse, final, *refs):
    if final:
        (q_ref, k_ref, v_ref, lr_ref, up_ref, bias_ref, tri_ref, of_ref, gr_ref, nw_ref,
         o_ref, st_ref) = refs
    else:
        q_ref, k_ref, v_ref, lr_ref, up_ref, bias_ref, tri_ref, o_ref, st_ref = refs
    C = GLA_CHUNK
    R = GLA_STEP_CHUNKS

    @pl.when(pl.program_id(0) == 0)
    def _():
        st_ref[...] = jnp.zeros_like(st_ref)

    chunks = range(R)
    order = tuple(reversed(chunks)) if reverse else tuple(chunks)
    rows = [slice(s * C, (s + 1) * C) for s in chunks]
    heads = range(GLA_HEADS)
    ks = [slice(h * GLA_DK, (h + 1) * GLA_DK) for h in heads]
    vs = [slice(h * GLA_DV, (h + 1) * GLA_DV) for h in heads]

    lr = lr_ref[...]
    lr_hi = lr.astype(BF16)
    lr_lo = (lr - lr_hi.astype(F32)).astype(BF16)
    g = _dot(jnp.concatenate([lr_hi, lr_lo, lr_hi], axis=1), up_ref[...]) + bias_ref[...]
    la = jax.nn.log_sigmoid(g) * (1.0 / GLA_TAU)
    tri = tri_ref[...]
    hi, lo = _split_hi_lo(la)
    b = [_dot(tri, hi[rows[s]]) + _dot(tri, lo[rows[s]]) for s in chunks]
    end = 0 if reverse else C - 1
    b_end = [b[s][end:end + 1] for s in chunks]
    b_mid = [b[s][C // 2:C // 2 + 1] for s in chunks]
    scale = GLA_DK ** -0.5
    q = [q_ref[rows[s], :].astype(F32) * scale for s in chunks]
    k = [k_ref[rows[s], :].astype(F32) for s in chunks]
    q_in = [(q[s] * jnp.exp(b[s] - b_mid[s])).astype(BF16) for s in chunks]
    k_in = [(k[s] * jnp.exp(b_mid[s] - b[s])).astype(BF16) for s in chunks]
    q_dec = [(q[s] * jnp.exp(b[s])).astype(BF16) for s in chunks]
    k_end = [(k[s] * jnp.exp(b_end[s] - b[s])).astype(BF16) for s in chunks]
    decay = [jnp.exp(b_end[s]) for s in chunks]
    ri = lax.broadcasted_iota(jnp.int32, (C, C), 0)
    ci = lax.broadcasted_iota(jnp.int32, (C, C), 1)
    causal = (ri <= ci) if reverse else (ri >= ci)
    att = [[jnp.where(causal, _dot_nt(q_in[s][:, ks[h]], k_in[s][:, ks[h]]), 0.0).astype(BF16)
            for h in heads] for s in chunks]
    intra = [[_dot(att[s][h], v_ref[rows[s], vs[h]]) for h in heads] for s in chunks]
    gain = [[_dot_tn(v_ref[rows[s], vs[h]], k_end[s][:, ks[h]]) for h in heads] for s in chunks]
    st = [st_ref[h] for h in heads]
    o = [[None] * GLA_HEADS for _ in chunks]
    for s in order:
        for h in heads:
            o[s][h] = intra[s][h] + _dot_nt(q_dec[s][:, ks[h]], st[h].astype(BF16))
            st[h] = st[h] * decay[s][:, ks[h]] + gain[s][h]
    for h in heads:
        st_ref[h] = st[h]
    for s in chunks:
        if final:
            tot = [o[s][h] + of_ref[rows[s], vs[h]] for h in heads]
            inv = [lax.rsqrt(jnp.mean(t * t, axis=-1, keepdims=True) + EPS) for t in tot]
            for h in heads:
                y = tot[h] * inv[h] * nw_ref[...]
                o_ref[rows[s], vs[h]] = (y * jax.nn.silu(gr_ref[rows[s], vs[h]].astype(F32))).astype(BF16)
        else:
            for h in heads:
                o_ref[rows[s], vs[h]] = o[s][h]


def _gla_pass(reverse, pqk, pv, plr, up, bias, tri, extra=None):
    T = pqk.shape[0]
    rows = GLA_STEP_CHUNKS * GLA_CHUNK
    n = T // rows
    final = extra is not None
    blk = (lambda i: n - 1 - i) if reverse else (lambda i: i)
    row = lambda w, col=0: pl.BlockSpec((rows, w), lambda i: (blk(i), col))
    in_specs = [row(GLA_QK_WIDTH, 0), row(GLA_QK_WIDTH, 1), row(GLA_WIDTH), row(2 * GLA_RANK),
                _resident(up.shape), _resident(bias.shape), _resident(tri.shape)]
    args = [pqk, pqk, pv, plr, up, bias, tri]
    if final:
        o_f, pr, norm_w = extra
        in_specs += [row(GLA_WIDTH), row(GLA_WIDTH), _resident(norm_w.shape)]
        args += [o_f, pr, norm_w]
    return pl.pallas_call(
        functools.partial(_gla_body, reverse, final),
        grid=(n,),
        in_specs=in_specs,
        out_specs=row(GLA_WIDTH),
        out_shape=jax.ShapeDtypeStruct((T, GLA_WIDTH), BF16 if final else F32),
        scratch_shapes=[pltpu.VMEM((GLA_HEADS, GLA_DV, GLA_DK), F32)],
        compiler_params=pltpu.CompilerParams(dimension_semantics=("arbitrary",),
                                             vmem_limit_bytes=VMEM_LIMIT),
        name="gla_bwd" if reverse else "gla_fwd",
    )(*args)


def _outproj_body(a_ref, g_ref, x_ref, w_ref, ln_ref, wr_hl_ref, wr_hi_ref, rb_ref, tri_ref,
                  h_ref, xn_ref, ri_ref, rw_ref, cnt_ref):
    tm = x_ref.shape[0]

    @pl.when(pl.program_id(0) == 0)
    def _():
        cnt_ref[...] = jnp.zeros_like(cnt_ref)

    h = x_ref[...] + _dot(jnp.concatenate([a_ref[...], g_ref[...]], axis=1), w_ref[...])
    h_ref[...] = h
    xn = h * lax.rsqrt(jnp.mean(h * h, axis=-1, keepdims=True) + EPS) * ln_ref[...]
    x_hi = xn.astype(BF16)
    xn_ref[...] = _pack_bf16_pairs(x_hi)
    x_lo = (xn - x_hi.astype(F32)).astype(BF16)
    l2 = _dot(x_hi, wr_hl_ref[...])
    logits = l2[:, :ROUTER_ROWS] + l2[:, ROUTER_ROWS:] + _dot(x_lo, wr_hi_ref[...])
    lt = jnp.transpose(logits) + rb_ref[...]

    gl = lt[0:N_GROUPS]
    gmax = jnp.max(gl, axis=0, keepdims=True)
    gi = lax.broadcasted_iota(jnp.int32, gl.shape, 0).astype(F32)
    g_sel = jnp.min(jnp.where(gl == gmax, gi, float(N_GROUPS)), axis=0, keepdims=True)
    g_gate = 1.0 / jnp.sum(jnp.exp(gl - gmax), axis=0, keepdims=True)

    el = lt[EXPERT_ROW0:EXPERT_ROW0 + N_EXPERTS]
    ei_int = lax.broadcasted_iota(jnp.int32, el.shape, 0)
    ei = ei_int.astype(F32)
    grp = (ei_int >> 3).astype(F32)
    cand = jnp.where(grp == g_sel, el, -jnp.inf)
    v1 = jnp.max(cand, axis=0, keepdims=True)
    e1 = jnp.min(jnp.where(cand == v1, ei, float(N_EXPERTS)), axis=0, keepdims=True)
    cand2 = jnp.where(ei == e1, -jnp.inf, cand)
    v2 = jnp.max(cand2, axis=0, keepdims=True)
    e2 = jnp.min(jnp.where(cand2 == v2, ei, float(N_EXPERTS)), axis=0, keepdims=True)
    d = jnp.exp(v2 - v1)
    w1 = g_gate / (1.0 + d)
    w2 = g_gate * d / (1.0 + d)

    oh1 = (ei == e1).astype(F32)
    oh2 = (ei == e2).astype(F32)
    cnt = oh1 + oh2
    before = _dot(cnt.astype(BF16), tri_ref[...]) + cnt_ref[:, 0:1]
    r1 = jnp.sum(oh1 * before, axis=0, keepdims=True)
    r2 = jnp.sum(oh2 * before, axis=0, keepdims=True)
    cnt_ref[...] = cnt_ref[...] + jnp.sum(cnt, axis=1, keepdims=True)

    ri_ref[...] = jnp.concatenate([e1, e2, r1, r2, jnp.zeros((4, tm), F32)], axis=0).astype(jnp.int32)
    rw_ref[...] = jnp.concatenate([w1, w2, jnp.zeros((6, tm), F32)], axis=0)


def _outproj_router(attn, gla, x2, w_out, ln_w, wr_hl, wr_hi, rbias, tri):
    T = x2.shape[0]
    tm = OUT_ROWS
    row = lambda w: pl.BlockSpec((tm, w), lambda i: (i, 0))
    col = lambda r: pl.BlockSpec((r, tm), lambda i: (0, i))
    return pl.pallas_call(
        _outproj_body,
        grid=(T // tm,),
        in_specs=[row(ATTN_WIDTH), row(GLA_WIDTH), row(D_MODEL),
                  _resident(w_out.shape), _resident(ln_w.shape),
                  _resident(wr_hl.shape), _resident(wr_hi.shape), _resident(rbias.shape),
                  _resident(tri.shape)],
        out_specs=[row(D_MODEL), row(D_MODEL // 2), col(8), col(8),
                   pl.BlockSpec((N_EXPERTS, 128), lambda i: (0, 0))],
        out_shape=[jax.ShapeDtypeStruct((T, D_MODEL), F32),
                   jax.ShapeDtypeStruct((T, D_MODEL // 2), jnp.uint32),
                   jax.ShapeDtypeStruct((8, T), jnp.int32),
                   jax.ShapeDtypeStruct((8, T), F32),
                   jax.ShapeDtypeStruct((N_EXPERTS, 128), F32)],
        compiler_params=pltpu.CompilerParams(dimension_semantics=("arbitrary",),
                                             vmem_limit_bytes=VMEM_LIMIT),
        name="outproj_router",
    )(attn, gla, x2, w_out, ln_w, wr_hl, wr_hi, rbias, tri)


def _row_copy_start(src, s, dst, d, sem, queue):
    pltpu.async_copy(src.at[pl.ds(s, 1)], dst.at[pl.ds(d, 1)], sem, priority=queue)


def _moe_body(be_ref, nu_ref, slot_ref, lo_ref, hi_ref, seq_ref, nxt_ref, xn_ref, wg_ref, wu_ref, wd_ref,
              y_ref, xbuf, xb_ref, tok_ref, wg_buf, wu_buf, wd_buf, wd_stage, sem, wsem, dsem):
    bm = MOE_ROWS
    b = pl.program_id(0)
    n_used = nu_ref[0]
    cur = b % MOE_ROW_BUFFERS
    expert = be_ref[b]
    wslot = seq_ref[expert] % 2
    first_of_expert = (b == 0) | (be_ref[jnp.maximum(b - 1, 0)] != expert)

    def weight_copies(e, slot):
        return [pltpu.make_async_copy(wg_ref.at[e], wg_buf.at[slot], wsem.at[slot]),
                pltpu.make_async_copy(wu_ref.at[e], wu_buf.at[slot], wsem.at[slot]),
                pltpu.make_async_copy(wd_ref.at[e], wd_stage, dsem)]

    def gather_start(block, buf, unrolled):
        def one(i, queue):
            _row_copy_start(xn_ref, tok_ref[block * bm + i], xbuf.at[buf], i, sem.at[buf], queue)
        if unrolled:
            for i in range(bm):
                one(i, i % 2)
        else:
            def pair(p, c):
                one(2 * p, 0)
                one(2 * p + 1, 1)
                return c
            lax.fori_loop(0, bm // 2, pair, 0)

    def gather_wait(buf):
        pltpu.make_async_copy(xn_ref.at[pl.ds(0, bm)], xbuf.at[buf], sem.at[buf]).wait()

    @pl.when(b == 0)
    def _():
        def per_expert(e, c):
            def zero(s, c2):
                tok_ref[s] = 0
                return c2
            return lax.fori_loop(lo_ref[e], hi_ref[e], zero, c)
        lax.fori_loop(0, N_EXPERTS, per_expert, 0)

        def fill(t, c):
            for k in range(TOP_K):
                tok_ref[slot_ref[TOP_K * t + k]] = t
            return c
        lax.fori_loop(0, slot_ref.shape[0] // TOP_K, fill, 0, unroll=8)
        gather_start(0, 0, False)

        @pl.when(n_used > 1)
        def _():
            gather_start(1, 1, False)

    @pl.when((b < n_used) & first_of_expert)
    def _():
        @pl.when(b == 0)
        def _():
            for c in weight_copies(expert, wslot):
                c.start()
        for c in weight_copies(expert, wslot):
            c.wait()
        for r in range(0, D_FF, MOE_CAST_ROWS):
            wd_buf[wslot, r:r + MOE_CAST_ROWS, :] = wd_stage[r:r + MOE_CAST_ROWS, :].astype(BF16)
        following = nxt_ref[expert]

        @pl.when(following >= 0)
        def _():
            for c in weight_copies(following, 1 - wslot):
                c.start()

    def stage_rows():
        gather_wait(cur)
        lo, hi = _unpack_bf16_pairs(xbuf[cur])
        xb_ref[:, :D_MODEL // 2] = lo.astype(BF16)
        xb_ref[:, D_MODEL // 2:] = hi.astype(BF16)

    def experts():
        x = xb_ref[...]
        hid = jax.nn.silu(_dot(x, wg_buf[wslot])) * _dot(x, wu_buf[wslot])
        y = _dot(hid.astype(BF16), wd_buf[wslot])
        y_ref[...] = _pack_bf16_pairs(y.astype(BF16))

    @pl.when(b + 2 < n_used)
    def _():
        stage_rows()
        gather_start(b + 2, (b + 2) % MOE_ROW_BUFFERS, True)
        experts()

    @pl.when((b < n_used) & (b + 2 >= n_used))
    def _():
        stage_rows()
        experts()

    @pl.when(b >= n_used)
    def _():
        y_ref[...] = jnp.zeros_like(y_ref)


def _moe(block_e, n_used, slots, pad_lo, pad_hi, expert_seq, expert_next, xn, wg, wu, wd, n_slots):
    bm = MOE_ROWS
    any_spec = pl.BlockSpec(memory_space=pl.ANY)
    grid_spec = pltpu.PrefetchScalarGridSpec(
        num_scalar_prefetch=7,
        grid=(n_slots // bm,),
        in_specs=[any_spec] * 4,
        out_specs=pl.BlockSpec((bm, D_MODEL // 2), lambda b, *_: (b, 0)),
        scratch_shapes=[pltpu.VMEM((MOE_ROW_BUFFERS, bm, D_MODEL // 2), jnp.uint32),
                        pltpu.VMEM((bm, D_MODEL), BF16),
                        pltpu.SMEM((n_slots,), jnp.int32),
                        pltpu.VMEM((2, D_MODEL, D_FF), BF16), pltpu.VMEM((2, D_MODEL, D_FF), BF16),
                        pltpu.VMEM((2, D_FF, D_MODEL), BF16), pltpu.VMEM((D_FF, D_MODEL), F32),
                        pltpu.SemaphoreType.DMA((MOE_ROW_BUFFERS,)), pltpu.SemaphoreType.DMA((2,)),
                        pltpu.SemaphoreType.DMA(())],
    )
    return pl.pallas_call(
        _moe_body,
        grid_spec=grid_spec,
        out_shape=jax.ShapeDtypeStruct((n_slots, D_MODEL // 2), jnp.uint32),
        compiler_params=pltpu.CompilerParams(dimension_semantics=("arbitrary",),
                                             vmem_limit_bytes=VMEM_LIMIT),
        name="moe",
    )(block_e, n_used, slots, pad_lo, pad_hi, expert_seq, expert_next, xn, wg, wu, wd)


def _combine_body(slot_ref, h_ref, w_ref, ys_ref, o_ref, g_ref, sem):
    tb = COMBINE_ROWS
    i = pl.program_id(0)
    cur = i % 2

    def gather_start(block, buf):
        def one(t, c):
            for k in range(TOP_K):
                _row_copy_start(ys_ref, slot_ref[TOP_K * (block * tb + t) + k], g_ref.at[buf, k], t,
                                sem.at[buf], k)
            return c
        lax.fori_loop(0, tb, one, 0, unroll=8)

    @pl.when(i == 0)
    def _():
        gather_start(0, 0)

    @pl.when(i + 1 < pl.num_programs(0))
    def _():
        gather_start(i + 1, 1 - cur)

    for k in range(TOP_K):
        pltpu.make_async_copy(ys_ref.at[pl.ds(0, tb)], g_ref.at[cur, k], sem.at[cur]).wait()
    lo0, hi0 = _unpack_bf16_pairs(g_ref[cur, 0])
    lo1, hi1 = _unpack_bf16_pairs(g_ref[cur, 1])
    w0, w1 = w_ref[:, 0:1], w_ref[:, 1:2]
    half = D_MODEL // 2
    o_ref[:, :half] = h_ref[:, :half] + w0 * lo0 + w1 * lo1
    o_ref[:, half:] = h_ref[:, half:] + w0 * hi0 + w1 * hi1


def _combine(slots, h, wcol, ys):
    T = h.shape[0]
    tb = COMBINE_ROWS
    grid_spec = pltpu.PrefetchScalarGridSpec(
        num_scalar_prefetch=1,
        grid=(T // tb,),
        in_specs=[pl.BlockSpec((tb, D_MODEL), lambda i, s: (i, 0)),
                  pl.BlockSpec((tb, TOP_K), lambda i, s: (i, 0)),
                  pl.BlockSpec(memory_space=pl.ANY)],
        out_specs=pl.BlockSpec((tb, D_MODEL), lambda i, s: (i, 0)),
        scratch_shapes=[pltpu.VMEM((2, TOP_K, tb, D_MODEL // 2), jnp.uint32),
                        pltpu.SemaphoreType.DMA((2,))],
    )
    return pl.pallas_call(
        _combine_body,
        grid_spec=grid_spec,
        out_shape=jax.ShapeDtypeStruct((T, D_MODEL), F32),
        compiler_params=pltpu.CompilerParams(dimension_semantics=("arbitrary",),
                                             vmem_limit_bytes=VMEM_LIMIT),
        name="combine",
    )(slots, h, wcol, ys)


def _rope_tables(T, rows):
    inv_freq = jnp.power(jnp.float32(ROPE_THETA),
                         -jnp.arange(ROPE_HALF, dtype=F32) * (2.0 / ROPE_DIM))

    def tables(pos):
        ang = pos.astype(F32)[:, None] * inv_freq[None, :]
        rest = (pos.shape[0], HEAD_DIM - ROPE_DIM)
        cos = jnp.concatenate([jnp.cos(ang), jnp.cos(ang), jnp.ones(rest, F32)], axis=1)
        sin = jnp.concatenate([jnp.sin(ang), jnp.sin(ang), jnp.zeros(rest, F32)], axis=1)
        return cos, sin

    step_cos, step_sin = tables(jnp.arange(T // rows) * rows)
    row_cos, row_sin = tables(jnp.arange(rows))
    return step_cos[:, None, :], step_sin[:, None, :], row_cos, row_sin


def _split_hi_lo(w):
    hi = w.astype(BF16)
    lo = (w - hi.astype(F32)).astype(BF16)
    return hi, lo


def _layer(x2, ln1_w, w_in, q_norm_w, k_norm_w, attn_sink, attn_out_norm_w, gate_up_f, gate_bias_f,
           gate_up_b, gate_bias_b, gla_out_norm_w, w_out, ln2_w, w_group, b_group, w_router, b_router,
           w_gate_e, w_up_e, w_down_e):
    T = x2.shape[0]
    row = lambda v: v.reshape(1, -1).astype(F32)

    c0, c1, c2, c3 = A_WIDTH, A_WIDTH + 2 * GLA_QK_WIDTH, A_WIDTH + 2 * GLA_QK_WIDTH + GLA_WIDTH, \
        A_WIDTH + 2 * GLA_QK_WIDTH + 2 * GLA_WIDTH
    wa, wqk, wv, wr, wl = (w_in[:, a:b].astype(BF16) for a, b in
                           ((0, c0), (c0, c1), (c1, c2), (c2, c3), (c3, w_in.shape[1])))
    pa, pqk, pv, pr, plr, wg_b = _inproj(x2, row(ln1_w), wa, wqk, wv, wr, wl, row(q_norm_w), row(k_norm_w),
                                         _rope_tables(T, PROJ_ROWS), _cast_rows_job(w_gate_e, T // PROJ_ROWS))

    attn, wu_b = _attention(pa, attn_sink.astype(F32), row(attn_out_norm_w),
                            _cast_rows_job(w_up_e, T // (ATTN_STEP_BLOCKS * ATTN_BLOCK)))

    C = GLA_CHUNK
    ones = jnp.ones((C, C), F32)
    zr = jnp.zeros((GLA_RANK, GLA_QK_WIDTH), F32)
    up_f = jnp.concatenate([gate_up_f.astype(F32), zr], axis=0)
    up_b = jnp.concatenate([zr, gate_up_b.astype(F32)], axis=0)

    def up_pieces(up):
        hi, lo = _split_hi_lo(up)
        return jnp.concatenate([hi, hi, lo], axis=0)

    o_f = _gla_pass(False, pqk, pv, plr, up_pieces(up_f), row(gate_bias_f), jnp.tril(ones).astype(BF16))
    gla = _gla_pass(True, pqk, pv, plr, up_pieces(up_b), row(gate_bias_b), jnp.triu(ones).astype(BF16),
                    extra=(o_f, pr, row(gla_out_norm_w)))

    wr_full = jnp.zeros((D_MODEL, ROUTER_ROWS), F32)
    wr_full = wr_full.at[:, :N_GROUPS].set(w_group.astype(F32))
    wr_full = wr_full.at[:, EXPERT_ROW0:EXPERT_ROW0 + N_EXPERTS].set(w_router.astype(F32))
    wr_hi, wr_lo = _split_hi_lo(wr_full)
    rbias = jnp.zeros((ROUTER_ROWS, 1), F32)
    rbias = rbias.at[:N_GROUPS, 0].set(b_group.astype(F32))
    rbias = rbias.at[EXPERT_ROW0:EXPERT_ROW0 + N_EXPERTS, 0].set(b_router.astype(F32))
    tm = OUT_ROWS
    earlier = jnp.triu(jnp.ones((tm, tm), F32), k=1).astype(BF16)
    h, xn, r_int, r_w, counts = _outproj_router(
        attn, gla, x2, w_out.astype(BF16), row(ln2_w),
        jnp.concatenate([wr_hi, wr_lo], axis=1), wr_hi, rbias, earlier)

    bm = MOE_ROWS
    n_blocks = (T * TOP_K) // bm + N_EXPERTS
    cnt = counts[:, 0].astype(jnp.int32)
    padded = (cnt + bm - 1) // bm * bm
    pad_end = jnp.cumsum(padded)
    pad_start = pad_end - padded
    n_used = (pad_end[-1] // bm).astype(jnp.int32).reshape(1)
    block_e = jnp.minimum(jnp.sum(pad_end[None, :] <= (jnp.arange(n_blocks) * bm)[:, None], axis=1),
                          N_EXPERTS - 1).astype(jnp.int32)
    e_sel = r_int[0:TOP_K].T
    start_sel = jnp.sum(jnp.where(e_sel[..., None] == jnp.arange(N_EXPERTS), pad_start, 0), axis=-1)
    slots = (start_sel + r_int[TOP_K:2 * TOP_K].T).reshape(-1).astype(jnp.int32)

    nonempty = cnt > 0
    expert_seq = (jnp.cumsum(nonempty) - 1).astype(jnp.int32)
    ids = jnp.arange(N_EXPERTS)
    later = (ids[None, :] > ids[:, None]) & nonempty[None, :]
    expert_next = jnp.where(later.any(axis=1), jnp.argmax(later, axis=1), -1).astype(jnp.int32)
    ys = _moe(block_e, n_used, slots, (pad_start + cnt).astype(jnp.int32), pad_end.astype(jnp.int32),
              expert_seq, expert_next, xn,
              wg_b.reshape(w_gate_e.shape), wu_b.reshape(w_up_e.shape), w_down_e.astype(F32), n_blocks * bm)
    return _combine(slots, h, r_w[0:TOP_K].T, ys)


def kernel(x, ln1_w, w_in, q_norm_w, k_norm_w, attn_sink, attn_out_norm_w, gla_gate_up_f, gla_gate_bias_f,
           gla_gate_up_b, gla_gate_bias_b, gla_out_norm_w, w_out, ln2_w, w_group, b_group, w_router,
           b_router, w_gate_e, w_up_e, w_down_e):
    B, S, D = x.shape
    h = x.reshape(B * S, D)
    assert B == 1
    for l in range(ln1_w.shape[0]):
        h = _layer(h, ln1_w[l], w_in[l], q_norm_w[l], k_norm_w[l], attn_sink[l], attn_out_norm_w[l],
                   gla_gate_up_f[l], gla_gate_bias_f[l], gla_gate_up_b[l], gla_gate_bias_b[l],
                   gla_out_norm_w[l], w_out[l], ln2_w[l], w_group[l], b_group[l], w_router[l],
                   b_router[l], w_gate_e[l], w_up_e[l], w_down_e[l])
    return h.reshape(B, S, D)
```

```python
import functools
from typing import NamedTuple

import jax
import jax.numpy as jnp
from jax import lax
from jax.experimental import pallas as pl
from jax.experimental.pallas import tpu as pltpu

F32 = jnp.float32
BF16 = jnp.bfloat16

EPS = 1e-6
D_MODEL = 2048

ATTN_Q_HEADS = 8
ATTN_KV_HEADS = 2
ATTN_GROUP = ATTN_Q_HEADS // ATTN_KV_HEADS
HEAD_DIM = 128
WINDOW = 128
ATTN_BLOCK = 128
ATTN_STEP_BLOCKS = 4
ROPE_THETA = 500000.0
ROPE_DIM = HEAD_DIM // 4
ROPE_HALF = ROPE_DIM // 2
ATTN_WIDTH = ATTN_Q_HEADS * HEAD_DIM
KV_WIDTH = ATTN_KV_HEADS * HEAD_DIM
A_WIDTH = ATTN_WIDTH + 2 * KV_WIDTH

GLA_HEADS = 4
GLA_DK = 128
GLA_DV = 256
GLA_RANK = 16
GLA_TAU = 16.0
GLA_QK_WIDTH = GLA_HEADS * GLA_DK
GLA_WIDTH = GLA_HEADS * GLA_DV
GLA_CHUNK = 128
GLA_STEP_CHUNKS = 4

N_GROUPS = 4
EXPERTS_PER_GROUP = 8
N_EXPERTS = N_GROUPS * EXPERTS_PER_GROUP
TOP_K = 2
D_FF = 1024
ROUTER_ROWS = 128
EXPERT_ROW0 = 8

PROJ_ROWS = 256
OUT_ROWS = 256
MOE_ROWS = 256
MOE_ROW_BUFFERS = 3
MOE_CAST_ROWS = 128
COMBINE_ROWS = 256

VMEM_LIMIT = 56 * 1024 * 1024


def _dot(a, b):
    return jnp.dot(a, b, preferred_element_type=F32)


def _dot_nt(a, b):
    return lax.dot_general(a, b, (((1,), (1,)), ((), ())), preferred_element_type=F32)


def _dot_tn(a, b):
    return lax.dot_general(a, b, (((0,), (0,)), ((), ())), preferred_element_type=F32)


def _pack_bf16_pairs(x):
    n = x.shape[1] // 2
    lo = lax.bitcast_convert_type(x[:, :n].astype(F32), jnp.uint32) >> 16
    hi = lax.bitcast_convert_type(x[:, n:].astype(F32), jnp.uint32) & jnp.uint32(0xFFFF0000)
    return lo | hi


def _unpack_bf16_pairs(w):
    lo = lax.bitcast_convert_type(w << 16, F32)
    hi = lax.bitcast_convert_type(w & jnp.uint32(0xFFFF0000), F32)
    return lo, hi


class CastJob(NamedTuple):
    array: jax.Array
    in_spec: pl.BlockSpec
    out_spec: pl.BlockSpec
    out_shape: jax.ShapeDtypeStruct


def _cast_rows_job(w, n_steps):
    w2 = w.astype(F32).reshape(-1, w.shape[-1])
    rows = w2.shape[0] // n_steps
    assert rows * n_steps == w2.shape[0] and rows % 16 == 0
    spec = pl.BlockSpec((rows, w2.shape[1]), lambda i, *_: (i, 0))
    return CastJob(w2, spec, spec, jax.ShapeDtypeStruct(w2.shape, BF16))


def _resident(shape):
    nd = len(shape)
    return pl.BlockSpec(shape, lambda *_: (0,) * nd, pipeline_mode=pl.Buffered(1))


def _inproj_body(x_ref, ln_ref, wa_ref, wqk_ref, wv_ref, wr_ref, wl_ref, qn_ref, kn_ref,
                 step_cos_ref, step_sin_ref, row_cos_ref, row_sin_ref, wcast_ref,
                 oa_ref, oqk_ref, ov_ref, or_ref, ol_ref, wcast_out_ref):
    wcast_out_ref[...] = wcast_ref[...].astype(BF16)
    x = x_ref[...]
    ms = jnp.mean(x * x, axis=-1, keepdims=True)
    xn = (x * lax.rsqrt(ms + EPS) * ln_ref[...]).astype(BF16)
    acc = _dot(xn, wa_ref[...])
    ca, sn = step_cos_ref[0], step_sin_ref[0]
    cb, sb_ = row_cos_ref[...], row_sin_ref[...]
    cos = cb * ca - sb_ * sn
    sin = sb_ * ca + cb * sn
    lane = lax.broadcasted_iota(jnp.int32, (1, HEAD_DIM), 1)
    sa = jnp.where((lane >= ROPE_HALF) & (lane < ROPE_DIM), sin, 0.0)
    sb = jnp.where(lane < ROPE_HALF, -sin, 0.0)
    for c in range(ATTN_Q_HEADS + ATTN_KV_HEADS):
        xh = acc[:, c * HEAD_DIM:(c + 1) * HEAD_DIM]
        w = qn_ref[...] if c < ATTN_Q_HEADS else kn_ref[...]
        y = xh * lax.rsqrt(jnp.mean(xh * xh, axis=-1, keepdims=True) + EPS) * w
        y = (y * cos + pltpu.roll(y, ROPE_HALF, 1) * sa
             + pltpu.roll(y, HEAD_DIM - ROPE_HALF, 1) * sb)
        if c < ATTN_Q_HEADS:
            y = y * (HEAD_DIM ** -0.5)
        oa_ref[:, c * HEAD_DIM:(c + 1) * HEAD_DIM] = y.astype(BF16)
    oa_ref[:, ATTN_WIDTH + KV_WIDTH:] = acc[:, ATTN_WIDTH + KV_WIDTH:].astype(BF16)
    oqk_ref[...] = _dot(xn, wqk_ref[...]).astype(BF16)
    ov_ref[...] = _dot(xn, wv_ref[...]).astype(BF16)
    or_ref[...] = _dot(xn, wr_ref[...]).astype(BF16)
    ol_ref[...] = _dot(xn, wl_ref[...])


def _inproj(x2, ln_w, wa, wqk, wv, wr, wl, qn, kn, rope, cast):
    T = x2.shape[0]
    tm = PROJ_ROWS
    row = lambda w: pl.BlockSpec((tm, w), lambda i: (i, 0))
    step_row = pl.BlockSpec((1, 1, HEAD_DIM), lambda i: (i, 0, 0))
    return pl.pallas_call(
        _inproj_body,
        grid=(T // tm,),
        in_specs=[row(D_MODEL), _resident((1, D_MODEL)),
                  _resident(wa.shape), _resident(wqk.shape), _resident(wv.shape),
                  _resident(wr.shape), _resident(wl.shape),
                  _resident((1, HEAD_DIM)), _resident((1, HEAD_DIM)),
                  step_row, step_row, _resident((tm, HEAD_DIM)), _resident((tm, HEAD_DIM)), cast.in_spec],
        out_specs=[row(A_WIDTH), row(2 * GLA_QK_WIDTH), row(GLA_WIDTH), row(GLA_WIDTH),
                   row(2 * GLA_RANK), cast.out_spec],
        out_shape=[jax.ShapeDtypeStruct((T, A_WIDTH), BF16),
                   jax.ShapeDtypeStruct((T, 2 * GLA_QK_WIDTH), BF16),
                   jax.ShapeDtypeStruct((T, GLA_WIDTH), BF16),
                   jax.ShapeDtypeStruct((T, GLA_WIDTH), BF16),
                   jax.ShapeDtypeStruct((T, 2 * GLA_RANK), F32), cast.out_shape],
        compiler_params=pltpu.CompilerParams(dimension_semantics=("parallel",),
                                             vmem_limit_bytes=VMEM_LIMIT),
        name="inproj",
    )(x2, ln_w, wa, wqk, wv, wr, wl, qn, kn, *rope, cast.array)


def _attn_body(sink_ref, q_ref, kp_ref, kc_ref, kn_ref, vp_ref, vc_ref, vn_ref, nw_ref, o_ref):
    n = pl.program_id(0)
    nb = pl.num_programs(0)
    rows = ATTN_GROUP * ATTN_BLOCK
    keys = 3 * ATTN_BLOCK
    r = lax.broadcasted_iota(jnp.int32, (rows, keys), 0) & (ATTN_BLOCK - 1)
    c = lax.broadcasted_iota(jnp.int32, (rows, keys), 1)
    band = (c >= r + (ATTN_BLOCK - WINDOW)) & (c <= r + (ATTN_BLOCK + WINDOW))
    edge_mask = [band & ((c >= ATTN_BLOCK) | (n > 0))] + [band] * (ATTN_STEP_BLOCKS - 2) + \
                [band & ((c < 2 * ATTN_BLOCK) | (n < nb - 1))]

    def window(j, prev_ref, cur_ref, next_ref, ks):
        blocks = [prev_ref[:, ks]] + [cur_ref[i * ATTN_BLOCK:(i + 1) * ATTN_BLOCK, ks]
                                      for i in range(ATTN_STEP_BLOCKS)] + [next_ref[:, ks]]
        return jnp.concatenate(blocks[j:j + 3], axis=0)

    chains = [(j, g) for j in range(ATTN_STEP_BLOCKS) for g in range(ATTN_KV_HEADS)]
    heads_of = lambda g: range(g * ATTN_GROUP, (g + 1) * ATTN_GROUP)
    ks_of = lambda g: slice(g * HEAD_DIM, (g + 1) * HEAD_DIM)
    qrows_of = lambda j: slice(j * ATTN_BLOCK, (j + 1) * ATTN_BLOCK)
    s = [jnp.where(edge_mask[j],
                   _dot_nt(jnp.concatenate([q_ref[qrows_of(j), h * HEAD_DIM:(h + 1) * HEAD_DIM]
                                            for h in heads_of(g)], axis=0),
                           window(j, kp_ref, kc_ref, kn_ref, ks_of(g))), -jnp.inf)
         for j, g in chains]
    lane_blocks = keys // HEAD_DIM
    sink = [jnp.concatenate([jnp.full((ATTN_BLOCK, HEAD_DIM), sink_ref[h], F32) for h in heads_of(g)], axis=0)
            for j, g in chains]
    m = [jnp.maximum(jnp.broadcast_to(jnp.max(si, axis=-1, keepdims=True), (rows, HEAD_DIM)), sk)
         for si, sk in zip(s, sink)]
    p = [jnp.concatenate([jnp.exp(si[:, c * HEAD_DIM:(c + 1) * HEAD_DIM] - mi) for c in range(lane_blocks)]
                         + [jnp.exp(sk - mi)], axis=1).astype(BF16)
         for si, sk, mi in zip(s, sink, m)]
    sink_rows = jnp.concatenate([jnp.zeros((HEAD_DIM, HEAD_DIM), BF16),
                                 jnp.full((HEAD_DIM, HEAD_DIM), 1.0 / HEAD_DIM, BF16)], axis=1)

    def values_and_ones(j, g):
        v3 = window(j, vp_ref, vc_ref, vn_ref, ks_of(g))
        return jnp.concatenate([jnp.concatenate([v3, jnp.ones((keys, HEAD_DIM), BF16)], axis=1), sink_rows],
                               axis=0)

    o = []
    for (j, g), pi in zip(chains, p):
        num_den = _dot(pi, values_and_ones(j, g))
        o.append(num_den[:, :HEAD_DIM] / num_den[:, HEAD_DIM:])
    for j in range(ATTN_STEP_BLOCKS):
        outs = [o[chains.index((j, g))][i * ATTN_BLOCK:(i + 1) * ATTN_BLOCK]
                for g in range(ATTN_KV_HEADS) for i in range(ATTN_GROUP)]
        a = jnp.concatenate(outs, axis=1)
        a = a * lax.rsqrt(jnp.mean(a * a, axis=-1, keepdims=True) + EPS) * nw_ref[...]
        o_ref[qrows_of(j), :] = a.astype(BF16)


def _attention(pa, sink, norm_w):
    T = pa.shape[0]
    sb = ATTN_STEP_BLOCKS
    nb = T // ATTN_BLOCK
    kcol = ATTN_WIDTH // KV_WIDTH
    vcol = kcol + 1
    prev = lambda col: pl.BlockSpec((ATTN_BLOCK, KV_WIDTH), lambda n, s: (jnp.maximum(sb * n - 1, 0), col))
    nxt = lambda col: pl.BlockSpec((ATTN_BLOCK, KV_WIDTH),
                                   lambda n, s: (jnp.minimum(sb * n + sb, nb - 1), col))
    cur = lambda col: pl.BlockSpec((sb * ATTN_BLOCK, KV_WIDTH), lambda n, s: (n, col))
    grid_spec = pltpu.PrefetchScalarGridSpec(
        num_scalar_prefetch=1,
        grid=(nb // sb,),
        in_specs=[pl.BlockSpec((sb * ATTN_BLOCK, ATTN_WIDTH), lambda n, s: (n, 0)),
                  prev(kcol), cur(kcol), nxt(kcol), prev(vcol), cur(vcol), nxt(vcol),
                  pl.BlockSpec((1, ATTN_WIDTH), lambda n, s: (0, 0))],
        out_specs=pl.BlockSpec((sb * ATTN_BLOCK, ATTN_WIDTH), lambda n, s: (n, 0)),
    )
    return pl.pallas_call(
        _attn_body,
        grid_spec=grid_spec,
        out_shape=jax.ShapeDtypeStruct((T, ATTN_WIDTH), BF16),
        compiler_params=pltpu.CompilerParams(dimension_semantics=("parallel",),
                                             vmem_limit_bytes=VMEM_LIMIT),
        name="attention",
    )(sink, pa, pa, pa, pa, pa, pa, pa, norm_w)


def _gla_body(reverse, final, *refs):
    if final:
        (q_ref, k_ref, v_ref, lr_ref, up_ref, bias_ref, tri_ref, of_ref, gr_ref, nw_ref,
         o_ref, st_ref) = refs
    else:
        q_ref, k_ref, v_ref, lr_ref, up_ref, bias_ref, tri_ref, o_ref, st_ref = refs
    C = GLA_CHUNK
    R = GLA_STEP_CHUNKS

    @pl.when(pl.program_id(0) == 0)
    def _():
        st_ref[...] = jnp.zeros_like(st_ref)

    chunks = range(R)
    order = tuple(reversed(chunks)) if reverse else tuple(chunks)
    rows = [slice(s * C, (s + 1) * C) for s in chunks]
    heads = range(GLA_HEADS)
    ks = [slice(h * GLA_DK, (h + 1) * GLA_DK) for h in heads]
    vs = [slice(h * GLA_DV, (h + 1) * GLA_DV) for h in heads]

    lr = lr_ref[...]
    lr_hi = lr.astype(BF16)
    lr_lo = (lr - lr_hi.astype(F32)).astype(BF16)
    g = _dot(jnp.concatenate([lr_hi, lr_lo, lr_hi], axis=1), up_ref[...]) + bias_ref[...]
    la = jax.nn.log_sigmoid(g) * (1.0 / GLA_TAU)
    tri = tri_ref[...]
    hi, lo = _split_hi_lo(la)
    b = [_dot(tri, hi[rows[s]]) + _dot(tri, lo[rows[s]]) for s in chunks]
    end = 0 if reverse else C - 1
    b_end = [b[s][end:end + 1] for s in chunks]
    b_mid = [b[s][C // 2:C // 2 + 1] for s in chunks]
    scale = GLA_DK ** -0.5
    q = [q_ref[rows[s], :].astype(F32) * scale for s in chunks]
    k = [k_ref[rows[s], :].astype(F32) for s in chunks]
    q_in = [(q[s] * jnp.exp(b[s] - b_mid[s])).astype(BF16) for s in chunks]
    k_in = [(k[s] * jnp.exp(b_mid[s] - b[s])).astype(BF16) for s in chunks]
    q_dec = [(q[s] * jnp.exp(b[s])).astype(BF16) for s in chunks]
    k_end = [(k[s] * jnp.exp(b_end[s] - b[s])).astype(BF16) for s in chunks]
    decay = [jnp.exp(b_end[s]) for s in chunks]
    ri = lax.broadcasted_iota(jnp.int32, (C, C), 0)
    ci = lax.broadcasted_iota(jnp.int32, (C, C), 1)
    causal = (ri <= ci) if reverse else (ri >= ci)
    att = [[jnp.where(causal, _dot_nt(q_in[s][:, ks[h]], k_in[s][:, ks[h]]), 0.0).astype(BF16)
            for h in heads] for s in chunks]
    intra = [[_dot(att[s][h], v_ref[rows[s], vs[h]]) for h in heads] for s in chunks]
    gain = [[_dot_tn(v_ref[rows[s], vs[h]], k_end[s][:, ks[h]]) for h in heads] for s in chunks]
    st = [st_ref[h] for h in heads]
    o = [[None] * GLA_HEADS for _ in chunks]
    for s in order:
        for h in heads:
            o[s][h] = intra[s][h] + _dot_nt(q_dec[s][:, ks[h]], st[h].astype(BF16))
            st[h] = st[h] * decay[s][:, ks[h]] + gain[s][h]
    for h in heads:
        st_ref[h] = st[h]
    for s in chunks:
        if final:
            tot = [o[s][h] + of_ref[rows[s], vs[h]] for h in heads]
            inv = [lax.rsqrt(jnp.mean(t * t, axis=-1, keepdims=True) + EPS) for t in tot]
            for h in heads:
                y = tot[h] * inv[h] * nw_ref[...]
                o_ref[rows[s], vs[h]] = (y * jax.nn.silu(gr_ref[rows[s], vs[h]].astype(F32))).astype(BF16)
        else:
            for h in heads:
                o_ref[rows[s], vs[h]] = o[s][h]


def _gla_pass(reverse, pqk, pv, plr, up, bias, tri, extra=None):
    T = pqk.shape[0]
    rows = GLA_STEP_CHUNKS * GLA_CHUNK
    n = T // rows
    final = extra is not None
    blk = (lambda i: n - 1 - i) if reverse else (lambda i: i)
    row = lambda w, col=0: pl.BlockSpec((rows, w), lambda i: (blk(i), col))
    in_specs = [row(GLA_QK_WIDTH, 0), row(GLA_QK_WIDTH, 1), row(GLA_WIDTH), row(2 * GLA_RANK),
                _resident(up.shape), _resident(bias.shape), _resident(tri.shape)]
    args = [pqk, pqk, pv, plr, up, bias, tri]
    if final:
        o_f, pr, norm_w = extra
        in_specs += [row(GLA_WIDTH), row(GLA_WIDTH), _resident(norm_w.shape)]
        args += [o_f, pr, norm_w]
    return pl.pallas_call(
        functools.partial(_gla_body, reverse, final),
        grid=(n,),
        in_specs=in_specs,
        out_specs=row(GLA_WIDTH),
        out_shape=jax.ShapeDtypeStruct((T, GLA_WIDTH), BF16 if final else F32),
        scratch_shapes=[pltpu.VMEM((GLA_HEADS, GLA_DV, GLA_DK), F32)],
        compiler_params=pltpu.CompilerParams(dimension_semantics=("arbitrary",),
                                             vmem_limit_bytes=VMEM_LIMIT),
        name="gla_bwd" if reverse else "gla_fwd",
    )(*args)


def _outproj_body(a_ref, g_ref, x_ref, w_ref, ln_ref, wr_hl_ref, wr_hi_ref, rb_ref, tri_ref,
                  h_ref, xn_ref, ri_ref, rw_ref, cnt_ref):
    tm = x_ref.shape[0]

    @pl.when(pl.program_id(0) == 0)
    def _():
        cnt_ref[...] = jnp.zeros_like(cnt_ref)

    h = x_ref[...] + _dot(jnp.concatenate([a_ref[...], g_ref[...]], axis=1), w_ref[...])
    h_ref[...] = h
    xn = h * lax.rsqrt(jnp.mean(h * h, axis=-1, keepdims=True) + EPS) * ln_ref[...]
    x_hi = xn.astype(BF16)
    xn_ref[...] = _pack_bf16_pairs(x_hi)
    x_lo = (xn - x_hi.astype(F32)).astype(BF16)
    l2 = _dot(x_hi, wr_hl_ref[...])
    logits = l2[:, :ROUTER_ROWS] + l2[:, ROUTER_ROWS:] + _dot(x_lo, wr_hi_ref[...])
    lt = jnp.transpose(logits) + rb_ref[...]

    gl = lt[0:N_GROUPS]
    gmax = jnp.max(gl, axis=0, keepdims=True)
    gi = lax.broadcasted_iota(jnp.int32, gl.shape, 0).astype(F32)
    g_sel = jnp.min(jnp.where(gl == gmax, gi, float(N_GROUPS)), axis=0, keepdims=True)
    g_gate = 1.0 / jnp.sum(jnp.exp(gl - gmax), axis=0, keepdims=True)

    el = lt[EXPERT_ROW0:EXPERT_ROW0 + N_EXPERTS]
    ei_int = lax.broadcasted_iota(jnp.int32, el.shape, 0)
    ei = ei_int.astype(F32)
    grp = (ei_int >> 3).astype(F32)
    cand = jnp.where(grp == g_sel, el, -jnp.inf)
    v1 = jnp.max(cand, axis=0, keepdims=True)
    e1 = jnp.min(jnp.where(cand == v1, ei, float(N_EXPERTS)), axis=0, keepdims=True)
    cand2 = jnp.where(ei == e1, -jnp.inf, cand)
    v2 = jnp.max(cand2, axis=0, keepdims=True)
    e2 = jnp.min(jnp.where(cand2 == v2, ei, float(N_EXPERTS)), axis=0, keepdims=True)
    d = jnp.exp(v2 - v1)
    w1 = g_gate / (1.0 + d)
    w2 = g_gate * d / (1.0 + d)

    oh1 = (ei == e1).astype(F32)
    oh2 = (ei == e2).astype(F32)
    cnt = oh1 + oh2
    before = _dot(cnt.astype(BF16), tri_ref[...]) + cnt_ref[:, 0:1]
    r1 = jnp.sum(oh1 * before, axis=0, keepdims=True)
    r2 = jnp.sum(oh2 * before, axis=0, keepdims=True)
    cnt_ref[...] = cnt_ref[...] + jnp.sum(cnt, axis=1, keepdims=True)

    ri_ref[...] = jnp.concatenate([e1, e2, r1, r2, jnp.zeros((4, tm), F32)], axis=0).astype(jnp.int32)
    rw_ref[...] = jnp.concatenate([w1, w2, jnp.zeros((6, tm), F32)], axis=0)


def _outproj_router(attn, gla, x2, w_out, ln_w, wr_hl, wr_hi, rbias, tri):
    T = x2.shape[0]
    tm = OUT_ROWS
    row = lambda w: pl.BlockSpec((tm, w), lambda i: (i, 0))
    col = lambda r: pl.BlockSpec((r, tm), lambda i: (0, i))
    return pl.pallas_call(
        _outproj_body,
        grid=(T // tm,),
        in_specs=[row(ATTN_WIDTH), row(GLA_WIDTH), row(D_MODEL),
                  _resident(w_out.shape), _resident(ln_w.shape),
                  _resident(wr_hl.shape), _resident(wr_hi.shape), _resident(rbias.shape),
                  _resident(tri.shape)],
        out_specs=[row(D_MODEL), row(D_MODEL // 2), col(8), col(8),
                   pl.BlockSpec((N_EXPERTS, 128), lambda i: (0, 0))],
        out_shape=[jax.ShapeDtypeStruct((T, D_MODEL), F32),
                   jax.ShapeDtypeStruct((T, D_MODEL // 2), jnp.uint32),
                   jax.ShapeDtypeStruct((8, T), jnp.int32),
                   jax.ShapeDtypeStruct((8, T), F32),
                   jax.ShapeDtypeStruct((N_EXPERTS, 128), F32)],
        compiler_params=pltpu.CompilerParams(dimension_semantics=("arbitrary",),
                                             vmem_limit_bytes=VMEM_LIMIT),
        name="outproj_router",
    )(attn, gla, x2, w_out, ln_w, wr_hl, wr_hi, rbias, tri)


def _row_copy_start(src, s, dst, d, sem, queue):
    pltpu.async_copy(src.at[pl.ds(s, 1)], dst.at[pl.ds(d, 1)], sem, priority=queue)


def _moe_body(be_ref, nu_ref, slot_ref, lo_ref, hi_ref, seq_ref, nxt_ref, xn_ref, wg_ref, wu_ref, wd_ref,
              y_ref, xbuf, xb_ref, tok_ref, wg_buf, wu_buf, wd_buf, wu_stage, wd_stage, sem, wsem, dsem):
    bm = MOE_ROWS
    b = pl.program_id(0)
    n_used = nu_ref[0]
    cur = b % MOE_ROW_BUFFERS
    expert = be_ref[b]
    wslot = seq_ref[expert] % 2
    first_of_expert = (b == 0) | (be_ref[jnp.maximum(b - 1, 0)] != expert)

    def weight_copies(e, slot):
        return [pltpu.make_async_copy(wg_ref.at[e], wg_buf.at[slot], wsem.at[slot]),
                pltpu.make_async_copy(wu_ref.at[e], wu_stage, dsem.at[0]),
                pltpu.make_async_copy(wd_ref.at[e], wd_stage, dsem.at[1])]

    def gather_start(block, buf, unrolled):
        def one(i, queue):
            _row_copy_start(xn_ref, tok_ref[block * bm + i], xbuf.at[buf], i, sem.at[buf], queue)
        if unrolled:
            for i in range(bm):
                one(i, i % 2)
        else:
            def pair(p, c):
                one(2 * p, 0)
                one(2 * p + 1, 1)
                return c
            lax.fori_loop(0, bm // 2, pair, 0)

    def gather_wait(buf):
        pltpu.make_async_copy(xn_ref.at[pl.ds(0, bm)], xbuf.at[buf], sem.at[buf]).wait()

    @pl.when(b == 0)
    def _():
        def per_expert(e, c):
            def zero(s, c2):
                tok_ref[s] = 0
                return c2
            return lax.fori_loop(lo_ref[e], hi_ref[e], zero, c)
        lax.fori_loop(0, N_EXPERTS, per_expert, 0)

        def fill(t, c):
            for k in range(TOP_K):
                tok_ref[slot_ref[TOP_K * t + k]] = t
            return c
        lax.fori_loop(0, slot_ref.shape[0] // TOP_K, fill, 0, unroll=8)
        gather_start(0, 0, False)

        @pl.when(n_used > 1)
        def _():
            gather_start(1, 1, False)

    @pl.when((b < n_used) & first_of_expert)
    def _():
        @pl.when(b == 0)
        def _():
            for c in weight_copies(expert, wslot):
                c.start()
        for c in weight_copies(expert, wslot):
            c.wait()
        for r in range(0, D_MODEL, MOE_CAST_ROWS):
            wu_buf[wslot, r:r + MOE_CAST_ROWS, :] = wu_stage[r:r + MOE_CAST_ROWS, :].astype(BF16)
        for r in range(0, D_FF, MOE_CAST_ROWS):
            wd_buf[wslot, r:r + MOE_CAST_ROWS, :] = wd_stage[r:r + MOE_CAST_ROWS, :].astype(BF16)
        following = nxt_ref[expert]

        @pl.when(following >= 0)
        def _():
            for c in weight_copies(following, 1 - wslot):
                c.start()

    def stage_rows():
        gather_wait(cur)
        lo, hi = _unpack_bf16_pairs(xbuf[cur])
        xb_ref[:, :D_MODEL // 2] = lo.astype(BF16)
        xb_ref[:, D_MODEL // 2:] = hi.astype(BF16)

    def experts():
        x = xb_ref[...]
        hid = jax.nn.silu(_dot(x, wg_buf[wslot])) * _dot(x, wu_buf[wslot])
        y = _dot(hid.astype(BF16), wd_buf[wslot])
        y_ref[...] = _pack_bf16_pairs(y.astype(BF16))

    @pl.when(b + 2 < n_used)
    def _():
        stage_rows()
        gather_start(b + 2, (b + 2) % MOE_ROW_BUFFERS, True)
        experts()

    @pl.when((b < n_used) & (b + 2 >= n_used))
    def _():
        stage_rows()
        experts()

    @pl.when(b >= n_used)
    def _():
        y_ref[...] = jnp.zeros_like(y_ref)


def _moe(block_e, n_used, slots, pad_lo, pad_hi, expert_seq, expert_next, xn, wg, wu, wd, n_slots):
    bm = MOE_ROWS
    any_spec = pl.BlockSpec(memory_space=pl.ANY)
    grid_spec = pltpu.PrefetchScalarGridSpec(
        num_scalar_prefetch=7,
        grid=(n_slots // bm,),
        in_specs=[any_spec] * 4,
        out_specs=pl.BlockSpec((bm, D_MODEL // 2), lambda b, *_: (b, 0)),
        scratch_shapes=[pltpu.VMEM((MOE_ROW_BUFFERS, bm, D_MODEL // 2), jnp.uint32),
                        pltpu.VMEM((bm, D_MODEL), BF16),
                        pltpu.SMEM((n_slots,), jnp.int32),
                        pltpu.VMEM((2, D_MODEL, D_FF), BF16), pltpu.VMEM((2, D_MODEL, D_FF), BF16),
                        pltpu.VMEM((2, D_FF, D_MODEL), BF16),
                        pltpu.VMEM((D_MODEL, D_FF), F32), pltpu.VMEM((D_FF, D_MODEL), F32),
                        pltpu.SemaphoreType.DMA((MOE_ROW_BUFFERS,)), pltpu.SemaphoreType.DMA((2,)),
                        pltpu.SemaphoreType.DMA((2,))],
    )
    return pl.pallas_call(
        _moe_body,
        grid_spec=grid_spec,
        out_shape=jax.ShapeDtypeStruct((n_slots, D_MODEL // 2), jnp.uint32),
        compiler_params=pltpu.CompilerParams(dimension_semantics=("arbitrary",),
                                             vmem_limit_bytes=VMEM_LIMIT),
        name="moe",
    )(block_e, n_used, slots, pad_lo, pad_hi, expert_seq, expert_next, xn, wg, wu, wd)


def _combine_body(slot_ref, h_ref, w_ref, ys_ref, o_ref, g_ref, sem):
    tb = COMBINE_ROWS
    i = pl.program_id(0)
    cur = i % 2

    def gather_start(block, buf):
        def one(t, c):
            for k in range(TOP_K):
                _row_copy_start(ys_ref, slot_ref[TOP_K * (block * tb + t) + k], g_ref.at[buf, k], t,
                                sem.at[buf], k)
            return c
        lax.fori_loop(0, tb, one, 0, unroll=8)

    @pl.when(i == 0)
    def _():
        gather_start(0, 0)

    @pl.when(i + 1 < pl.num_programs(0))
    def _():
        gather_start(i + 1, 1 - cur)

    for k in range(TOP_K):
        pltpu.make_async_copy(ys_ref.at[pl.ds(0, tb)], g_ref.at[cur, k], sem.at[cur]).wait()
    lo0, hi0 = _unpack_bf16_pairs(g_ref[cur, 0])
    lo1, hi1 = _unpack_bf16_pairs(g_ref[cur, 1])
    w0, w1 = w_ref[:, 0:1], w_ref[:, 1:2]
    half = D_MODEL // 2
    o_ref[:, :half] = h_ref[:, :half] + w0 * lo0 + w1 * lo1
    o_ref[:, half:] = h_ref[:, half:] + w0 * hi0 + w1 * hi1


def _combine(slots, h, wcol, ys):
    T = h.shape[0]
    tb = COMBINE_ROWS
    grid_spec = pltpu.PrefetchScalarGridSpec(
        num_scalar_prefetch=1,
        grid=(T // tb,),
        in_specs=[pl.BlockSpec((tb, D_MODEL), lambda i, s: (i, 0)),
                  pl.BlockSpec((tb, TOP_K), lambda i, s: (i, 0)),
                  pl.BlockSpec(memory_space=pl.ANY)],
        out_specs=pl.BlockSpec((tb, D_MODEL), lambda i, s: (i, 0)),
        scratch_shapes=[pltpu.VMEM((2, TOP_K, tb, D_MODEL // 2), jnp.uint32),
                        pltpu.SemaphoreType.DMA((2,))],
    )
    return pl.pallas_call(
        _combine_body,
        grid_spec=grid_spec,
        out_shape=jax.ShapeDtypeStruct((T, D_MODEL), F32),
        compiler_params=pltpu.CompilerParams(dimension_semantics=("arbitrary",),
                                             vmem_limit_bytes=VMEM_LIMIT),
        name="combine",
    )(slots, h, wcol, ys)


def _rope_tables(T, rows):
    inv_freq = jnp.power(jnp.float32(ROPE_THETA),
                         -jnp.arange(ROPE_HALF, dtype=F32) * (2.0 / ROPE_DIM))

    def tables(pos):
        ang = pos.astype(F32)[:, None] * inv_freq[None, :]
        rest = (pos.shape[0], HEAD_DIM - ROPE_DIM)
        cos = jnp.concatenate([jnp.cos(ang), jnp.cos(ang), jnp.ones(rest, F32)], axis=1)
        sin = jnp.concatenate([jnp.sin(ang), jnp.sin(ang), jnp.zeros(rest, F32)], axis=1)
        return cos, sin

    step_cos, step_sin = tables(jnp.arange(T // rows) * rows)
    row_cos, row_sin = tables(jnp.arange(rows))
    return step_cos[:, None, :], step_sin[:, None, :], row_cos, row_sin


def _split_hi_lo(w):
    hi = w.astype(BF16)
    lo = (w - hi.astype(F32)).astype(BF16)
    return hi, lo


def _layer(x2, ln1_w, w_in, q_norm_w, k_norm_w, attn_sink, attn_out_norm_w, gate_up_f, gate_bias_f,
           gate_up_b, gate_bias_b, gla_out_norm_w, w_out, ln2_w, w_group, b_group, w_router, b_router,
           w_gate_e, w_up_e, w_down_e):
    T = x2.shape[0]
    row = lambda v: v.reshape(1, -1).astype(F32)

    c0, c1, c2, c3 = A_WIDTH, A_WIDTH + 2 * GLA_QK_WIDTH, A_WIDTH + 2 * GLA_QK_WIDTH + GLA_WIDTH, \
        A_WIDTH + 2 * GLA_QK_WIDTH + 2 * GLA_WIDTH
    wa, wqk, wv, wr, wl = (w_in[:, a:b].astype(BF16) for a, b in
                           ((0, c0), (c0, c1), (c1, c2), (c2, c3), (c3, w_in.shape[1])))
    pa, pqk, pv, pr, plr, wg_b = _inproj(x2, row(ln1_w), wa, wqk, wv, wr, wl, row(q_norm_w), row(k_norm_w),
                                         _rope_tables(T, PROJ_ROWS), _cast_rows_job(w_gate_e, T // PROJ_ROWS))

    attn = _attention(pa, attn_sink.astype(F32), row(attn_out_norm_w))

    C = GLA_CHUNK
    ones = jnp.ones((C, C), F32)
    zr = jnp.zeros((GLA_RANK, GLA_QK_WIDTH), F32)
    up_f = jnp.concatenate([gate_up_f.astype(F32), zr], axis=0)
    up_b = jnp.concatenate([zr, gate_up_b.astype(F32)], axis=0)

    def up_pieces(up):
        hi, lo = _split_hi_lo(up)
        return jnp.concatenate([hi, hi, lo], axis=0)

    o_f = _gla_pass(False, pqk, pv, plr, up_pieces(up_f), row(gate_bias_f), jnp.tril(ones).astype(BF16))
    gla = _gla_pass(True, pqk, pv, plr, up_pieces(up_b), row(gate_bias_b), jnp.triu(ones).astype(BF16),
                    extra=(o_f, pr, row(gla_out_norm_w)))

    wr_full = jnp.zeros((D_MODEL, ROUTER_ROWS), F32)
    wr_full = wr_full.at[:, :N_GROUPS].set(w_group.astype(F32))
    wr_full = wr_full.at[:, EXPERT_ROW0:EXPERT_ROW0 + N_EXPERTS].set(w_router.astype(F32))
    wr_hi, wr_lo = _split_hi_lo(wr_full)
    rbias = jnp.zeros((ROUTER_ROWS, 1), F32)
    rbias = rbias.at[:N_GROUPS, 0].set(b_group.astype(F32))
    rbias = rbias.at[EXPERT_ROW0:EXPERT_ROW0 + N_EXPERTS, 0].set(b_router.astype(F32))
    tm = OUT_ROWS
    earlier = jnp.triu(jnp.ones((tm, tm), F32), k=1).astype(BF16)
    h, xn, r_int, r_w, counts = _outproj_router(
        attn, gla, x2, w_out.astype(BF16), row(ln2_w),
        jnp.concatenate([wr_hi, wr_lo], axis=1), wr_hi, rbias, earlier)

    bm = MOE_ROWS
    n_blocks = (T * TOP_K) // bm + N_EXPERTS
    cnt = counts[:, 0].astype(jnp.int32)
    padded = (cnt + bm - 1) // bm * bm
    pad_end = jnp.cumsum(padded)
    pad_start = pad_end - padded
    n_used = (pad_end[-1] // bm).astype(jnp.int32).reshape(1)
    block_e = jnp.minimum(jnp.sum(pad_end[None, :] <= (jnp.arange(n_blocks) * bm)[:, None], axis=1),
                          N_EXPERTS - 1).astype(jnp.int32)
    e_sel = r_int[0:TOP_K].T
    start_sel = jnp.sum(jnp.where(e_sel[..., None] == jnp.arange(N_EXPERTS), pad_start, 0), axis=-1)
    slots = (start_sel + r_int[TOP_K:2 * TOP_K].T).reshape(-1).astype(jnp.int32)

    nonempty = cnt > 0
    expert_seq = (jnp.cumsum(nonempty) - 1).astype(jnp.int32)
    ids = jnp.arange(N_EXPERTS)
    later = (ids[None, :] > ids[:, None]) & nonempty[None, :]
    expert_next = jnp.where(later.any(axis=1), jnp.argmax(later, axis=1), -1).astype(jnp.int32)
    ys = _moe(block_e, n_used, slots, (pad_start + cnt).astype(jnp.int32), pad_end.astype(jnp.int32),
              expert_seq, expert_next, xn,
              wg_b.reshape(w_gate_e.shape), w_up_e.astype(F32), w_down_e.astype(F32), n_blocks * bm)
    return _combine(slots, h, r_w[0:TOP_K].T, ys)


def kernel(x, ln1_w, w_in, q_norm_w, k_norm_w, attn_sink, attn_out_norm_w, gla_gate_up_f, gla_gate_bias_f,
           gla_gate_up_b, gla_gate_bias_b, gla_out_norm_w, w_out, ln2_w, w_group, b_group, w_router,
           b_router, w_gate_e, w_up_e, w_down_e):
    B, S, D = x.shape
    h = x.reshape(B * S, D)
    assert B == 1
    for l in range(ln1_w.shape[0]):
        h = _layer(h, ln1_w[l], w_in[l], q_norm_w[l], k_norm_w[l], attn_sink[l], attn_out_norm_w[l],
                   gla_gate_up_f[l], gla_gate_bias_f[l], gla_gate_up_b[l], gla_gate_bias_b[l],
                   gla_out_norm_w[l], w_out[l], ln2_w[l], w_group[l], b_group[l], w_router[l],
                   b_router[l], w_gate_e[l], w_up_e[l], w_down_e[l])
    return h.reshape(B, S, D)
```

```python
import functools
from typing import NamedTuple

import jax
import jax.numpy as jnp
from jax import lax
from jax.experimental import pallas as pl
from jax.experimental.pallas import tpu as pltpu

F32 = jnp.float32
BF16 = jnp.bfloat16

EPS = 1e-6
D_MODEL = 2048

ATTN_Q_HEADS = 8
ATTN_KV_HEADS = 2
ATTN_GROUP = ATTN_Q_HEADS // ATTN_KV_HEADS
HEAD_DIM = 128
WINDOW = 128
ATTN_BLOCK = 128
ATTN_STEP_BLOCKS = 4
ROPE_THETA = 500000.0
ROPE_DIM = HEAD_DIM // 4
ROPE_HALF = ROPE_DIM // 2
ATTN_WIDTH = ATTN_Q_HEADS * HEAD_DIM
KV_WIDTH = ATTN_KV_HEADS * HEAD_DIM
A_WIDTH = ATTN_WIDTH + 2 * KV_WIDTH

GLA_HEADS = 4
GLA_DK = 128
GLA_DV = 256
GLA_RANK = 16
GLA_TAU = 16.0
GLA_QK_WIDTH = GLA_HEADS * GLA_DK
GLA_WIDTH = GLA_HEADS * GLA_DV
GLA_CHUNK = 128
GLA_STEP_CHUNKS = 4

N_GROUPS = 4
EXPERTS_PER_GROUP = 8
N_EXPERTS = N_GROUPS * EXPERTS_PER_GROUP
TOP_K = 2
D_FF = 1024
ROUTER_ROWS = 128
EXPERT_ROW0 = 8

PROJ_ROWS = 256
OUT_ROWS = 256
MOE_ROWS = 256
MOE_ROW_BUFFERS = 3
MOE_CAST_ROWS = 128
COMBINE_ROWS = 256

VMEM_LIMIT = 56 * 1024 * 1024


def _dot(a, b):
    return jnp.dot(a, b, preferred_element_type=F32)


def _dot_nt(a, b):
    return lax.dot_general(a, b, (((1,), (1,)), ((), ())), preferred_element_type=F32)


def _dot_tn(a, b):
    return lax.dot_general(a, b, (((0,), (0,)), ((), ())), preferred_element_type=F32)


def _pack_bf16_pairs(x):
    n = x.shape[1] // 2
    lo = lax.bitcast_convert_type(x[:, :n].astype(F32), jnp.uint32) >> 16
    hi = lax.bitcast_convert_type(x[:, n:].astype(F32), jnp.uint32) & jnp.uint32(0xFFFF0000)
    return lo | hi


def _unpack_bf16_pairs(w):
    lo = lax.bitcast_convert_type(w << 16, F32)
    hi = lax.bitcast_convert_type(w & jnp.uint32(0xFFFF0000), F32)
    return lo, hi


class CastJob(NamedTuple):
    array: jax.Array
    in_spec: pl.BlockSpec
    out_spec: pl.BlockSpec
    out_shape: jax.ShapeDtypeStruct


def _cast_rows_job(w, n_steps):
    w2 = w.astype(F32).reshape(-1, w.shape[-1])
    rows = w2.shape[0] // n_steps
    assert rows * n_steps == w2.shape[0] and rows % 16 == 0
    spec = pl.BlockSpec((rows, w2.shape[1]), lambda i, *_: (i, 0))
    return CastJob(w2, spec, spec, jax.ShapeDtypeStruct(w2.shape, BF16))


def _resident(shape):
    nd = len(shape)
    return pl.BlockSpec(shape, lambda *_: (0,) * nd, pipeline_mode=pl.Buffered(1))


def _inproj_body(x_ref, ln_ref, wa_ref, wqk_ref, wv_ref, wr_ref, wl_ref, qn_ref, kn_ref,
                 step_cos_ref, step_sin_ref, row_cos_ref, row_sin_ref, wcast_ref,
                 oa_ref, oqk_ref, ov_ref, or_ref, ol_ref, wcast_out_ref):
    wcast_out_ref[...] = wcast_ref[...].astype(BF16)
    x = x_ref[...]
    ms = jnp.mean(x * x, axis=-1, keepdims=True)
    xn = (x * lax.rsqrt(ms + EPS) * ln_ref[...]).astype(BF16)
    acc = _dot(xn, wa_ref[...])
    ca, sn = step_cos_ref[0], step_sin_ref[0]
    cb, sb_ = row_cos_ref[...], row_sin_ref[...]
    cos = cb * ca - sb_ * sn
    sin = sb_ * ca + cb * sn
    lane = lax.broadcasted_iota(jnp.int32, (1, HEAD_DIM), 1)
    sa = jnp.where((lane >= ROPE_HALF) & (lane < ROPE_DIM), sin, 0.0)
    sb = jnp.where(lane < ROPE_HALF, -sin, 0.0)
    for c in range(ATTN_Q_HEADS + ATTN_KV_HEADS):
        xh = acc[:, c * HEAD_DIM:(c + 1) * HEAD_DIM]
        w = qn_ref[...] if c < ATTN_Q_HEADS else kn_ref[...]
        y = xh * lax.rsqrt(jnp.mean(xh * xh, axis=-1, keepdims=True) + EPS) * w
        y = (y * cos + pltpu.roll(y, ROPE_HALF, 1) * sa
             + pltpu.roll(y, HEAD_DIM - ROPE_HALF, 1) * sb)
        if c < ATTN_Q_HEADS:
            y = y * (HEAD_DIM ** -0.5)
        oa_ref[:, c * HEAD_DIM:(c + 1) * HEAD_DIM] = y.astype(BF16)
    oa_ref[:, ATTN_WIDTH + KV_WIDTH:] = acc[:, ATTN_WIDTH + KV_WIDTH:].astype(BF16)
    oqk_ref[...] = _dot(xn, wqk_ref[...]).astype(BF16)
    ov_ref[...] = _dot(xn, wv_ref[...]).astype(BF16)
    or_ref[...] = _dot(xn, wr_ref[...]).astype(BF16)
    ol_ref[...] = _dot(xn, wl_ref[...])


def _inproj(x2, ln_w, wa, wqk, wv, wr, wl, qn, kn, rope, cast):
    T = x2.shape[0]
    tm = PROJ_ROWS
    row = lambda w: pl.BlockSpec((tm, w), lambda i: (i, 0))
    step_row = pl.BlockSpec((1, 1, HEAD_DIM), lambda i: (i, 0, 0))
    return pl.pallas_call(
        _inproj_body,
        grid=(T // tm,),
        in_specs=[row(D_MODEL), _resident((1, D_MODEL)),
                  _resident(wa.shape), _resident(wqk.shape), _resident(wv.shape),
                  _resident(wr.shape), _resident(wl.shape),
                  _resident((1, HEAD_DIM)), _resident((1, HEAD_DIM)),
                  step_row, step_row, _resident((tm, HEAD_DIM)), _resident((tm, HEAD_DIM)), cast.in_spec],
        out_specs=[row(A_WIDTH), row(2 * GLA_QK_WIDTH), row(GLA_WIDTH), row(GLA_WIDTH),
                   row(2 * GLA_RANK), cast.out_spec],
        out_shape=[jax.ShapeDtypeStruct((T, A_WIDTH), BF16),
                   jax.ShapeDtypeStruct((T, 2 * GLA_QK_WIDTH), BF16),
                   jax.ShapeDtypeStruct((T, GLA_WIDTH), BF16),
                   jax.ShapeDtypeStruct((T, GLA_WIDTH), BF16),
                   jax.ShapeDtypeStruct((T, 2 * GLA_RANK), F32), cast.out_shape],
        compiler_params=pltpu.CompilerParams(dimension_semantics=("parallel",),
                                             vmem_limit_bytes=VMEM_LIMIT),
        name="inproj",
    )(x2, ln_w, wa, wqk, wv, wr, wl, qn, kn, *rope, cast.array)


def _attn_body(sink_ref, q_ref, kp_ref, kc_ref, kn_ref, vp_ref, vc_ref, vn_ref, nw_ref, o_ref):
    n = pl.program_id(0)
    nb = pl.num_programs(0)
    rows = ATTN_GROUP * ATTN_BLOCK
    keys = 3 * ATTN_BLOCK
    r = lax.broadcasted_iota(jnp.int32, (rows, keys), 0) & (ATTN_BLOCK - 1)
    c = lax.broadcasted_iota(jnp.int32, (rows, keys), 1)
    band = (c >= r + (ATTN_BLOCK - WINDOW)) & (c <= r + (ATTN_BLOCK + WINDOW))
    edge_mask = [band & ((c >= ATTN_BLOCK) | (n > 0))] + [band] * (ATTN_STEP_BLOCKS - 2) + \
                [band & ((c < 2 * ATTN_BLOCK) | (n < nb - 1))]

    def window(j, prev_ref, cur_ref, next_ref, ks):
        blocks = [prev_ref[:, ks]] + [cur_ref[i * ATTN_BLOCK:(i + 1) * ATTN_BLOCK, ks]
                                      for i in range(ATTN_STEP_BLOCKS)] + [next_ref[:, ks]]
        return jnp.concatenate(blocks[j:j + 3], axis=0)

    chains = [(j, g) for j in range(ATTN_STEP_BLOCKS) for g in range(ATTN_KV_HEADS)]
    heads_of = lambda g: range(g * ATTN_GROUP, (g + 1) * ATTN_GROUP)
    ks_of = lambda g: slice(g * HEAD_DIM, (g + 1) * HEAD_DIM)
    qrows_of = lambda j: slice(j * ATTN_BLOCK, (j + 1) * ATTN_BLOCK)
    s = [jnp.where(edge_mask[j],
                   _dot_nt(jnp.concatenate([q_ref[qrows_of(j), h * HEAD_DIM:(h + 1) * HEAD_DIM]
                                            for h in heads_of(g)], axis=0),
                           window(j, kp_ref, kc_ref, kn_ref, ks_of(g))), -jnp.inf)
         for j, g in chains]
    lane_blocks = keys // HEAD_DIM
    sink = [jnp.concatenate([jnp.full((ATTN_BLOCK, HEAD_DIM), sink_ref[h], F32) for h in heads_of(g)], axis=0)
            for j, g in chains]
    m = [jnp.maximum(jnp.broadcast_to(jnp.max(si, axis=-1, keepdims=True), (rows, HEAD_DIM)), sk)
         for si, sk in zip(s, sink)]
    p = [jnp.concatenate([jnp.exp(si[:, c * HEAD_DIM:(c + 1) * HEAD_DIM] - mi) for c in range(lane_blocks)]
                         + [jnp.exp(sk - mi)], axis=1).astype(BF16)
         for si, sk, mi in zip(s, sink, m)]
    sink_rows = jnp.concatenate([jnp.zeros((HEAD_DIM, HEAD_DIM), BF16),
                                 jnp.full((HEAD_DIM, HEAD_DIM), 1.0 / HEAD_DIM, BF16)], axis=1)

    def values_and_ones(j, g):
        v3 = window(j, vp_ref, vc_ref, vn_ref, ks_of(g))
        return jnp.concatenate([jnp.concatenate([v3, jnp.ones((keys, HEAD_DIM), BF16)], axis=1), sink_rows],
                               axis=0)

    o = []
    for (j, g), pi in zip(chains, p):
        num_den = _dot(pi, values_and_ones(j, g))
        o.append(num_den[:, :HEAD_DIM] / num_den[:, HEAD_DIM:])
    for j in range(ATTN_STEP_BLOCKS):
        outs = [o[chains.index((j, g))][i * ATTN_BLOCK:(i + 1) * ATTN_BLOCK]
                for g in range(ATTN_KV_HEADS) for i in range(ATTN_GROUP)]
        a = jnp.concatenate(outs, axis=1)
        a = a * lax.rsqrt(jnp.mean(a * a, axis=-1, keepdims=True) + EPS) * nw_ref[...]
        o_ref[qrows_of(j), :] = a.astype(BF16)


def _attention(pa, sink, norm_w):
    T = pa.shape[0]
    sb = ATTN_STEP_BLOCKS
    nb = T // ATTN_BLOCK
    kcol = ATTN_WIDTH // KV_WIDTH
    vcol = kcol + 1
    prev = lambda col: pl.BlockSpec((ATTN_BLOCK, KV_WIDTH), lambda n, s: (jnp.maximum(sb * n - 1, 0), col))
    nxt = lambda col: pl.BlockSpec((ATTN_BLOCK, KV_WIDTH),
                                   lambda n, s: (jnp.minimum(sb * n + sb, nb - 1), col))
    cur = lambda col: pl.BlockSpec((sb * ATTN_BLOCK, KV_WIDTH), lambda n, s: (n, col))
    grid_spec = pltpu.PrefetchScalarGridSpec(
        num_scalar_prefetch=1,
        grid=(nb // sb,),
        in_specs=[pl.BlockSpec((sb * ATTN_BLOCK, ATTN_WIDTH), lambda n, s: (n, 0)),
                  prev(kcol), cur(kcol), nxt(kcol), prev(vcol), cur(vcol), nxt(vcol),
                  pl.BlockSpec((1, ATTN_WIDTH), lambda n, s: (0, 0))],
        out_specs=pl.BlockSpec((sb * ATTN_BLOCK, ATTN_WIDTH), lambda n, s: (n, 0)),
    )
    return pl.pallas_call(
        _attn_body,
        grid_spec=grid_spec,
        out_shape=jax.ShapeDtypeStruct((T, ATTN_WIDTH), BF16),
        compiler_params=pltpu.CompilerParams(dimension_semantics=("parallel",),
                                             vmem_limit_bytes=VMEM_LIMIT),
        name="attention",
    )(sink, pa, pa, pa, pa, pa, pa, pa, norm_w)


def _gla_body(reverse, final, *refs):
    if final:
        (q_ref, k_ref, v_ref, lr_ref, up_ref, bias_ref, tri_ref, of_ref, gr_ref, nw_ref,
         o_ref, st_ref) = refs
    else:
        q_ref, k_ref, v_ref, lr_ref, up_ref, bias_ref, tri_ref, o_ref, st_ref = refs
    C = GLA_CHUNK
    R = GLA_STEP_CHUNKS

    @pl.when(pl.program_id(0) == 0)
    def _():
        st_ref[...] = jnp.zeros_like(st_ref)

    chunks = range(R)
    order = tuple(reversed(chunks)) if reverse else tuple(chunks)
    rows = [slice(s * C, (s + 1) * C) for s in chunks]
    heads = range(GLA_HEADS)
    ks = [slice(h * GLA_DK, (h + 1) * GLA_DK) for h in heads]
    vs = [slice(h * GLA_DV, (h + 1) * GLA_DV) for h in heads]

    lr = lr_ref[...]
    lr_hi = lr.astype(BF16)
    lr_lo = (lr - lr_hi.astype(F32)).astype(BF16)
    g = _dot(jnp.concatenate([lr_hi, lr_lo, lr_hi], axis=1), up_ref[...]) + bias_ref[...]
    la = jax.nn.log_sigmoid(g) * (1.0 / GLA_TAU)
    tri = tri_ref[...]
    hi, lo = _split_hi_lo(la)
    b = [_dot(tri, hi[rows[s]]) + _dot(tri, lo[rows[s]]) for s in chunks]
    end = 0 if reverse else C - 1
    b_end = [b[s][end:end + 1] for s in chunks]
    b_mid = [b[s][C // 2:C // 2 + 1] for s in chunks]
    scale = GLA_DK ** -0.5
    q = [q_ref[rows[s], :].astype(F32) * scale for s in chunks]
    k = [k_ref[rows[s], :].astype(F32) for s in chunks]
    q_in = [(q[s] * jnp.exp(b[s] - b_mid[s])).astype(BF16) for s in chunks]
    k_in = [(k[s] * jnp.exp(b_mid[s] - b[s])).astype(BF16) for s in chunks]
    q_dec = [(q[s] * jnp.exp(b[s])).astype(BF16) for s in chunks]
    k_end = [(k[s] * jnp.exp(b_end[s] - b[s])).astype(BF16) for s in chunks]
    decay = [jnp.exp(b_end[s]) for s in chunks]
    ri = lax.broadcasted_iota(jnp.int32, (C, C), 0)
    ci = lax.broadcasted_iota(jnp.int32, (C, C), 1)
    causal = (ri <= ci) if reverse else (ri >= ci)
    att = [[jnp.where(causal, _dot_nt(q_in[s][:, ks[h]], k_in[s][:, ks[h]]), 0.0).astype(BF16)
            for h in heads] for s in chunks]
    intra = [[_dot(att[s][h], v_ref[rows[s], vs[h]]) for h in heads] for s in chunks]
    gain = [[_dot_tn(v_ref[rows[s], vs[h]], k_end[s][:, ks[h]]) for h in heads] for s in chunks]
    st = [st_ref[h] for h in heads]
    o = [[None] * GLA_HEADS for _ in chunks]
    for s in order:
        for h in heads:
            o[s][h] = intra[s][h] + _dot_nt(q_dec[s][:, ks[h]], st[h].astype(BF16))
            st[h] = st[h] * decay[s][:, ks[h]] + gain[s][h]
    for h in heads:
        st_ref[h] = st[h]
    for s in chunks:
        if final:
            tot = [o[s][h] + of_ref[rows[s], vs[h]] for h in heads]
            inv = [lax.rsqrt(jnp.mean(t * t, axis=-1, keepdims=True) + EPS) for t in tot]
            for h in heads:
                y = tot[h] * inv[h] * nw_ref[...]
                o_ref[rows[s], vs[h]] = (y * jax.nn.silu(gr_ref[rows[s], vs[h]].astype(F32))).astype(BF16)
        else:
            for h in heads:
                o_ref[rows[s], vs[h]] = o[s][h]


def _gla_pass(reverse, pqk, pv, plr, up, bias, tri, extra=None):
    T = pqk.shape[0]
    rows = GLA_STEP_CHUNKS * GLA_CHUNK
    n = T // rows
    final = extra is not None
    blk = (lambda i: n - 1 - i) if reverse else (lambda i: i)
    row = lambda w, col=0: pl.BlockSpec((rows, w), lambda i: (blk(i), col))
    in_specs = [row(GLA_QK_WIDTH, 0), row(GLA_QK_WIDTH, 1), row(GLA_WIDTH), row(2 * GLA_RANK),
                _resident(up.shape), _resident(bias.shape), _resident(tri.shape)]
    args = [pqk, pqk, pv, plr, up, bias, tri]
    if final:
        o_f, pr, norm_w = extra
        in_specs += [row(GLA_WIDTH), row(GLA_WIDTH), _resident(norm_w.shape)]
        args += [o_f, pr, norm_w]
    return pl.pallas_call(
        functools.partial(_gla_body, reverse, final),
        grid=(n,),
        in_specs=in_specs,
        out_specs=row(GLA_WIDTH),
        out_shape=jax.ShapeDtypeStruct((T, GLA_WIDTH), BF16 if final else F32),
        scratch_shapes=[pltpu.VMEM((GLA_HEADS, GLA_DV, GLA_DK), F32)],
        compiler_params=pltpu.CompilerParams(dimension_semantics=("arbitrary",),
                                             vmem_limit_bytes=VMEM_LIMIT),
        name="gla_bwd" if reverse else "gla_fwd",
    )(*args)


def _outproj_body(a_ref, g_ref, x_ref, w_ref, ln_ref, wr_hl_ref, wr_hi_ref, rb_ref, tri_ref,
                  h_ref, xn_ref, ri_ref, rw_ref, cnt_ref, logit_ref):
    tm = x_ref.shape[0]
    i = pl.program_id(0)
    last = pl.num_programs(0) - 1

    @pl.when(i == 0)
    def _():
        cnt_ref[...] = jnp.zeros_like(cnt_ref)

    def project():
        h = x_ref[...] + _dot(jnp.concatenate([a_ref[...], g_ref[...]], axis=1), w_ref[...])
        h_ref[...] = h
        xn = h * lax.rsqrt(jnp.mean(h * h, axis=-1, keepdims=True) + EPS) * ln_ref[...]
        x_hi = xn.astype(BF16)
        xn_ref[...] = _pack_bf16_pairs(x_hi)
        x_lo = (xn - x_hi.astype(F32)).astype(BF16)
        l2 = _dot(x_hi, wr_hl_ref[...])
        logit_ref[i % 2] = l2[:, :ROUTER_ROWS] + l2[:, ROUTER_ROWS:] + _dot(x_lo, wr_hi_ref[...])

    def route():
        logits = logit_ref[(i + 1) % 2]
        lt = jnp.transpose(logits) + rb_ref[...]

        gl = lt[0:N_GROUPS]
        gmax = jnp.max(gl, axis=0, keepdims=True)
        gi = lax.broadcasted_iota(jnp.int32, gl.shape, 0).astype(F32)
        g_sel = jnp.min(jnp.where(gl == gmax, gi, float(N_GROUPS)), axis=0, keepdims=True)
        g_gate = 1.0 / jnp.sum(jnp.exp(gl - gmax), axis=0, keepdims=True)

        el = lt[EXPERT_ROW0:EXPERT_ROW0 + N_EXPERTS]
        ei_int = lax.broadcasted_iota(jnp.int32, el.shape, 0)
        ei = ei_int.astype(F32)
        grp = (ei_int >> 3).astype(F32)
        cand = jnp.where(grp == g_sel, el, -jnp.inf)
        v1 = jnp.max(cand, axis=0, keepdims=True)
        e1 = jnp.min(jnp.where(cand == v1, ei, float(N_EXPERTS)), axis=0, keepdims=True)
        cand2 = jnp.where(ei == e1, -jnp.inf, cand)
        v2 = jnp.max(cand2, axis=0, keepdims=True)
        e2 = jnp.min(jnp.where(cand2 == v2, ei, float(N_EXPERTS)), axis=0, keepdims=True)
        d = jnp.exp(v2 - v1)
        w1 = g_gate / (1.0 + d)
        w2 = g_gate * d / (1.0 + d)

        oh1 = (ei == e1).astype(F32)
        oh2 = (ei == e2).astype(F32)
        cnt = oh1 + oh2
        before = _dot(cnt.astype(BF16), tri_ref[...]) + cnt_ref[:, 0:1]
        r1 = jnp.sum(oh1 * before, axis=0, keepdims=True)
        r2 = jnp.sum(oh2 * before, axis=0, keepdims=True)
        cnt_ref[...] = cnt_ref[...] + jnp.sum(cnt, axis=1, keepdims=True)

        ri_ref[...] = jnp.concatenate([e1, e2, r1, r2, jnp.zeros((4, tm), F32)], axis=0).astype(jnp.int32)
        rw_ref[...] = jnp.concatenate([w1, w2, jnp.zeros((6, tm), F32)], axis=0)

    @pl.when(i == 0)
    def _():
        project()

    @pl.when((i > 0) & (i < last))
    def _():
        route()
        project()

    @pl.when(i == last)
    def _():
        route()


def _outproj_router(attn, gla, x2, w_out, ln_w, wr_hl, wr_hi, rbias, tri):
    T = x2.shape[0]
    tm = OUT_ROWS
    n = T // tm
    row = lambda w: pl.BlockSpec((tm, w), lambda i: (jnp.minimum(i, n - 1), 0))
    col = lambda r: pl.BlockSpec((r, tm), lambda i: (0, jnp.maximum(i - 1, 0)))
    return pl.pallas_call(
        _outproj_body,
        grid=(n + 1,),
        in_specs=[row(ATTN_WIDTH), row(GLA_WIDTH), row(D_MODEL),
                  _resident(w_out.shape), _resident(ln_w.shape),
                  _resident(wr_hl.shape), _resident(wr_hi.shape), _resident(rbias.shape),
                  _resident(tri.shape)],
        out_specs=[row(D_MODEL), row(D_MODEL // 2), col(8), col(8),
                   pl.BlockSpec((N_EXPERTS, 128), lambda i: (0, 0))],
        out_shape=[jax.ShapeDtypeStruct((T, D_MODEL), F32),
                   jax.ShapeDtypeStruct((T, D_MODEL // 2), jnp.uint32),
                   jax.ShapeDtypeStruct((8, T), jnp.int32),
                   jax.ShapeDtypeStruct((8, T), F32),
                   jax.ShapeDtypeStruct((N_EXPERTS, 128), F32)],
        scratch_shapes=[pltpu.VMEM((2, tm, ROUTER_ROWS), F32)],
        compiler_params=pltpu.CompilerParams(dimension_semantics=("arbitrary",),
                                             vmem_limit_bytes=VMEM_LIMIT),
        name="outproj_router",
    )(attn, gla, x2, w_out, ln_w, wr_hl, wr_hi, rbias, tri)


def _row_copy_start(src, s, dst, d, sem, queue):
    pltpu.async_copy(src.at[pl.ds(s, 1)], dst.at[pl.ds(d, 1)], sem, priority=queue)


def _moe_body(be_ref, nu_ref, slot_ref, lo_ref, hi_ref, seq_ref, nxt_ref, xn_ref, wg_ref, wu_ref, wd_ref,
              y_ref, xbuf, xb_ref, tok_ref, wg_buf, wu_buf, wd_buf, wu_stage, wd_stage, sem, wsem, dsem):
    bm = MOE_ROWS
    b = pl.program_id(0)
    n_used = nu_ref[0]
    cur = b % MOE_ROW_BUFFERS
    expert = be_ref[b]
    wslot = seq_ref[expert] % 2
    first_of_expert = (b == 0) | (be_ref[jnp.maximum(b - 1, 0)] != expert)

    def weight_copies(e, slot):
        return [pltpu.make_async_copy(wg_ref.at[e], wg_buf.at[slot], wsem.at[slot]),
                pltpu.make_async_copy(wu_ref.at[e], wu_stage, dsem.at[0]),
                pltpu.make_async_copy(wd_ref.at[e], wd_stage, dsem.at[1])]

    def gather_start(block, buf, unrolled):
        def one(i, queue):
            _row_copy_start(xn_ref, tok_ref[block * bm + i], xbuf.at[buf], i, sem.at[buf], queue)
        if unrolled:
            for i in range(bm):
                one(i, i % 2)
        else:
            def pair(p, c):
                one(2 * p, 0)
                one(2 * p + 1, 1)
                return c
            lax.fori_loop(0, bm // 2, pair, 0)

    def gather_wait(buf):
        pltpu.make_async_copy(xn_ref.at[pl.ds(0, bm)], xbuf.at[buf], sem.at[buf]).wait()

    @pl.when(b == 0)
    def _():
        for c in weight_copies(expert, wslot):
            c.start()
        def per_expert(e, c):
            def zero(s, c2):
                tok_ref[s] = 0
                return c2
            return lax.fori_loop(lo_ref[e], hi_ref[e], zero, c)
        lax.fori_loop(0, N_EXPERTS, per_expert, 0)

        def fill(t, c):
            for k in range(TOP_K):
                tok_ref[slot_ref[TOP_K * t + k]] = t
            return c
        lax.fori_loop(0, slot_ref.shape[0] // TOP_K, fill, 0, unroll=8)
        gather_start(0, 0, False)

        @pl.when(n_used > 1)
        def _():
            gather_start(1, 1, False)

    @pl.when((b < n_used) & first_of_expert)
    def _():
        for c in weight_copies(expert, wslot):
            c.wait()
        for r in range(0, D_MODEL, MOE_CAST_ROWS):
            wu_buf[wslot, r:r + MOE_CAST_ROWS, :] = wu_stage[r:r + MOE_CAST_ROWS, :].astype(BF16)
        for r in range(0, D_FF, MOE_CAST_ROWS):
            wd_buf[wslot, r:r + MOE_CAST_ROWS, :] = wd_stage[r:r + MOE_CAST_ROWS, :].astype(BF16)
        following = nxt_ref[expert]

        @pl.when(following >= 0)
        def _():
            for c in weight_copies(following, 1 - wslot):
                c.start()

    def stage_rows():
        gather_wait(cur)
        lo, hi = _unpack_bf16_pairs(xbuf[cur])
        xb_ref[:, :D_MODEL // 2] = lo.astype(BF16)
        xb_ref[:, D_MODEL // 2:] = hi.astype(BF16)

    def experts():
        x = xb_ref[...]
        hid = jax.nn.silu(_dot(x, wg_buf[wslot])) * _dot(x, wu_buf[wslot])
        y = _dot(hid.astype(BF16), wd_buf[wslot])
        y_ref[...] = _pack_bf16_pairs(y.astype(BF16))

    @pl.when(b + 2 < n_used)
    def _():
        stage_rows()
        gather_start(b + 2, (b + 2) % MOE_ROW_BUFFERS, True)
        experts()

    @pl.when((b < n_used) & (b + 2 >= n_used))
    def _():
        stage_rows()
        experts()

    @pl.when(b >= n_used)
    def _():
        y_ref[...] = jnp.zeros_like(y_ref)


def _moe(block_e, n_used, slots, pad_lo, pad_hi, expert_seq, expert_next, xn, wg, wu, wd, n_slots):
    bm = MOE_ROWS
    any_spec = pl.BlockSpec(memory_space=pl.ANY)
    grid_spec = pltpu.PrefetchScalarGridSpec(
        num_scalar_prefetch=7,
        grid=(n_slots // bm,),
        in_specs=[any_spec] * 4,
        out_specs=pl.BlockSpec((bm, D_MODEL // 2), lambda b, *_: (b, 0)),
        scratch_shapes=[pltpu.VMEM((MOE_ROW_BUFFERS, bm, D_MODEL // 2), jnp.uint32),
                        pltpu.VMEM((bm, D_MODEL), BF16),
                        pltpu.SMEM((n_slots,), jnp.int32),
                        pltpu.VMEM((2, D_MODEL, D_FF), BF16), pltpu.VMEM((2, D_MODEL, D_FF), BF16),
                        pltpu.VMEM((2, D_FF, D_MODEL), BF16),
                        pltpu.VMEM((D_MODEL, D_FF), F32), pltpu.VMEM((D_FF, D_MODEL), F32),
                        pltpu.SemaphoreType.DMA((MOE_ROW_BUFFERS,)), pltpu.SemaphoreType.DMA((2,)),
                        pltpu.SemaphoreType.DMA((2,))],
    )
    return pl.pallas_call(
        _moe_body,
        grid_spec=grid_spec,
        out_shape=jax.ShapeDtypeStruct((n_slots, D_MODEL // 2), jnp.uint32),
        compiler_params=pltpu.CompilerParams(dimension_semantics=("arbitrary",),
                                             vmem_limit_bytes=VMEM_LIMIT),
        name="moe",
    )(block_e, n_used, slots, pad_lo, pad_hi, expert_seq, expert_next, xn, wg, wu, wd)


def _combine_body(slot_ref, h_ref, w_ref, ys_ref, o_ref, g_ref, sem):
    tb = COMBINE_ROWS
    i = pl.program_id(0)
    cur = i % 2

    def gather_start(block, buf):
        def one(t, c):
            for k in range(TOP_K):
                _row_copy_start(ys_ref, slot_ref[TOP_K * (block * tb + t) + k], g_ref.at[buf, k], t,
                                sem.at[buf], k)
            return c
        lax.fori_loop(0, tb, one, 0, unroll=8)

    @pl.when(i == 0)
    def _():
        gather_start(0, 0)

    @pl.when(i + 1 < pl.num_programs(0))
    def _():
        gather_start(i + 1, 1 - cur)

    for k in range(TOP_K):
        pltpu.make_async_copy(ys_ref.at[pl.ds(0, tb)], g_ref.at[cur, k], sem.at[cur]).wait()
    lo0, hi0 = _unpack_bf16_pairs(g_ref[cur, 0])
    lo1, hi1 = _unpack_bf16_pairs(g_ref[cur, 1])
    w0, w1 = w_ref[:, 0:1], w_ref[:, 1:2]
    half = D_MODEL // 2
    o_ref[:, :half] = h_ref[:, :half] + w0 * lo0 + w1 * lo1
    o_ref[:, half:] = h_ref[:, half:] + w0 * hi0 + w1 * hi1


def _combine(slots, h, wcol, ys):
    T = h.shape[0]
    tb = COMBINE_ROWS
    grid_spec = pltpu.PrefetchScalarGridSpec(
        num_scalar_prefetch=1,
        grid=(T // tb,),
        in_specs=[pl.BlockSpec((tb, D_MODEL), lambda i, s: (i, 0)),
                  pl.BlockSpec((tb, TOP_K), lambda i, s: (i, 0)),
                  pl.BlockSpec(memory_space=pl.ANY)],
        out_specs=pl.BlockSpec((tb, D_MODEL), lambda i, s: (i, 0)),
        scratch_shapes=[pltpu.VMEM((2, TOP_K, tb, D_MODEL // 2), jnp.uint32),
                        pltpu.SemaphoreType.DMA((2,))],
    )
    return pl.pallas_call(
        _combine_body,
        grid_spec=grid_spec,
        out_shape=jax.ShapeDtypeStruct((T, D_MODEL), F32),
        compiler_params=pltpu.CompilerParams(dimension_semantics=("arbitrary",),
                                             vmem_limit_bytes=VMEM_LIMIT),
        name="combine",
    )(slots, h, wcol, ys)


def _rope_tables(T, rows):
    inv_freq = jnp.power(jnp.float32(ROPE_THETA),
                         -jnp.arange(ROPE_HALF, dtype=F32) * (2.0 / ROPE_DIM))

    def tables(pos):
        ang = pos.astype(F32)[:, None] * inv_freq[None, :]
        rest = (pos.shape[0], HEAD_DIM - ROPE_DIM)
        cos = jnp.concatenate([jnp.cos(ang), jnp.cos(ang), jnp.ones(rest, F32)], axis=1)
        sin = jnp.concatenate([jnp.sin(ang), jnp.sin(ang), jnp.zeros(rest, F32)], axis=1)
        return cos, sin

    step_cos, step_sin = tables(jnp.arange(T // rows) * rows)
    row_cos, row_sin = tables(jnp.arange(rows))
    return step_cos[:, None, :], step_sin[:, None, :], row_cos, row_sin


def _split_hi_lo(w):
    hi = w.astype(BF16)
    lo = (w - hi.astype(F32)).astype(BF16)
    return hi, lo


def _layer(x2, ln1_w, w_in, q_norm_w, k_norm_w, attn_sink, attn_out_norm_w, gate_up_f, gate_bias_f,
           gate_up_b, gate_bias_b, gla_out_norm_w, w_out, ln2_w, w_group, b_group, w_router, b_router,
           w_gate_e, w_up_e, w_down_e):
    T = x2.shape[0]
    row = lambda v: v.reshape(1, -1).astype(F32)

    c0, c1, c2, c3 = A_WIDTH, A_WIDTH + 2 * GLA_QK_WIDTH, A_WIDTH + 2 * GLA_QK_WIDTH + GLA_WIDTH, \
        A_WIDTH + 2 * GLA_QK_WIDTH + 2 * GLA_WIDTH
    wa, wqk, wv, wr, wl = (w_in[:, a:b].astype(BF16) for a, b in
                           ((0, c0), (c0, c1), (c1, c2), (c2, c3), (c3, w_in.shape[1])))
    pa, pqk, pv, pr, plr, wg_b = _inproj(x2, row(ln1_w), wa, wqk, wv, wr, wl, row(q_norm_w), row(k_norm_w),
                                         _rope_tables(T, PROJ_ROWS), _cast_rows_job(w_gate_e, T // PROJ_ROWS))

    attn = _attention(pa, attn_sink.astype(F32), row(attn_out_norm_w))

    C = GLA_CHUNK
    ones = jnp.ones((C, C), F32)
    zr = jnp.zeros((GLA_RANK, GLA_QK_WIDTH), F32)
    up_f = jnp.concatenate([gate_up_f.astype(F32), zr], axis=0)
    up_b = jnp.concatenate([zr, gate_up_b.astype(F32)], axis=0)

    def up_pieces(up):
        hi, lo = _split_hi_lo(up)
        return jnp.concatenate([hi, hi, lo], axis=0)

    o_f = _gla_pass(False, pqk, pv, plr, up_pieces(up_f), row(gate_bias_f), jnp.tril(ones).astype(BF16))
    gla = _gla_pass(True, pqk, pv, plr, up_pieces(up_b), row(gate_bias_b), jnp.triu(ones).astype(BF16),
                    extra=(o_f, pr, row(gla_out_norm_w)))

    wr_full = jnp.zeros((D_MODEL, ROUTER_ROWS), F32)
    wr_full = wr_full.at[:, :N_GROUPS].set(w_group.astype(F32))
    wr_full = wr_full.at[:, EXPERT_ROW0:EXPERT_ROW0 + N_EXPERTS].set(w_router.astype(F32))
    wr_hi, wr_lo = _split_hi_lo(wr_full)
    rbias = jnp.zeros((ROUTER_ROWS, 1), F32)
    rbias = rbias.at[:N_GROUPS, 0].set(b_group.astype(F32))
    rbias = rbias.at[EXPERT_ROW0:EXPERT_ROW0 + N_EXPERTS, 0].set(b_router.astype(F32))
    tm = OUT_ROWS
    earlier = jnp.triu(jnp.ones((tm, tm), F32), k=1).astype(BF16)
    h, xn, r_int, r_w, counts = _outproj_router(
        attn, gla, x2, w_out.astype(BF16), row(ln2_w),
        jnp.concatenate([wr_hi, wr_lo], axis=1), wr_hi, rbias, earlier)

    bm = MOE_ROWS
    n_blocks = (T * TOP_K) // bm + N_EXPERTS
    cnt = counts[:, 0].astype(jnp.int32)
    padded = (cnt + bm - 1) // bm * bm
    pad_end = jnp.cumsum(padded)
    pad_start = pad_end - padded
    n_used = (pad_end[-1] // bm).astype(jnp.int32).reshape(1)
    block_e = jnp.minimum(jnp.sum(pad_end[None, :] <= (jnp.arange(n_blocks) * bm)[:, None], axis=1),
                          N_EXPERTS - 1).astype(jnp.int32)
    e_sel = r_int[0:TOP_K].T
    start_sel = jnp.sum(jnp.where(e_sel[..., None] == jnp.arange(N_EXPERTS), pad_start, 0), axis=-1)
    slots = (start_sel + r_int[TOP_K:2 * TOP_K].T).reshape(-1).astype(jnp.int32)

    nonempty = cnt > 0
    expert_seq = (jnp.cumsum(nonempty) - 1).astype(jnp.int32)
    ids = jnp.arange(N_EXPERTS)
    later = (ids[None, :] > ids[:, None]) & nonempty[None, :]
    expert_next = jnp.where(later.any(axis=1), jnp.argmax(later, axis=1), -1).astype(jnp.int32)
    ys = _moe(block_e, n_used, slots, (pad_start + cnt).astype(jnp.int32), pad_end.astype(jnp.int32),
              expert_seq, expert_next, xn,
              wg_b.reshape(w_gate_e.shape), w_up_e.astype(F32), w_down_e.astype(F32), n_blocks * bm)
    return _combine(slots, h, r_w[0:TOP_K].T, ys)


def kernel(x, ln1_w, w_in, q_norm_w, k_norm_w, attn_sink, attn_out_norm_w, gla_gate_up_f, gla_gate_bias_f,
           gla_gate_up_b, gla_gate_bias_b, gla_out_norm_w, w_out, ln2_w, w_group, b_group, w_router,
           b_router, w_gate_e, w_up_e, w_down_e):
    B, S, D = x.shape
    h = x.reshape(B * S, D)
    assert B == 1
    for l in range(ln1_w.shape[0]):
        h = _layer(h, ln1_w[l], w_in[l], q_norm_w[l], k_norm_w[l], attn_sink[l], attn_out_norm_w[l],
                   gla_gate_up_f[l], gla_gate_bias_f[l], gla_gate_up_b[l], gla_gate_bias_b[l],
                   gla_out_norm_w[l], w_out[l], ln2_w[l], w_group[l], b_group[l], w_router[l],
                   b_router[l], w_gate_e[l], w_up_e[l], w_down_e[l])
    return h.reshape(B, S, D)
```

```python
import functools
from typing import NamedTuple

import jax
import jax.numpy as jnp
from jax import lax
from jax.experimental import pallas as pl
from jax.experimental.pallas import tpu as pltpu

F32 = jnp.float32
BF16 = jnp.bfloat16

EPS = 1e-6
D_MODEL = 2048

ATTN_Q_HEADS = 8
ATTN_KV_HEADS = 2
ATTN_GROUP = ATTN_Q_HEADS // ATTN_KV_HEADS
HEAD_DIM = 128
WINDOW = 128
ATTN_BLOCK = 128
ATTN_STEP_BLOCKS = 8
ROPE_THETA = 500000.0
ROPE_DIM = HEAD_DIM // 4
ROPE_HALF = ROPE_DIM // 2
ATTN_WIDTH = ATTN_Q_HEADS * HEAD_DIM
KV_WIDTH = ATTN_KV_HEADS * HEAD_DIM
A_WIDTH = ATTN_WIDTH + 2 * KV_WIDTH

GLA_HEADS = 4
GLA_DK = 128
GLA_DV = 256
GLA_RANK = 16
GLA_TAU = 16.0
GLA_QK_WIDTH = GLA_HEADS * GLA_DK
GLA_WIDTH = GLA_HEADS * GLA_DV
GLA_CHUNK = 128
GLA_STEP_CHUNKS = 8

N_GROUPS = 4
EXPERTS_PER_GROUP = 8
N_EXPERTS = N_GROUPS * EXPERTS_PER_GROUP
TOP_K = 2
D_FF = 1024
ROUTER_ROWS = 128
EXPERT_ROW0 = 8

PROJ_ROWS = 256
OUT_ROWS = 256
MOE_ROWS = 256
MOE_ROW_BUFFERS = 3
MOE_CAST_ROWS = 128
COMBINE_ROWS = 256

VMEM_LIMIT = 56 * 1024 * 1024


def _dot(a, b):
    return jnp.dot(a, b, preferred_element_type=F32)


def _dot_nt(a, b):
    return lax.dot_general(a, b, (((1,), (1,)), ((), ())), preferred_element_type=F32)


def _dot_tn(a, b):
    return lax.dot_general(a, b, (((0,), (0,)), ((), ())), preferred_element_type=F32)


def _pack_bf16_pairs(x):
    n = x.shape[1] // 2
    lo = lax.bitcast_convert_type(x[:, :n].astype(F32), jnp.uint32) >> 16
    hi = lax.bitcast_convert_type(x[:, n:].astype(F32), jnp.uint32) & jnp.uint32(0xFFFF0000)
    return lo | hi


def _unpack_bf16_pairs(w):
    lo = lax.bitcast_convert_type(w << 16, F32)
    hi = lax.bitcast_convert_type(w & jnp.uint32(0xFFFF0000), F32)
    return lo, hi


class CastJob(NamedTuple):
    array: jax.Array
    in_spec: pl.BlockSpec
    out_spec: pl.BlockSpec
    out_shape: jax.ShapeDtypeStruct


def _cast_rows_job(w, n_steps):
    w2 = w.astype(F32).reshape(-1, w.shape[-1])
    rows = w2.shape[0] // n_steps
    assert rows * n_steps == w2.shape[0] and rows % 16 == 0
    spec = pl.BlockSpec((rows, w2.shape[1]), lambda i, *_: (i, 0))
    return CastJob(w2, spec, spec, jax.ShapeDtypeStruct(w2.shape, BF16))


def _resident(shape):
    nd = len(shape)
    return pl.BlockSpec(shape, lambda *_: (0,) * nd, pipeline_mode=pl.Buffered(1))


def _inproj_body(x_ref, ln_ref, wa_ref, wqk_ref, wv_ref, wr_ref, wl_ref, qn_ref, kn_ref,
                 step_cos_ref, step_sin_ref, row_cos_ref, row_sin_ref, wcast_ref,
                 oa_ref, oqk_ref, ov_ref, or_ref, ol_ref, wcast_out_ref):
    wcast_out_ref[...] = wcast_ref[...].astype(BF16)
    x = x_ref[...]
    ms = jnp.mean(x * x, axis=-1, keepdims=True)
    xn = (x * lax.rsqrt(ms + EPS) * ln_ref[...]).astype(BF16)
    acc = _dot(xn, wa_ref[...])
    ca, sn = step_cos_ref[0], step_sin_ref[0]
    cb, sb_ = row_cos_ref[...], row_sin_ref[...]
    cos = cb * ca - sb_ * sn
    sin = sb_ * ca + cb * sn
    lane = lax.broadcasted_iota(jnp.int32, (1, HEAD_DIM), 1)
    sa = jnp.where((lane >= ROPE_HALF) & (lane < ROPE_DIM), sin, 0.0)
    sb = jnp.where(lane < ROPE_HALF, -sin, 0.0)
    for c in range(ATTN_Q_HEADS + ATTN_KV_HEADS):
        xh = acc[:, c * HEAD_DIM:(c + 1) * HEAD_DIM]
        w = qn_ref[...] if c < ATTN_Q_HEADS else kn_ref[...]
        y = xh * lax.rsqrt(jnp.mean(xh * xh, axis=-1, keepdims=True) + EPS) * w
        y = (y * cos + pltpu.roll(y, ROPE_HALF, 1) * sa
             + pltpu.roll(y, HEAD_DIM - ROPE_HALF, 1) * sb)
        if c < ATTN_Q_HEADS:
            y = y * (HEAD_DIM ** -0.5)
        oa_ref[:, c * HEAD_DIM:(c + 1) * HEAD_DIM] = y.astype(BF16)
    oa_ref[:, ATTN_WIDTH + KV_WIDTH:] = acc[:, ATTN_WIDTH + KV_WIDTH:].astype(BF16)
    oqk_ref[...] = _dot(xn, wqk_ref[...]).astype(BF16)
    ov_ref[...] = _dot(xn, wv_ref[...]).astype(BF16)
    or_ref[...] = _dot(xn, wr_ref[...]).astype(BF16)
    ol_ref[...] = _dot(xn, wl_ref[...])


def _inproj(x2, ln_w, wa, wqk, wv, wr, wl, qn, kn, rope, cast):
    T = x2.shape[0]
    tm = PROJ_ROWS
    row = lambda w: pl.BlockSpec((tm, w), lambda i: (i, 0))
    step_row = pl.BlockSpec((1, 1, HEAD_DIM), lambda i: (i, 0, 0))
    return pl.pallas_call(
        _inproj_body,
        grid=(T // tm,),
        in_specs=[row(D_MODEL), _resident((1, D_MODEL)),
                  _resident(wa.shape), _resident(wqk.shape), _resident(wv.shape),
                  _resident(wr.shape), _resident(wl.shape),
                  _resident((1, HEAD_DIM)), _resident((1, HEAD_DIM)),
                  step_row, step_row, _resident((tm, HEAD_DIM)), _resident((tm, HEAD_DIM)), cast.in_spec],
        out_specs=[row(A_WIDTH), row(2 * GLA_QK_WIDTH), row(GLA_WIDTH), row(GLA_WIDTH),
                   row(2 * GLA_RANK), cast.out_spec],
        out_shape=[jax.ShapeDtypeStruct((T, A_WIDTH), BF16),
                   jax.ShapeDtypeStruct((T, 2 * GLA_QK_WIDTH), BF16),
                   jax.ShapeDtypeStruct((T, GLA_WIDTH), BF16),
                   jax.ShapeDtypeStruct((T, GLA_WIDTH), BF16),
                   jax.ShapeDtypeStruct((T, 2 * GLA_RANK), F32), cast.out_shape],
        compiler_params=pltpu.CompilerParams(dimension_semantics=("parallel",),
                                             vmem_limit_bytes=VMEM_LIMIT),
        name="inproj",
    )(x2, ln_w, wa, wqk, wv, wr, wl, qn, kn, *rope, cast.array)


def _attn_body(sink_ref, q_ref, kp_ref, kc_ref, kn_ref, vp_ref, vc_ref, vn_ref, nw_ref, o_ref):
    n = pl.program_id(0)
    nb = pl.num_programs(0)
    rows = ATTN_GROUP * ATTN_BLOCK
    keys = 3 * ATTN_BLOCK
    r = lax.broadcasted_iota(jnp.int32, (rows, keys), 0) & (ATTN_BLOCK - 1)
    c = lax.broadcasted_iota(jnp.int32, (rows, keys), 1)
    band = (c >= r + (ATTN_BLOCK - WINDOW)) & (c <= r + (ATTN_BLOCK + WINDOW))
    edge_mask = [band & ((c >= ATTN_BLOCK) | (n > 0))] + [band] * (ATTN_STEP_BLOCKS - 2) + \
                [band & ((c < 2 * ATTN_BLOCK) | (n < nb - 1))]

    def window(j, prev_ref, cur_ref, next_ref, ks):
        blocks = [prev_ref[:, ks]] + [cur_ref[i * ATTN_BLOCK:(i + 1) * ATTN_BLOCK, ks]
                                      for i in range(ATTN_STEP_BLOCKS)] + [next_ref[:, ks]]
        return jnp.concatenate(blocks[j:j + 3], axis=0)

    chains = [(j, g) for j in range(ATTN_STEP_BLOCKS) for g in range(ATTN_KV_HEADS)]
    heads_of = lambda g: range(g * ATTN_GROUP, (g + 1) * ATTN_GROUP)
    ks_of = lambda g: slice(g * HEAD_DIM, (g + 1) * HEAD_DIM)
    qrows_of = lambda j: slice(j * ATTN_BLOCK, (j + 1) * ATTN_BLOCK)
    s = [jnp.where(edge_mask[j],
                   _dot_nt(jnp.concatenate([q_ref[qrows_of(j), h * HEAD_DIM:(h + 1) * HEAD_DIM]
                                            for h in heads_of(g)], axis=0),
                           window(j, kp_ref, kc_ref, kn_ref, ks_of(g))), -jnp.inf)
         for j, g in chains]
    lane_blocks = keys // HEAD_DIM
    sink = [jnp.concatenate([jnp.full((ATTN_BLOCK, HEAD_DIM), sink_ref[h], F32) for h in heads_of(g)], axis=0)
            for j, g in chains]
    m = [jnp.maximum(jnp.broadcast_to(jnp.max(si, axis=-1, keepdims=True), (rows, HEAD_DIM)), sk)
         for si, sk in zip(s, sink)]
    p = [jnp.concatenate([jnp.exp(si[:, c * HEAD_DIM:(c + 1) * HEAD_DIM] - mi) for c in range(lane_blocks)]
                         + [jnp.exp(sk - mi)], axis=1).astype(BF16)
         for si, sk, mi in zip(s, sink, m)]
    sink_rows = jnp.concatenate([jnp.zeros((HEAD_DIM, HEAD_DIM), BF16),
                                 jnp.full((HEAD_DIM, HEAD_DIM), 1.0 / HEAD_DIM, BF16)], axis=1)

    def values_and_ones(j, g):
        v3 = window(j, vp_ref, vc_ref, vn_ref, ks_of(g))
        return jnp.concatenate([jnp.concatenate([v3, jnp.ones((keys, HEAD_DIM), BF16)], axis=1), sink_rows],
                               axis=0)

    o = []
    for (j, g), pi in zip(chains, p):
        num_den = _dot(pi, values_and_ones(j, g))
        o.append(num_den[:, :HEAD_DIM] / num_den[:, HEAD_DIM:])
    for j in range(ATTN_STEP_BLOCKS):
        outs = [o[chains.index((j, g))][i * ATTN_BLOCK:(i + 1) * ATTN_BLOCK]
                for g in range(ATTN_KV_HEADS) for i in range(ATTN_GROUP)]
        a = jnp.concatenate(outs, axis=1)
        a = a * lax.rsqrt(jnp.mean(a * a, axis=-1, keepdims=True) + EPS) * nw_ref[...]
        o_ref[qrows_of(j), :] = a.astype(BF16)


def _attention(pa, sink, norm_w):
    T = pa.shape[0]
    sb = ATTN_STEP_BLOCKS
    nb = T // ATTN_BLOCK
    kcol = ATTN_WIDTH // KV_WIDTH
    vcol = kcol + 1
    prev = lambda col: pl.BlockSpec((ATTN_BLOCK, KV_WIDTH), lambda n, s: (jnp.maximum(sb * n - 1, 0), col))
    nxt = lambda col: pl.BlockSpec((ATTN_BLOCK, KV_WIDTH),
                                   lambda n, s: (jnp.minimum(sb * n + sb, nb - 1), col))
    cur = lambda col: pl.BlockSpec((sb * ATTN_BLOCK, KV_WIDTH), lambda n, s: (n, col))
    grid_spec = pltpu.PrefetchScalarGridSpec(
        num_scalar_prefetch=1,
        grid=(nb // sb,),
        in_specs=[pl.BlockSpec((sb * ATTN_BLOCK, ATTN_WIDTH), lambda n, s: (n, 0)),
                  prev(kcol), cur(kcol), nxt(kcol), prev(vcol), cur(vcol), nxt(vcol),
                  pl.BlockSpec((1, ATTN_WIDTH), lambda n, s: (0, 0))],
        out_specs=pl.BlockSpec((sb * ATTN_BLOCK, ATTN_WIDTH), lambda n, s: (n, 0)),
    )
    return pl.pallas_call(
        _attn_body,
        grid_spec=grid_spec,
        out_shape=jax.ShapeDtypeStruct((T, ATTN_WIDTH), BF16),
        compiler_params=pltpu.CompilerParams(dimension_semantics=("parallel",),
                                             vmem_limit_bytes=VMEM_LIMIT),
        name="attention",
    )(sink, pa, pa, pa, pa, pa, pa, pa, norm_w)


def _gla_body(reverse, final, *refs):
    if final:
        (q_ref, k_ref, v_ref, lr_ref, up_ref, bias_ref, tri_ref, of_ref, gr_ref, nw_ref,
         o_ref, st_ref) = refs
    else:
        q_ref, k_ref, v_ref, lr_ref, up_ref, bias_ref, tri_ref, o_ref, st_ref = refs
    C = GLA_CHUNK
    R = GLA_STEP_CHUNKS

    @pl.when(pl.program_id(0) == 0)
    def _():
        st_ref[...] = jnp.zeros_like(st_ref)

    chunks = range(R)
    order = tuple(reversed(chunks)) if reverse else tuple(chunks)
    rows = [slice(s * C, (s + 1) * C) for s in chunks]
    heads = range(GLA_HEADS)
    ks = [slice(h * GLA_DK, (h + 1) * GLA_DK) for h in heads]
    vs = [slice(h * GLA_DV, (h + 1) * GLA_DV) for h in heads]

    lr = lr_ref[...]
    lr_hi = lr.astype(BF16)
    lr_lo = (lr - lr_hi.astype(F32)).astype(BF16)
    g = _dot(jnp.concatenate([lr_hi, lr_lo, lr_hi], axis=1), up_ref[...]) + bias_ref[...]
    la = jax.nn.log_sigmoid(g) * (1.0 / GLA_TAU)
    tri = tri_ref[...]
    hi, lo = _split_hi_lo(la)
    b = [_dot(tri, hi[rows[s]]) + _dot(tri, lo[rows[s]]) for s in chunks]
    end = 0 if reverse else C - 1
    b_end = [b[s][end:end + 1] for s in chunks]
    b_mid = [b[s][C // 2:C // 2 + 1] for s in chunks]
    scale = GLA_DK ** -0.5
    q = [q_ref[rows[s], :].astype(F32) * scale for s in chunks]
    k = [k_ref[rows[s], :].astype(F32) for s in chunks]
    q_in = [(q[s] * jnp.exp(b[s] - b_mid[s])).astype(BF16) for s in chunks]
    k_in = [(k[s] * jnp.exp(b_mid[s] - b[s])).astype(BF16) for s in chunks]
    q_dec = [(q[s] * jnp.exp(b[s])).astype(BF16) for s in chunks]
    k_end = [(k[s] * jnp.exp(b_end[s] - b[s])).astype(BF16) for s in chunks]
    decay = [jnp.exp(b_end[s]) for s in chunks]
    ri = lax.broadcasted_iota(jnp.int32, (C, C), 0)
    ci = lax.broadcasted_iota(jnp.int32, (C, C), 1)
    causal = (ri <= ci) if reverse else (ri >= ci)
    att = [[jnp.where(causal, _dot_nt(q_in[s][:, ks[h]], k_in[s][:, ks[h]]), 0.0).astype(BF16)
            for h in heads] for s in chunks]
    intra = [[_dot(att[s][h], v_ref[rows[s], vs[h]]) for h in heads] for s in chunks]
    gain = [[_dot_tn(v_ref[rows[s], vs[h]], k_end[s][:, ks[h]]) for h in heads] for s in chunks]
    st = [st_ref[h] for h in heads]
    o = [[None] * GLA_HEADS for _ in chunks]
    for s in order:
        for h in heads:
            o[s][h] = intra[s][h] + _dot_nt(q_dec[s][:, ks[h]], st[h].astype(BF16))
            st[h] = st[h] * decay[s][:, ks[h]] + gain[s][h]
    for h in heads:
        st_ref[h] = st[h]
    for s in chunks:
        if final:
            tot = [o[s][h] + of_ref[rows[s], vs[h]] for h in heads]
            inv = [lax.rsqrt(jnp.mean(t * t, axis=-1, keepdims=True) + EPS) for t in tot]
            for h in heads:
                y = tot[h] * inv[h] * nw_ref[...]
                o_ref[rows[s], vs[h]] = (y * jax.nn.silu(gr_ref[rows[s], vs[h]].astype(F32))).astype(BF16)
        else:
            for h in heads:
                o_ref[rows[s], vs[h]] = o[s][h]


def _gla_pass(reverse, pqk, pv, plr, up, bias, tri, extra=None):
    T = pqk.shape[0]
    rows = GLA_STEP_CHUNKS * GLA_CHUNK
    n = T // rows
    final = extra is not None
    blk = (lambda i: n - 1 - i) if reverse else (lambda i: i)
    row = lambda w, col=0: pl.BlockSpec((rows, w), lambda i: (blk(i), col))
    in_specs = [row(GLA_QK_WIDTH, 0), row(GLA_QK_WIDTH, 1), row(GLA_WIDTH), row(2 * GLA_RANK),
                _resident(up.shape), _resident(bias.shape), _resident(tri.shape)]
    args = [pqk, pqk, pv, plr, up, bias, tri]
    if final:
        o_f, pr, norm_w = extra
        in_specs += [row(GLA_WIDTH), row(GLA_WIDTH), _resident(norm_w.shape)]
        args += [o_f, pr, norm_w]
    return pl.pallas_call(
        functools.partial(_gla_body, reverse, final),
        grid=(n,),
        in_specs=in_specs,
        out_specs=row(GLA_WIDTH),
        out_shape=jax.ShapeDtypeStruct((T, GLA_WIDTH), BF16 if final else F32),
        scratch_shapes=[pltpu.VMEM((GLA_HEADS, GLA_DV, GLA_DK), F32)],
        compiler_params=pltpu.CompilerParams(dimension_semantics=("arbitrary",),
                                             vmem_limit_bytes=VMEM_LIMIT),
        name="gla_bwd" if reverse else "gla_fwd",
    )(*args)


def _outproj_body(a_ref, g_ref, x_ref, w_ref, ln_ref, wr_hl_ref, wr_hi_ref, rb_ref, tri_ref,
                  h_ref, xn_ref, ri_ref, rw_ref, cnt_ref):
    tm = x_ref.shape[0]

    @pl.when(pl.program_id(0) == 0)
    def _():
        cnt_ref[...] = jnp.zeros_like(cnt_ref)

    h = x_ref[...] + _dot(jnp.concatenate([a_ref[...], g_ref[...]], axis=1), w_ref[...])
    h_ref[...] = h
    xn = h * lax.rsqrt(jnp.mean(h * h, axis=-1, keepdims=True) + EPS) * ln_ref[...]
    x_hi = xn.astype(BF16)
    xn_ref[...] = _pack_bf16_pairs(x_hi)
    x_lo = (xn - x_hi.astype(F32)).astype(BF16)
    l2 = _dot(x_hi, wr_hl_ref[...])
    logits = l2[:, :ROUTER_ROWS] + l2[:, ROUTER_ROWS:] + _dot(x_lo, wr_hi_ref[...])
    lt = jnp.transpose(logits) + rb_ref[...]

    gl = lt[0:N_GROUPS]
    gmax = jnp.max(gl, axis=0, keepdims=True)
    gi = lax.broadcasted_iota(jnp.int32, gl.shape, 0).astype(F32)
    g_sel = jnp.min(jnp.where(gl == gmax, gi, float(N_GROUPS)), axis=0, keepdims=True)
    g_gate = 1.0 / jnp.sum(jnp.exp(gl - gmax), axis=0, keepdims=True)

    el = lt[EXPERT_ROW0:EXPERT_ROW0 + N_EXPERTS]
    ei_int = lax.broadcasted_iota(jnp.int32, el.shape, 0)
    ei = ei_int.astype(F32)
    grp = (ei_int >> 3).astype(F32)
    cand = jnp.where(grp == g_sel, el, -jnp.inf)
    v1 = jnp.max(cand, axis=0, keepdims=True)
    e1 = jnp.min(jnp.where(cand == v1, ei, float(N_EXPERTS)), axis=0, keepdims=True)
    cand2 = jnp.where(ei == e1, -jnp.inf, cand)
    v2 = jnp.max(cand2, axis=0, keepdims=True)
    e2 = jnp.min(jnp.where(cand2 == v2, ei, float(N_EXPERTS)), axis=0, keepdims=True)
    d = jnp.exp(v2 - v1)
    w1 = g_gate / (1.0 + d)
    w2 = g_gate * d / (1.0 + d)

    oh1 = (ei == e1).astype(F32)
    oh2 = (ei == e2).astype(F32)
    cnt = oh1 + oh2
    before = _dot(cnt.astype(BF16), tri_ref[...]) + cnt_ref[:, 0:1]
    r1 = jnp.sum(oh1 * before, axis=0, keepdims=True)
    r2 = jnp.sum(oh2 * before, axis=0, keepdims=True)
    cnt_ref[...] = cnt_ref[...] + jnp.sum(cnt, axis=1, keepdims=True)

    ri_ref[...] = jnp.concatenate([e1, e2, r1, r2, jnp.zeros((4, tm), F32)], axis=0).astype(jnp.int32)
    rw_ref[...] = jnp.concatenate([w1, w2, jnp.zeros((6, tm), F32)], axis=0)


def _outproj_router(attn, gla, x2, w_out, ln_w, wr_hl, wr_hi, rbias, tri):
    T = x2.shape[0]
    tm = OUT_ROWS
    row = lambda w: pl.BlockSpec((tm, w), lambda i: (i, 0))
    col = lambda r: pl.BlockSpec((r, tm), lambda i: (0, i))
    return pl.pallas_call(
        _outproj_body,
        grid=(T // tm,),
        in_specs=[row(ATTN_WIDTH), row(GLA_WIDTH), row(D_MODEL),
                  _resident(w_out.shape), _resident(ln_w.shape),
                  _resident(wr_hl.shape), _resident(wr_hi.shape), _resident(rbias.shape),
                  _resident(tri.shape)],
        out_specs=[row(D_MODEL), row(D_MODEL // 2), col(8), col(8),
                   pl.BlockSpec((N_EXPERTS, 128), lambda i: (0, 0))],
        out_shape=[jax.ShapeDtypeStruct((T, D_MODEL), F32),
                   jax.ShapeDtypeStruct((T, D_MODEL // 2), jnp.uint32),
                   jax.ShapeDtypeStruct((8, T), jnp.int32),
                   jax.ShapeDtypeStruct((8, T), F32),
                   jax.ShapeDtypeStruct((N_EXPERTS, 128), F32)],
        compiler_params=pltpu.CompilerParams(dimension_semantics=("arbitrary",),
                                             vmem_limit_bytes=VMEM_LIMIT),
        name="outproj_router",
    )(attn, gla, x2, w_out, ln_w, wr_hl, wr_hi, rbias, tri)


def _row_copy_start(src, s, dst, d, sem, queue):
    pltpu.async_copy(src.at[pl.ds(s, 1)], dst.at[pl.ds(d, 1)], sem, priority=queue)


def _moe_body(be_ref, nu_ref, slot_ref, lo_ref, hi_ref, seq_ref, nxt_ref, xn_ref, wg_ref, wu_ref, wd_ref,
              y_ref, xbuf, xb_ref, tok_ref, wg_buf, wu_buf, wd_buf, wu_stage, wd_stage, sem, wsem, dsem):
    bm = MOE_ROWS
    b = pl.program_id(0)
    n_used = nu_ref[0]
    cur = b % MOE_ROW_BUFFERS
    expert = be_ref[b]
    wslot = seq_ref[expert] % 2
    first_of_expert = (b == 0) | (be_ref[jnp.maximum(b - 1, 0)] != expert)

    def weight_copies(e, slot):
        return [pltpu.make_async_copy(wg_ref.at[e], wg_buf.at[slot], wsem.at[slot]),
                pltpu.make_async_copy(wu_ref.at[e], wu_stage, dsem.at[0]),
                pltpu.make_async_copy(wd_ref.at[e], wd_stage, dsem.at[1])]

    def gather_start(block, buf, unrolled):
        def one(i, queue):
            _row_copy_start(xn_ref, tok_ref[block * bm + i], xbuf.at[buf], i, sem.at[buf], queue)
        if unrolled:
            for i in range(bm):
                one(i, i % 2)
        else:
            def pair(p, c):
                one(2 * p, 0)
                one(2 * p + 1, 1)
                return c
            lax.fori_loop(0, bm // 2, pair, 0)

    def gather_wait(buf):
        pltpu.make_async_copy(xn_ref.at[pl.ds(0, bm)], xbuf.at[buf], sem.at[buf]).wait()

    @pl.when(b == 0)
    def _():
        for c in weight_copies(expert, wslot):
            c.start()
        def per_expert(e, c):
            def zero(s, c2):
                tok_ref[s] = 0
                return c2
            return lax.fori_loop(lo_ref[e], hi_ref[e], zero, c)
        lax.fori_loop(0, N_EXPERTS, per_expert, 0)

        def fill(t, c):
            for k in range(TOP_K):
                tok_ref[slot_ref[TOP_K * t + k]] = t
            return c
        lax.fori_loop(0, slot_ref.shape[0] // TOP_K, fill, 0, unroll=8)
        gather_start(0, 0, False)

        @pl.when(n_used > 1)
        def _():
            gather_start(1, 1, False)

    @pl.when((b < n_used) & first_of_expert)
    def _():
        for c in weight_copies(expert, wslot):
            c.wait()
        for r in range(0, D_MODEL, MOE_CAST_ROWS):
            wu_buf[wslot, r:r + MOE_CAST_ROWS, :] = wu_stage[r:r + MOE_CAST_ROWS, :].astype(BF16)
        for r in range(0, D_FF, MOE_CAST_ROWS):
            wd_buf[wslot, r:r + MOE_CAST_ROWS, :] = wd_stage[r:r + MOE_CAST_ROWS, :].astype(BF16)
        following = nxt_ref[expert]

        @pl.when(following >= 0)
        def _():
            for c in weight_copies(following, 1 - wslot):
                c.start()

    def stage_rows():
        gather_wait(cur)
        lo, hi = _unpack_bf16_pairs(xbuf[cur])
        xb_ref[:, :D_MODEL // 2] = lo.astype(BF16)
        xb_ref[:, D_MODEL // 2:] = hi.astype(BF16)

    def experts():
        x = xb_ref[...]
        hid = jax.nn.silu(_dot(x, wg_buf[wslot])) * _dot(x, wu_buf[wslot])
        y = _dot(hid.astype(BF16), wd_buf[wslot])
        y_ref[...] = _pack_bf16_pairs(y.astype(BF16))

    @pl.when(b + 2 < n_used)
    def _():
        stage_rows()
        gather_start(b + 2, (b + 2) % MOE_ROW_BUFFERS, True)
        experts()

    @pl.when((b < n_used) & (b + 2 >= n_used))
    def _():
        stage_rows()
        experts()

    @pl.when(b >= n_used)
    def _():
        y_ref[...] = jnp.zeros_like(y_ref)


def _moe(block_e, n_used, slots, pad_lo, pad_hi, expert_seq, expert_next, xn, wg, wu, wd, n_slots):
    bm = MOE_ROWS
    any_spec = pl.BlockSpec(memory_space=pl.ANY)
    grid_spec = pltpu.PrefetchScalarGridSpec(
        num_scalar_prefetch=7,
        grid=(n_slots // bm,),
        in_specs=[any_spec] * 4,
        out_specs=pl.BlockSpec((bm, D_MODEL // 2), lambda b, *_: (b, 0)),
        scratch_shapes=[pltpu.VMEM((MOE_ROW_BUFFERS, bm, D_MODEL // 2), jnp.uint32),
                        pltpu.VMEM((bm, D_MODEL), BF16),
                        pltpu.SMEM((n_slots,), jnp.int32),
                        pltpu.VMEM((2, D_MODEL, D_FF), BF16), pltpu.VMEM((2, D_MODEL, D_FF), BF16),
                        pltpu.VMEM((2, D_FF, D_MODEL), BF16),
                        pltpu.VMEM((D_MODEL, D_FF), F32), pltpu.VMEM((D_FF, D_MODEL), F32),
                        pltpu.SemaphoreType.DMA((MOE_ROW_BUFFERS,)), pltpu.SemaphoreType.DMA((2,)),
                        pltpu.SemaphoreType.DMA((2,))],
    )
    return pl.pallas_call(
        _moe_body,
        grid_spec=grid_spec,
        out_shape=jax.ShapeDtypeStruct((n_slots, D_MODEL // 2), jnp.uint32),
        compiler_params=pltpu.CompilerParams(dimension_semantics=("arbitrary",),
                                             vmem_limit_bytes=VMEM_LIMIT),
        name="moe",
    )(block_e, n_used, slots, pad_lo, pad_hi, expert_seq, expert_next, xn, wg, wu, wd)


def _combine_body(slot_ref, h_ref, w_ref, ys_ref, o_ref, g_ref, sem):
    tb = COMBINE_ROWS
    i = pl.program_id(0)
    cur = i % 2

    def gather_start(block, buf):
        def one(t, c):
            for k in range(TOP_K):
                _row_copy_start(ys_ref, slot_ref[TOP_K * (block * tb + t) + k], g_ref.at[buf, k], t,
                                sem.at[buf], k)
            return c
        lax.fori_loop(0, tb, one, 0, unroll=8)

    @pl.when(i == 0)
    def _():
        gather_start(0, 0)

    @pl.when(i + 1 < pl.num_programs(0))
    def _():
        gather_start(i + 1, 1 - cur)

    for k in range(TOP_K):
        pltpu.make_async_copy(ys_ref.at[pl.ds(0, tb)], g_ref.at[cur, k], sem.at[cur]).wait()
    lo0, hi0 = _unpack_bf16_pairs(g_ref[cur, 0])
    lo1, hi1 = _unpack_bf16_pairs(g_ref[cur, 1])
    w0, w1 = w_ref[:, 0:1], w_ref[:, 1:2]
    half = D_MODEL // 2
    o_ref[:, :half] = h_ref[:, :half] + w0 * lo0 + w1 * lo1
    o_ref[:, half:] = h_ref[:, half:] + w0 * hi0 + w1 * hi1


def _combine(slots, h, wcol, ys):
    T = h.shape[0]
    tb = COMBINE_ROWS
    grid_spec = pltpu.PrefetchScalarGridSpec(
        num_scalar_prefetch=1,
        grid=(T // tb,),
        in_specs=[pl.BlockSpec((tb, D_MODEL), lambda i, s: (i, 0)),
                  pl.BlockSpec((tb, TOP_K), lambda i, s: (i, 0)),
                  pl.BlockSpec(memory_space=pl.ANY)],
        out_specs=pl.BlockSpec((tb, D_MODEL), lambda i, s: (i, 0)),
        scratch_shapes=[pltpu.VMEM((2, TOP_K, tb, D_MODEL // 2), jnp.uint32),
                        pltpu.SemaphoreType.DMA((2,))],
    )
    return pl.pallas_call(
        _combine_body,
        grid_spec=grid_spec,
        out_shape=jax.ShapeDtypeStruct((T, D_MODEL), F32),
        compiler_params=pltpu.CompilerParams(dimension_semantics=("arbitrary",),
                                             vmem_limit_bytes=VMEM_LIMIT),
        name="combine",
    )(slots, h, wcol, ys)


def _rope_tables(T, rows):
    inv_freq = jnp.power(jnp.float32(ROPE_THETA),
                         -jnp.arange(ROPE_HALF, dtype=F32) * (2.0 / ROPE_DIM))

    def tables(pos):
        ang = pos.astype(F32)[:, None] * inv_freq[None, :]
        rest = (pos.shape[0], HEAD_DIM - ROPE_DIM)
        cos = jnp.concatenate([jnp.cos(ang), jnp.cos(ang), jnp.ones(rest, F32)], axis=1)
        sin = jnp.concatenate([jnp.sin(ang), jnp.sin(ang), jnp.zeros(rest, F32)], axis=1)
        return cos, sin

    step_cos, step_sin = tables(jnp.arange(T // rows) * rows)
    row_cos, row_sin = tables(jnp.arange(rows))
    return step_cos[:, None, :], step_sin[:, None, :], row_cos, row_sin


def _split_hi_lo(w):
    hi = w.astype(BF16)
    lo = (w - hi.astype(F32)).astype(BF16)
    return hi, lo


def _layer(x2, ln1_w, w_in, q_norm_w, k_norm_w, attn_sink, attn_out_norm_w, gate_up_f, gate_bias_f,
           gate_up_b, gate_bias_b, gla_out_norm_w, w_out, ln2_w, w_group, b_group, w_router, b_router,
           w_gate_e, w_up_e, w_down_e):
    T = x2.shape[0]
    row = lambda v: v.reshape(1, -1).astype(F32)

    c0, c1, c2, c3 = A_WIDTH, A_WIDTH + 2 * GLA_QK_WIDTH, A_WIDTH + 2 * GLA_QK_WIDTH + GLA_WIDTH, \
        A_WIDTH + 2 * GLA_QK_WIDTH + 2 * GLA_WIDTH
    wa, wqk, wv, wr, wl = (w_in[:, a:b].astype(BF16) for a, b in
                           ((0, c0), (c0, c1), (c1, c2), (c2, c3), (c3, w_in.shape[1])))
    pa, pqk, pv, pr, plr, wg_b = _inproj(x2, row(ln1_w), wa, wqk, wv, wr, wl, row(q_norm_w), row(k_norm_w),
                                         _rope_tables(T, PROJ_ROWS), _cast_rows_job(w_gate_e, T // PROJ_ROWS))

    attn = _attention(pa, attn_sink.astype(F32), row(attn_out_norm_w))

    C = GLA_CHUNK
    ones = jnp.ones((C, C), F32)
    zr = jnp.zeros((GLA_RANK, GLA_QK_WIDTH), F32)
    up_f = jnp.concatenate([gate_up_f.astype(F32), zr], axis=0)
    up_b = jnp.concatenate([zr, gate_up_b.astype(F32)], axis=0)

    def up_pieces(up):
        hi, lo = _split_hi_lo(up)
        return jnp.concatenate([hi, hi, lo], axis=0)

    o_f = _gla_pass(False, pqk, pv, plr, up_pieces(up_f), row(gate_bias_f), jnp.tril(ones).astype(BF16))
    gla = _gla_pass(True, pqk, pv, plr, up_pieces(up_b), row(gate_bias_b), jnp.triu(ones).astype(BF16),
                    extra=(o_f, pr, row(gla_out_norm_w)))

    wr_full = jnp.zeros((D_MODEL, ROUTER_ROWS), F32)
    wr_full = wr_full.at[:, :N_GROUPS].set(w_group.astype(F32))
    wr_full = wr_full.at[:, EXPERT_ROW0:EXPERT_ROW0 + N_EXPERTS].set(w_router.astype(F32))
    wr_hi, wr_lo = _split_hi_lo(wr_full)
    rbias = jnp.zeros((ROUTER_ROWS, 1), F32)
    rbias = rbias.at[:N_GROUPS, 0].set(b_group.astype(F32))
    rbias = rbias.at[EXPERT_ROW0:EXPERT_ROW0 + N_EXPERTS, 0].set(b_router.astype(F32))
    tm = OUT_ROWS
    earlier = jnp.triu(jnp.ones((tm, tm), F32), k=1).astype(BF16)
    h, xn, r_int, r_w, counts = _outproj_router(
        attn, gla, x2, w_out.astype(BF16), row(ln2_w),
        jnp.concatenate([wr_hi, wr_lo], axis=1), wr_hi, rbias, earlier)

    bm = MOE_ROWS
    n_blocks = (T * TOP_K) // bm + N_EXPERTS
    cnt = counts[:, 0].astype(jnp.int32)
    padded = (cnt + bm - 1) // bm * bm
    pad_end = jnp.cumsum(padded)
    pad_start = pad_end - padded
    n_used = (pad_end[-1] // bm).astype(jnp.int32).reshape(1)
    block_e = jnp.minimum(jnp.sum(pad_end[None, :] <= (jnp.arange(n_blocks) * bm)[:, None], axis=1),
                          N_EXPERTS - 1).astype(jnp.int32)
    e_sel = r_int[0:TOP_K].T
    start_sel = jnp.sum(jnp.where(e_sel[..., None] == jnp.arange(N_EXPERTS), pad_start, 0), axis=-1)
    slots = (start_sel + r_int[TOP_K:2 * TOP_K].T).reshape(-1).astype(jnp.int32)

    nonempty = cnt > 0
    expert_seq = (jnp.cumsum(nonempty) - 1).astype(jnp.int32)
    ids = jnp.arange(N_EXPERTS)
    later = (ids[None, :] > ids[:, None]) & nonempty[None, :]
    expert_next = jnp.where(later.any(axis=1), jnp.argmax(later, axis=1), -1).astype(jnp.int32)
    ys = _moe(block_e, n_used, slots, (pad_start + cnt).astype(jnp.int32), pad_end.astype(jnp.int32),
              expert_seq, expert_next, xn,
              wg_b.reshape(w_gate_e.shape), w_up_e.astype(F32), w_down_e.astype(F32), n_blocks * bm)
    return _combine(slots, h, r_w[0:TOP_K].T, ys)


def kernel(x, ln1_w, w_in, q_norm_w, k_norm_w, attn_sink, attn_out_norm_w, gla_gate_up_f, gla_gate_bias_f,
           gla_gate_up_b, gla_gate_bias_b, gla_out_norm_w, w_out, ln2_w, w_group, b_group, w_router,
           b_router, w_gate_e, w_up_e, w_down_e):
    B, S, D = x.shape
    h = x.reshape(B * S, D)
    assert B == 1
    for l in range(ln1_w.shape[0]):
        h = _layer(h, ln1_w[l], w_in[l], q_norm_w[l], k_norm_w[l], attn_sink[l], attn_out_norm_w[l],
                   gla_gate_up_f[l], gla_gate_bias_f[l], gla_gate_up_b[l], gla_gate_bias_b[l],
                   gla_out_norm_w[l], w_out[l], ln2_w[l], w_group[l], b_group[l], w_router[l],
                   b_router[l], w_gate_e[l], w_up_e[l], w_down_e[l])
    return h.reshape(B, S, D)
```

```python
import functools
from typing import NamedTuple

import jax
import jax.numpy as jnp
from jax import lax
from jax.experimental import pallas as pl
from jax.experimental.pallas import tpu as pltpu

F32 = jnp.float32
BF16 = jnp.bfloat16

EPS = 1e-6
D_MODEL = 2048

ATTN_Q_HEADS = 8
ATTN_KV_HEADS = 2
ATTN_GROUP = ATTN_Q_HEADS // ATTN_KV_HEADS
HEAD_DIM = 128
WINDOW = 128
ATTN_BLOCK = 128
ATTN_STEP_BLOCKS = 8
ROPE_THETA = 500000.0
ROPE_DIM = HEAD_DIM // 4
ROPE_HALF = ROPE_DIM // 2
ATTN_WIDTH = ATTN_Q_HEADS * HEAD_DIM
KV_WIDTH = ATTN_KV_HEADS * HEAD_DIM
A_WIDTH = ATTN_WIDTH + 2 * KV_WIDTH

GLA_HEADS = 4
GLA_DK = 128
GLA_DV = 256
GLA_RANK = 16
GLA_TAU = 16.0
GLA_QK_WIDTH = GLA_HEADS * GLA_DK
GLA_WIDTH = GLA_HEADS * GLA_DV
GLA_CHUNK = 128
GLA_STEP_CHUNKS = 8

N_GROUPS = 4
EXPERTS_PER_GROUP = 8
N_EXPERTS = N_GROUPS * EXPERTS_PER_GROUP
EXPERT_GROUP_SHIFT = EXPERTS_PER_GROUP.bit_length() - 1
assert 1 << EXPERT_GROUP_SHIFT == EXPERTS_PER_GROUP
TOP_K = 2
D_FF = 1024
ROUTER_ROWS = 128
EXPERT_ROW0 = 8

PROJ_ROWS = 256
OUT_ROWS = 256
MOE_ROWS = 256
MOE_ROW_BUFFERS = 3
MOE_CAST_ROWS = 128
COMBINE_ROWS = 256

VMEM_LIMIT = 56 * 1024 * 1024


def _dot(a, b):
    return jnp.dot(a, b, preferred_element_type=F32)


def _dot_nt(a, b):
    return lax.dot_general(a, b, (((1,), (1,)), ((), ())), preferred_element_type=F32)


def _dot_tn(a, b):
    return lax.dot_general(a, b, (((0,), (0,)), ((), ())), preferred_element_type=F32)


def _pack_bf16_pairs(x):
    n = x.shape[1] // 2
    lo = lax.bitcast_convert_type(x[:, :n].astype(F32), jnp.uint32) >> 16
    hi = lax.bitcast_convert_type(x[:, n:].astype(F32), jnp.uint32) & jnp.uint32(0xFFFF0000)
    return lo | hi


def _unpack_bf16_pairs(w):
    lo = lax.bitcast_convert_type(w << 16, F32)
    hi = lax.bitcast_convert_type(w & jnp.uint32(0xFFFF0000), F32)
    return lo, hi


class CastJob(NamedTuple):
    array: jax.Array
    in_spec: pl.BlockSpec
    out_spec: pl.BlockSpec
    out_shape: jax.ShapeDtypeStruct


def _cast_rows_job(w, n_steps):
    w2 = w.astype(F32).reshape(-1, w.shape[-1])
    rows = w2.shape[0] // n_steps
    assert rows * n_steps == w2.shape[0] and rows % 16 == 0
    spec = pl.BlockSpec((rows, w2.shape[1]), lambda i, *_: (i, 0))
    return CastJob(w2, spec, spec, jax.ShapeDtypeStruct(w2.shape, BF16))


def _resident(shape):
    nd = len(shape)
    return pl.BlockSpec(shape, lambda *_: (0,) * nd, pipeline_mode=pl.Buffered(1))


def _inproj_body(x_ref, ln_ref, wa_ref, wqk_ref, wv_ref, wr_ref, wl_ref, qn_ref, kn_ref,
                 step_cos_ref, step_sin_ref, row_cos_ref, row_sin_ref, wcast_ref,
                 oa_ref, oqk_ref, ov_ref, or_ref, ol_ref, wcast_out_ref):
    wcast_out_ref[...] = wcast_ref[...].astype(BF16)
    x = x_ref[...]
    ms = jnp.mean(x * x, axis=-1, keepdims=True)
    xn = (x * lax.rsqrt(ms + EPS) * ln_ref[...]).astype(BF16)
    acc = _dot(xn, wa_ref[...])
    ca, sn = step_cos_ref[0], step_sin_ref[0]
    cb, sb_ = row_cos_ref[...], row_sin_ref[...]
    cos = cb * ca - sb_ * sn
    sin = sb_ * ca + cb * sn
    lane = lax.broadcasted_iota(jnp.int32, (1, HEAD_DIM), 1)
    sa = jnp.where((lane >= ROPE_HALF) & (lane < ROPE_DIM), sin, 0.0)
    sb = jnp.where(lane < ROPE_HALF, -sin, 0.0)
    for c in range(ATTN_Q_HEADS + ATTN_KV_HEADS):
        xh = acc[:, c * HEAD_DIM:(c + 1) * HEAD_DIM]
        w = qn_ref[...] if c < ATTN_Q_HEADS else kn_ref[...]
        y = xh * lax.rsqrt(jnp.mean(xh * xh, axis=-1, keepdims=True) + EPS) * w
        y = (y * cos + pltpu.roll(y, ROPE_HALF, 1) * sa
             + pltpu.roll(y, HEAD_DIM - ROPE_HALF, 1) * sb)
        if c < ATTN_Q_HEADS:
            y = y * (HEAD_DIM ** -0.5)
        oa_ref[:, c * HEAD_DIM:(c + 1) * HEAD_DIM] = y.astype(BF16)
    oa_ref[:, ATTN_WIDTH + KV_WIDTH:] = acc[:, ATTN_WIDTH + KV_WIDTH:].astype(BF16)
    oqk_ref[...] = _dot(xn, wqk_ref[...]).astype(BF16)
    ov_ref[...] = _dot(xn, wv_ref[...]).astype(BF16)
    or_ref[...] = _dot(xn, wr_ref[...]).astype(BF16)
    ol_ref[...] = _dot(xn, wl_ref[...])


def _inproj(x2, ln_w, wa, wqk, wv, wr, wl, qn, kn, rope, cast):
    T = x2.shape[0]
    tm = PROJ_ROWS
    row = lambda w: pl.BlockSpec((tm, w), lambda i: (i, 0))
    step_row = pl.BlockSpec((1, 1, HEAD_DIM), lambda i: (i, 0, 0))
    return pl.pallas_call(
        _inproj_body,
        grid=(T // tm,),
        in_specs=[row(D_MODEL), _resident((1, D_MODEL)),
                  _resident(wa.shape), _resident(wqk.shape), _resident(wv.shape),
                  _resident(wr.shape), _resident(wl.shape),
                  _resident((1, HEAD_DIM)), _resident((1, HEAD_DIM)),
                  step_row, step_row, _resident((tm, HEAD_DIM)), _resident((tm, HEAD_DIM)), cast.in_spec],
        out_specs=[row(A_WIDTH), row(2 * GLA_QK_WIDTH), row(GLA_WIDTH), row(GLA_WIDTH),
                   row(2 * GLA_RANK), cast.out_spec],
        out_shape=[jax.ShapeDtypeStruct((T, A_WIDTH), BF16),
                   jax.ShapeDtypeStruct((T, 2 * GLA_QK_WIDTH), BF16),
                   jax.ShapeDtypeStruct((T, GLA_WIDTH), BF16),
                   jax.ShapeDtypeStruct((T, GLA_WIDTH), BF16),
                   jax.ShapeDtypeStruct((T, 2 * GLA_RANK), F32), cast.out_shape],
        compiler_params=pltpu.CompilerParams(dimension_semantics=("parallel",),
                                             vmem_limit_bytes=VMEM_LIMIT),
        name="inproj",
    )(x2, ln_w, wa, wqk, wv, wr, wl, qn, kn, *rope, cast.array)


def _attn_body(sink_ref, q_ref, kp_ref, kc_ref, kn_ref, vp_ref, vc_ref, vn_ref, nw_ref, o_ref):
    n = pl.program_id(0)
    nb = pl.num_programs(0)
    rows = ATTN_GROUP * ATTN_BLOCK
    keys = 3 * ATTN_BLOCK
    r = lax.broadcasted_iota(jnp.int32, (rows, keys), 0) & (ATTN_BLOCK - 1)
    c = lax.broadcasted_iota(jnp.int32, (rows, keys), 1)
    band = (c >= r + (ATTN_BLOCK - WINDOW)) & (c <= r + (ATTN_BLOCK + WINDOW))
    edge_mask = [band & ((c >= ATTN_BLOCK) | (n > 0))] + [band] * (ATTN_STEP_BLOCKS - 2) + \
                [band & ((c < 2 * ATTN_BLOCK) | (n < nb - 1))]

    def window(j, prev_ref, cur_ref, next_ref, ks):
        blocks = [prev_ref[:, ks]] + [cur_ref[i * ATTN_BLOCK:(i + 1) * ATTN_BLOCK, ks]
                                      for i in range(ATTN_STEP_BLOCKS)] + [next_ref[:, ks]]
        return jnp.concatenate(blocks[j:j + 3], axis=0)

    chains = [(j, g) for j in range(ATTN_STEP_BLOCKS) for g in range(ATTN_KV_HEADS)]
    heads_of = lambda g: range(g * ATTN_GROUP, (g + 1) * ATTN_GROUP)
    ks_of = lambda g: slice(g * HEAD_DIM, (g + 1) * HEAD_DIM)
    qrows_of = lambda j: slice(j * ATTN_BLOCK, (j + 1) * ATTN_BLOCK)
    s = [jnp.where(edge_mask[j],
                   _dot_nt(jnp.concatenate([q_ref[qrows_of(j), h * HEAD_DIM:(h + 1) * HEAD_DIM]
                                            for h in heads_of(g)], axis=0),
                           window(j, kp_ref, kc_ref, kn_ref, ks_of(g))), -jnp.inf)
         for j, g in chains]
    lane_blocks = keys // HEAD_DIM
    sink = [jnp.concatenate([jnp.full((ATTN_BLOCK, HEAD_DIM), sink_ref[h], F32) for h in heads_of(g)], axis=0)
            for j, g in chains]
    m = [jnp.maximum(jnp.broadcast_to(jnp.max(si, axis=-1, keepdims=True), (rows, HEAD_DIM)), sk)
         for si, sk in zip(s, sink)]
    p = [jnp.concatenate([jnp.exp(si[:, c * HEAD_DIM:(c + 1) * HEAD_DIM] - mi) for c in range(lane_blocks)]
                         + [jnp.exp(sk - mi)], axis=1).astype(BF16)
         for si, sk, mi in zip(s, sink, m)]
    sink_rows = jnp.concatenate([jnp.zeros((HEAD_DIM, HEAD_DIM), BF16),
                                 jnp.full((HEAD_DIM, HEAD_DIM), 1.0 / HEAD_DIM, BF16)], axis=1)

    def values_and_ones(j, g):
        v3 = window(j, vp_ref, vc_ref, vn_ref, ks_of(g))
        return jnp.concatenate([jnp.concatenate([v3, jnp.ones((keys, HEAD_DIM), BF16)], axis=1), sink_rows],
                               axis=0)

    o = []
    for (j, g), pi in zip(chains, p):
        num_den = _dot(pi, values_and_ones(j, g))
        o.append(num_den[:, :HEAD_DIM] / num_den[:, HEAD_DIM:])
    for j in range(ATTN_STEP_BLOCKS):
        outs = [o[chains.index((j, g))][i * ATTN_BLOCK:(i + 1) * ATTN_BLOCK]
                for g in range(ATTN_KV_HEADS) for i in range(ATTN_GROUP)]
        a = jnp.concatenate(outs, axis=1)
        a = a * lax.rsqrt(jnp.mean(a * a, axis=-1, keepdims=True) + EPS) * nw_ref[...]
        o_ref[qrows_of(j), :] = a.astype(BF16)


def _attention(pa, sink, norm_w):
    T = pa.shape[0]
    sb = ATTN_STEP_BLOCKS
    nb = T // ATTN_BLOCK
    kcol = ATTN_WIDTH // KV_WIDTH
    vcol = kcol + 1
    prev = lambda col: pl.BlockSpec((ATTN_BLOCK, KV_WIDTH), lambda n, s: (jnp.maximum(sb * n - 1, 0), col))
    nxt = lambda col: pl.BlockSpec((ATTN_BLOCK, KV_WIDTH),
                                   lambda n, s: (jnp.minimum(sb * n + sb, nb - 1), col))
    cur = lambda col: pl.BlockSpec((sb * ATTN_BLOCK, KV_WIDTH), lambda n, s: (n, col))
    grid_spec = pltpu.PrefetchScalarGridSpec(
        num_scalar_prefetch=1,
        grid=(nb // sb,),
        in_specs=[pl.BlockSpec((sb * ATTN_BLOCK, ATTN_WIDTH), lambda n, s: (n, 0)),
                  prev(kcol), cur(kcol), nxt(kcol), prev(vcol), cur(vcol), nxt(vcol),
                  pl.BlockSpec((1, ATTN_WIDTH), lambda n, s: (0, 0))],
        out_specs=pl.BlockSpec((sb * ATTN_BLOCK, ATTN_WIDTH), lambda n, s: (n, 0)),
    )
    return pl.pallas_call(
        _attn_body,
        grid_spec=grid_spec,
        out_shape=jax.ShapeDtypeStruct((T, ATTN_WIDTH), BF16),
        compiler_params=pltpu.CompilerParams(dimension_semantics=("parallel",),
                                             vmem_limit_bytes=VMEM_LIMIT),
        name="attention",
    )(sink, pa, pa, pa, pa, pa, pa, pa, norm_w)


def _gla_body(reverse, final, *refs):
    if final:
        (q_ref, k_ref, v_ref, lr_ref, up_ref, bias_ref, tri_ref, of_ref, gr_ref, nw_ref,
         o_ref, st_ref) = refs
    else:
        q_ref, k_ref, v_ref, lr_ref, up_ref, bias_ref, tri_ref, o_ref, st_ref = refs
    C = GLA_CHUNK
    R = GLA_STEP_CHUNKS

    @pl.when(pl.program_id(0) == 0)
    def _():
        st_ref[...] = jnp.zeros_like(st_ref)

    chunks = range(R)
    order = tuple(reversed(chunks)) if reverse else tuple(chunks)
    rows = [slice(s * C, (s + 1) * C) for s in chunks]
    heads = range(GLA_HEADS)
    ks = [slice(h * GLA_DK, (h + 1) * GLA_DK) for h in heads]
    vs = [slice(h * GLA_DV, (h + 1) * GLA_DV) for h in heads]

    lr = lr_ref[...]
    lr_hi = lr.astype(BF16)
    lr_lo = (lr - lr_hi.astype(F32)).astype(BF16)
    g = _dot(jnp.concatenate([lr_hi, lr_lo, lr_hi], axis=1), up_ref[...]) + bias_ref[...]
    la = jax.nn.log_sigmoid(g) * (1.0 / GLA_TAU)
    tri = tri_ref[...]
    hi, lo = _split_hi_lo(la)
    b = [_dot(tri, hi[rows[s]]) + _dot(tri, lo[rows[s]]) for s in chunks]
    end = 0 if reverse else C - 1
    b_end = [b[s][end:end + 1] for s in chunks]
    b_mid = [b[s][C // 2:C // 2 + 1] for s in chunks]
    scale = GLA_DK ** -0.5
    q = [q_ref[rows[s], :].astype(F32) * scale for s in chunks]
    k = [k_ref[rows[s], :].astype(F32) for s in chunks]
    q_in = [(q[s] * jnp.exp(b[s] - b_mid[s])).astype(BF16) for s in chunks]
    k_in = [(k[s] * jnp.exp(b_mid[s] - b[s])).astype(BF16) for s in chunks]
    q_dec = [(q[s] * jnp.exp(b[s])).astype(BF16) for s in chunks]
    k_end = [(k[s] * jnp.exp(b_end[s] - b[s])).astype(BF16) for s in chunks]
    decay = [jnp.exp(b_end[s]) for s in chunks]
    ri = lax.broadcasted_iota(jnp.int32, (C, C), 0)
    ci = lax.broadcasted_iota(jnp.int32, (C, C), 1)
    causal = (ri <= ci) if reverse else (ri >= ci)
    att = [[jnp.where(causal, _dot_nt(q_in[s][:, ks[h]], k_in[s][:, ks[h]]), 0.0).astype(BF16)
            for h in heads] for s in chunks]
    intra = [[_dot(att[s][h], v_ref[rows[s], vs[h]]) for h in heads] for s in chunks]
    gain = [[_dot_tn(v_ref[rows[s], vs[h]], k_end[s][:, ks[h]]) for h in heads] for s in chunks]
    st = [st_ref[h] for h in heads]
    o = [[None] * GLA_HEADS for _ in chunks]
    for s in order:
        for h in heads:
            o[s][h] = intra[s][h] + _dot_nt(q_dec[s][:, ks[h]], st[h].astype(BF16))
            st[h] = st[h] * decay[s][:, ks[h]] + gain[s][h]
    for h in heads:
        st_ref[h] = st[h]
    for s in chunks:
        if final:
            tot = [o[s][h] + of_ref[rows[s], vs[h]] for h in heads]
            inv = [lax.rsqrt(jnp.mean(t * t, axis=-1, keepdims=True) + EPS) for t in tot]
            for h in heads:
                y = tot[h] * inv[h] * nw_ref[...]
                o_ref[rows[s], vs[h]] = (y * jax.nn.silu(gr_ref[rows[s], vs[h]].astype(F32))).astype(BF16)
        else:
            for h in heads:
                o_ref[rows[s], vs[h]] = o[s][h]


def _gla_pass(reverse, pqk, pv, plr, up, bias, tri, extra=None):
    T = pqk.shape[0]
    rows = GLA_STEP_CHUNKS * GLA_CHUNK
    n = T // rows
    final = extra is not None
    blk = (lambda i: n - 1 - i) if reverse else (lambda i: i)
    row = lambda w, col=0: pl.BlockSpec((rows, w), lambda i: (blk(i), col))
    in_specs = [row(GLA_QK_WIDTH, 0), row(GLA_QK_WIDTH, 1), row(GLA_WIDTH), row(2 * GLA_RANK),
                _resident(up.shape), _resident(bias.shape), _resident(tri.shape)]
    args = [pqk, pqk, pv, plr, up, bias, tri]
    if final:
        o_f, pr, norm_w = extra
        in_specs += [row(GLA_WIDTH), row(GLA_WIDTH), _resident(norm_w.shape)]
        args += [o_f, pr, norm_w]
    return pl.pallas_call(
        functools.partial(_gla_body, reverse, final),
        grid=(n,),
        in_specs=in_specs,
        out_specs=row(GLA_WIDTH),
        out_shape=jax.ShapeDtypeStruct((T, GLA_WIDTH), BF16 if final else F32),
        scratch_shapes=[pltpu.VMEM((GLA_HEADS, GLA_DV, GLA_DK), F32)],
        compiler_params=pltpu.CompilerParams(dimension_semantics=("arbitrary",),
                                             vmem_limit_bytes=VMEM_LIMIT),
        name="gla_bwd" if reverse else "gla_fwd",
    )(*args)


def _outproj_body(a_ref, g_ref, x_ref, w_ref, ln_ref, wr_hl_ref, wr_hi_ref, rb_ref, tri_ref,
                  h_ref, xn_ref, ri_ref, rw_ref, cnt_ref):
    tm = x_ref.shape[0]

    @pl.when(pl.program_id(0) == 0)
    def _():
        cnt_ref[...] = jnp.zeros_like(cnt_ref)

    h = x_ref[...] + _dot(jnp.concatenate([a_ref[...], g_ref[...]], axis=1), w_ref[...])
    h_ref[...] = h
    xn = h * lax.rsqrt(jnp.mean(h * h, axis=-1, keepdims=True) + EPS) * ln_ref[...]
    x_hi = xn.astype(BF16)
    xn_ref[...] = _pack_bf16_pairs(x_hi)
    x_lo = (xn - x_hi.astype(F32)).astype(BF16)
    l2 = _dot(x_hi, wr_hl_ref[...])
    logits = l2[:, :ROUTER_ROWS] + l2[:, ROUTER_ROWS:] + _dot(x_lo, wr_hi_ref[...])
    lt = jnp.transpose(logits) + rb_ref[...]

    gl = lt[0:N_GROUPS]
    gmax = jnp.max(gl, axis=0, keepdims=True)
    gi = lax.broadcasted_iota(jnp.int32, gl.shape, 0).astype(F32)
    g_sel = jnp.min(jnp.where(gl == gmax, gi, float(N_GROUPS)), axis=0, keepdims=True)
    g_gate = 1.0 / jnp.sum(jnp.exp(gl - gmax), axis=0, keepdims=True)

    el = lt[EXPERT_ROW0:EXPERT_ROW0 + N_EXPERTS]
    ei_int = lax.broadcasted_iota(jnp.int32, el.shape, 0)
    ei = ei_int.astype(F32)
    grp = (ei_int >> EXPERT_GROUP_SHIFT).astype(F32)
    cand = jnp.where(grp == g_sel, el, -jnp.inf)
    v1 = jnp.max(cand, axis=0, keepdims=True)
    e1 = jnp.min(jnp.where(cand == v1, ei, float(N_EXPERTS)), axis=0, keepdims=True)
    cand2 = jnp.where(ei == e1, -jnp.inf, cand)
    v2 = jnp.max(cand2, axis=0, keepdims=True)
    e2 = jnp.min(jnp.where(cand2 == v2, ei, float(N_EXPERTS)), axis=0, keepdims=True)
    d = jnp.exp(v2 - v1)
    w1 = g_gate / (1.0 + d)
    w2 = g_gate * d / (1.0 + d)

    oh1 = (ei == e1).astype(F32)
    oh2 = (ei == e2).astype(F32)
    cnt = oh1 + oh2
    before = _dot(cnt.astype(BF16), tri_ref[...]) + cnt_ref[:, 0:1]
    r1 = jnp.sum(oh1 * before, axis=0, keepdims=True)
    r2 = jnp.sum(oh2 * before, axis=0, keepdims=True)
    cnt_ref[...] = cnt_ref[...] + jnp.sum(cnt, axis=1, keepdims=True)

    ri_ref[...] = jnp.concatenate([e1, e2, r1, r2, jnp.zeros((4, tm), F32)], axis=0).astype(jnp.int32)
    rw_ref[...] = jnp.concatenate([w1, w2, jnp.zeros((6, tm), F32)], axis=0)


def _outproj_router(attn, gla, x2, w_out, ln_w, wr_hl, wr_hi, rbias, tri):
    T = x2.shape[0]
    tm = OUT_ROWS
    row = lambda w: pl.BlockSpec((tm, w), lambda i: (i, 0))
    col = lambda r: pl.BlockSpec((r, tm), lambda i: (0, i))
    return pl.pallas_call(
        _outproj_body,
        grid=(T // tm,),
        in_specs=[row(ATTN_WIDTH), row(GLA_WIDTH), row(D_MODEL),
                  _resident(w_out.shape), _resident(ln_w.shape),
                  _resident(wr_hl.shape), _resident(wr_hi.shape), _resident(rbias.shape),
                  _resident(tri.shape)],
        out_specs=[row(D_MODEL), row(D_MODEL // 2), col(8), col(8),
                   pl.BlockSpec((N_EXPERTS, 128), lambda i: (0, 0))],
        out_shape=[jax.ShapeDtypeStruct((T, D_MODEL), F32),
                   jax.ShapeDtypeStruct((T, D_MODEL // 2), jnp.uint32),
                   jax.ShapeDtypeStruct((8, T), jnp.int32),
                   jax.ShapeDtypeStruct((8, T), F32),
                   jax.ShapeDtypeStruct((N_EXPERTS, 128), F32)],
        compiler_params=pltpu.CompilerParams(dimension_semantics=("arbitrary",),
                                             vmem_limit_bytes=VMEM_LIMIT),
        name="outproj_router",
    )(attn, gla, x2, w_out, ln_w, wr_hl, wr_hi, rbias, tri)


def _row_copy_start(src, s, dst, d, sem):
    pltpu.make_async_copy(src.at[pl.ds(s, 1)], dst.at[pl.ds(d, 1)], sem).start()


def _moe_body(be_ref, nu_ref, slot_ref, lo_ref, hi_ref, seq_ref, nxt_ref, xn_ref, wg_ref, wu_ref, wd_ref,
              y_ref, xbuf, xb_ref, tok_ref, wg_buf, wu_buf, wd_buf, wu_stage, wd_stage, sem, wsem, dsem):
    bm = MOE_ROWS
    b = pl.program_id(0)
    n_used = nu_ref[0]
    cur = b % MOE_ROW_BUFFERS
    expert = be_ref[b]
    wslot = seq_ref[expert] % 2
    first_of_expert = (b == 0) | (be_ref[jnp.maximum(b - 1, 0)] != expert)

    def weight_copies(e, slot):
        return [pltpu.make_async_copy(wg_ref.at[e], wg_buf.at[slot], wsem.at[slot]),
                pltpu.make_async_copy(wu_ref.at[e], wu_stage, dsem.at[0]),
                pltpu.make_async_copy(wd_ref.at[e], wd_stage, dsem.at[1])]

    def gather_start(block, buf, unrolled):
        def one(i):
            _row_copy_start(xn_ref, tok_ref[block * bm + i], xbuf.at[buf], i, sem.at[buf])
        if unrolled:
            for i in range(bm):
                one(i)
        else:
            lax.fori_loop(0, bm, lambda i, c: (one(i), c)[1], 0)

    def gather_wait(buf):
        pltpu.make_async_copy(xn_ref.at[pl.ds(0, bm)], xbuf.at[buf], sem.at[buf]).wait()

    @pl.when(b == 0)
    def _():
        for c in weight_copies(expert, wslot):
            c.start()
        def per_expert(e, c):
            def zero(s, c2):
                tok_ref[s] = 0
                return c2
            return lax.fori_loop(lo_ref[e], hi_ref[e], zero, c)
        lax.fori_loop(0, N_EXPERTS, per_expert, 0)

        def fill(t, c):
            for k in range(TOP_K):
                tok_ref[slot_ref[TOP_K * t + k]] = t
            return c
        lax.fori_loop(0, slot_ref.shape[0] // TOP_K, fill, 0, unroll=8)
        gather_start(0, 0, False)

        @pl.when(n_used > 1)
        def _():
            gather_start(1, 1, False)

    @pl.when((b < n_used) & first_of_expert)
    def _():
        for c in weight_copies(expert, wslot):
            c.wait()
        for r in range(0, D_MODEL, MOE_CAST_ROWS):
            wu_buf[wslot, r:r + MOE_CAST_ROWS, :] = wu_stage[r:r + MOE_CAST_ROWS, :].astype(BF16)
        for r in range(0, D_FF, MOE_CAST_ROWS):
            wd_buf[wslot, r:r + MOE_CAST_ROWS, :] = wd_stage[r:r + MOE_CAST_ROWS, :].astype(BF16)
        following = nxt_ref[expert]

        @pl.when(following >= 0)
        def _():
            for c in weight_copies(following, 1 - wslot):
                c.start()

    def stage_rows():
        gather_wait(cur)
        lo, hi = _unpack_bf16_pairs(xbuf[cur])
        xb_ref[:, :D_MODEL // 2] = lo.astype(BF16)
        xb_ref[:, D_MODEL // 2:] = hi.astype(BF16)

    def experts():
        x = xb_ref[...]
        hid = jax.nn.silu(_dot(x, wg_buf[wslot])) * _dot(x, wu_buf[wslot])
        y = _dot(hid.astype(BF16), wd_buf[wslot])
        y_ref[...] = _pack_bf16_pairs(y.astype(BF16))

    @pl.when(b + 2 < n_used)
    def _():
        stage_rows()
        gather_start(b + 2, (b + 2) % MOE_ROW_BUFFERS, True)
        experts()

    @pl.when((b < n_used) & (b + 2 >= n_used))
    def _():
        stage_rows()
        experts()

    @pl.when(b >= n_used)
    def _():
        y_ref[...] = jnp.zeros_like(y_ref)


def _moe(block_e, n_used, slots, pad_lo, pad_hi, expert_seq, expert_next, xn, wg, wu, wd, n_slots):
    bm = MOE_ROWS
    any_spec = pl.BlockSpec(memory_space=pl.ANY)
    grid_spec = pltpu.PrefetchScalarGridSpec(
        num_scalar_prefetch=7,
        grid=(n_slots // bm,),
        in_specs=[any_spec] * 4,
        out_specs=pl.BlockSpec((bm, D_MODEL // 2), lambda b, *_: (b, 0)),
        scratch_shapes=[pltpu.VMEM((MOE_ROW_BUFFERS, bm, D_MODEL // 2), jnp.uint32),
                        pltpu.VMEM((bm, D_MODEL), BF16),
                        pltpu.SMEM((n_slots,), jnp.int32),
                        pltpu.VMEM((2, D_MODEL, D_FF), BF16), pltpu.VMEM((2, D_MODEL, D_FF), BF16),
                        pltpu.VMEM((2, D_FF, D_MODEL), BF16),
                        pltpu.VMEM((D_MODEL, D_FF), F32), pltpu.VMEM((D_FF, D_MODEL), F32),
                        pltpu.SemaphoreType.DMA((MOE_ROW_BUFFERS,)), pltpu.SemaphoreType.DMA((2,)),
                        pltpu.SemaphoreType.DMA((2,))],
    )
    return pl.pallas_call(
        _moe_body,
        grid_spec=grid_spec,
        out_shape=jax.ShapeDtypeStruct((n_slots, D_MODEL // 2), jnp.uint32),
        compiler_params=pltpu.CompilerParams(dimension_semantics=("arbitrary",),
                                             vmem_limit_bytes=VMEM_LIMIT),
        name="moe",
    )(block_e, n_used, slots, pad_lo, pad_hi, expert_seq, expert_next, xn, wg, wu, wd)


def _combine_body(slot_ref, h_ref, w_ref, ys_ref, o_ref, g_ref, sem):
    tb = COMBINE_ROWS
    i = pl.program_id(0)
    cur = i % 2

    def gather_start(block, buf):
        def one(t, c):
            for k in range(TOP_K):
                _row_copy_start(ys_ref, slot_ref[TOP_K * (block * tb + t) + k], g_ref.at[buf, k], t,
                                sem.at[buf])
            return c
        lax.fori_loop(0, tb, one, 0, unroll=8)

    @pl.when(i == 0)
    def _():
        gather_start(0, 0)

    @pl.when(i + 1 < pl.num_programs(0))
    def _():
        gather_start(i + 1, 1 - cur)

    for k in range(TOP_K):
        pltpu.make_async_copy(ys_ref.at[pl.ds(0, tb)], g_ref.at[cur, k], sem.at[cur]).wait()
    lo0, hi0 = _unpack_bf16_pairs(g_ref[cur, 0])
    lo1, hi1 = _unpack_bf16_pairs(g_ref[cur, 1])
    w0, w1 = w_ref[:, 0:1], w_ref[:, 1:2]
    half = D_MODEL // 2
    o_ref[:, :half] = h_ref[:, :half] + w0 * lo0 + w1 * lo1
    o_ref[:, half:] = h_ref[:, half:] + w0 * hi0 + w1 * hi1


def _combine(slots, h, wcol, ys):
    T = h.shape[0]
    tb = COMBINE_ROWS
    grid_spec = pltpu.PrefetchScalarGridSpec(
        num_scalar_prefetch=1,
        grid=(T // tb,),
        in_specs=[pl.BlockSpec((tb, D_MODEL), lambda i, s: (i, 0)),
                  pl.BlockSpec((tb, TOP_K), lambda i, s: (i, 0)),
                  pl.BlockSpec(memory_space=pl.ANY)],
        out_specs=pl.BlockSpec((tb, D_MODEL), lambda i, s: (i, 0)),
        scratch_shapes=[pltpu.VMEM((2, TOP_K, tb, D_MODEL // 2), jnp.uint32),
                        pltpu.SemaphoreType.DMA((2,))],
    )
    return pl.pallas_call(
        _combine_body,
        grid_spec=grid_spec,
        out_shape=jax.ShapeDtypeStruct((T, D_MODEL), F32),
        compiler_params=pltpu.CompilerParams(dimension_semantics=("arbitrary",),
                                             vmem_limit_bytes=VMEM_LIMIT),
        name="combine",
    )(slots, h, wcol, ys)


def _rope_tables(T, rows):
    inv_freq = jnp.power(jnp.float32(ROPE_THETA),
                         -jnp.arange(ROPE_HALF, dtype=F32) * (2.0 / ROPE_DIM))

    def tables(pos):
        ang = pos.astype(F32)[:, None] * inv_freq[None, :]
        rest = (pos.shape[0], HEAD_DIM - ROPE_DIM)
        cos = jnp.concatenate([jnp.cos(ang), jnp.cos(ang), jnp.ones(rest, F32)], axis=1)
        sin = jnp.concatenate([jnp.sin(ang), jnp.sin(ang), jnp.zeros(rest, F32)], axis=1)
        return cos, sin

    step_cos, step_sin = tables(jnp.arange(T // rows) * rows)
    row_cos, row_sin = tables(jnp.arange(rows))
    return step_cos[:, None, :], step_sin[:, None, :], row_cos, row_sin


def _split_hi_lo(w):
    hi = w.astype(BF16)
    lo = (w - hi.astype(F32)).astype(BF16)
    return hi, lo


def _layer(x2, ln1_w, w_in, q_norm_w, k_norm_w, attn_sink, attn_out_norm_w, gate_up_f, gate_bias_f,
           gate_up_b, gate_bias_b, gla_out_norm_w, w_out, ln2_w, w_group, b_group, w_router, b_router,
           w_gate_e, w_up_e, w_down_e):
    T = x2.shape[0]
    row = lambda v: v.reshape(1, -1).astype(F32)

    c0, c1, c2, c3 = A_WIDTH, A_WIDTH + 2 * GLA_QK_WIDTH, A_WIDTH + 2 * GLA_QK_WIDTH + GLA_WIDTH, \
        A_WIDTH + 2 * GLA_QK_WIDTH + 2 * GLA_WIDTH
    wa, wqk, wv, wr, wl = (w_in[:, a:b].astype(BF16) for a, b in
                           ((0, c0), (c0, c1), (c1, c2), (c2, c3), (c3, w_in.shape[1])))
    pa, pqk, pv, pr, plr, wg_b = _inproj(x2, row(ln1_w), wa, wqk, wv, wr, wl, row(q_norm_w), row(k_norm_w),
                                         _rope_tables(T, PROJ_ROWS), _cast_rows_job(w_gate_e, T // PROJ_ROWS))

    attn = _attention(pa, attn_sink.astype(F32), row(attn_out_norm_w))

    C = GLA_CHUNK
    ones = jnp.ones((C, C), F32)
    zr = jnp.zeros((GLA_RANK, GLA_QK_WIDTH), F32)
    up_f = jnp.concatenate([gate_up_f.astype(F32), zr], axis=0)
    up_b = jnp.concatenate([zr, gate_up_b.astype(F32)], axis=0)

    def up_pieces(up):
        hi, lo = _split_hi_lo(up)
        return jnp.concatenate([hi, hi, lo], axis=0)

    o_f = _gla_pass(False, pqk, pv, plr, up_pieces(up_f), row(gate_bias_f), jnp.tril(ones).astype(BF16))
    gla = _gla_pass(True, pqk, pv, plr, up_pieces(up_b), row(gate_bias_b), jnp.triu(ones).astype(BF16),
                    extra=(o_f, pr, row(gla_out_norm_w)))

    wr_full = jnp.zeros((D_MODEL, ROUTER_ROWS), F32)
    wr_full = wr_full.at[:, :N_GROUPS].set(w_group.astype(F32))
    wr_full = wr_full.at[:, EXPERT_ROW0:EXPERT_ROW0 + N_EXPERTS].set(w_router.astype(F32))
    wr_hi, wr_lo = _split_hi_lo(wr_full)
    rbias = jnp.zeros((ROUTER_ROWS, 1), F32)
    rbias = rbias.at[:N_GROUPS, 0].set(b_group.astype(F32))
    rbias = rbias.at[EXPERT_ROW0:EXPERT_ROW0 + N_EXPERTS, 0].set(b_router.astype(F32))
    tm = OUT_ROWS
    earlier = jnp.triu(jnp.ones((tm, tm), F32), k=1).astype(BF16)
    h, xn, r_int, r_w, counts = _outproj_router(
        attn, gla, x2, w_out.astype(BF16), row(ln2_w),
        jnp.concatenate([wr_hi, wr_lo], axis=1), wr_hi, rbias, earlier)

    bm = MOE_ROWS
    n_blocks = (T * TOP_K) // bm + N_EXPERTS
    cnt = counts[:, 0].astype(jnp.int32)
    padded = (cnt + bm - 1) // bm * bm
    pad_end = jnp.cumsum(padded)
    pad_start = pad_end - padded
    n_used = (pad_end[-1] // bm).astype(jnp.int32).reshape(1)
    block_e = jnp.minimum(jnp.sum(pad_end[None, :] <= (jnp.arange(n_blocks) * bm)[:, None], axis=1),
                          N_EXPERTS - 1).astype(jnp.int32)
    e_sel = r_int[0:TOP_K].T
    start_sel = jnp.sum(jnp.where(e_sel[..., None] == jnp.arange(N_EXPERTS), pad_start, 0), axis=-1)
    slots = (start_sel + r_int[TOP_K:2 * TOP_K].T).reshape(-1).astype(jnp.int32)

    nonempty = cnt > 0
    expert_seq = (jnp.cumsum(nonempty) - 1).astype(jnp.int32)
    ids = jnp.arange(N_EXPERTS)
    later = (ids[None, :] > ids[:, None]) & nonempty[None, :]
    expert_next = jnp.where(later.any(axis=1), jnp.argmax(later, axis=1), -1).astype(jnp.int32)
    ys = _moe(block_e, n_used, slots, (pad_start + cnt).astype(jnp.int32), pad_end.astype(jnp.int32),
              expert_seq, expert_next, xn,
              wg_b.reshape(w_gate_e.shape), w_up_e.astype(F32), w_down_e.astype(F32), n_blocks * bm)
    return _combine(slots, h, r_w[0:TOP_K].T, ys)


def kernel(x, ln1_w, w_in, q_norm_w, k_norm_w, attn_sink, attn_out_norm_w, gla_gate_up_f, gla_gate_bias_f,
           gla_gate_up_b, gla_gate_bias_b, gla_out_norm_w, w_out, ln2_w, w_group, b_group, w_router,
           b_router, w_gate_e, w_up_e, w_down_e):
    B, S, D = x.shape
    h = x.reshape(B * S, D)
    assert B == 1
    for l in range(ln1_w.shape[0]):
        h = _layer(h, ln1_w[l], w_in[l], q_norm_w[l], k_norm_w[l], attn_sink[l], attn_out_norm_w[l],
                   gla_gate_up_f[l], gla_gate_bias_f[l], gla_gate_up_b[l], gla_gate_bias_b[l],
                   gla_out_norm_w[l], w_out[l], ln2_w[l], w_group[l], b_group[l], w_router[l],
                   b_router[l], w_gate_e[l], w_up_e[l], w_down_e[l])
    return h.reshape(B, S, D)
```
